```python
import math
import jax, jax.numpy as jnp
from jax import lax
import numpy as np

D_MODEL = 1024
BATCH = 4
SEQ = 4096
DEPTH = 1
DEC_BATCH = 128
DEC_SEQ = 4
PAST_LEN = 2048
PAGE_SIZE = 128

MIX_WIDTH = D_MODEL
ATTN_WIDTH = MIX_WIDTH // 2
SSM_WIDTH = MIX_WIDTH - ATTN_WIDTH
HEAD_DIM = 64
N_HEADS = ATTN_WIDTH // HEAD_DIM
SSM_GROUP = 16
N_GROUPS = SSM_WIDTH // SSM_GROUP
STATE_DIM = 64
D_FF = 4 * D_MODEL
PLE_DIM = 256
Q_BLOCK = 128
IN_WIDTH = 3 * ATTN_WIDTH + N_HEADS + SSM_WIDTH
ALPHA = (2.0 * DEPTH) ** 0.25
BETA = (8.0 * DEPTH) ** -0.25
LN_EPS = 1e-5
FGATE_BIAS = 3.0
DT_MIN = 1e-3
DT_MAX = 1e-1
NEG_INF = -1e30

kernel_name = "fox_s5_parallel_heads_deepnorm_step"


def layer_norm(x, g, b):
    xf = x.astype(jnp.float32)
    mu = jnp.mean(xf, axis=-1, keepdims=True)
    var = jnp.mean(jnp.square(xf - mu), axis=-1, keepdims=True)
    return ((xf - mu) * lax.rsqrt(var + LN_EPS) * g + b).astype(x.dtype)


def project_in(h, w_in, b_f):
    z = h @ w_in
    q, k, v, fl, u = jnp.split(z, [ATTN_WIDTH, 2 * ATTN_WIDTH, 3 * ATTN_WIDTH, 3 * ATTN_WIDTH + N_HEADS], axis=-1)
    shp = h.shape[:-1] + (N_HEADS, HEAD_DIM)
    logf = jax.nn.log_sigmoid((fl + b_f).astype(jnp.float32))
    return q.reshape(shp), k.reshape(shp), v.reshape(shp), logf, u


def fox_attend(q, k, v, cq, ck, q_pos):
    s = jnp.einsum('bqhd,bkhd->bhqk', q, k).astype(jnp.float32) * (HEAD_DIM ** -0.5)
    bias = jnp.transpose(cq, (0, 2, 1))[..., :, None] - jnp.transpose(ck, (0, 2, 1))[..., None, :]
    k_pos = jnp.arange(k.shape[1], dtype=jnp.int32)
    mask = k_pos[None, :] <= q_pos[:, None]
    s = jnp.where(mask, s + bias, NEG_INF)
    p = jax.nn.softmax(s, axis=-1).astype(v.dtype)
    return jnp.einsum('bhqk,bkhd->bqhd', p, v)


def fox_prompt(q, k, v, logf):
    b, t = q.shape[:2]
    c = jnp.cumsum(logf, axis=1)
    nb = t // Q_BLOCK
    qb = q.reshape(b, nb, Q_BLOCK, N_HEADS, HEAD_DIM).swapaxes(0, 1)
    cb = c.reshape(b, nb, Q_BLOCK, N_HEADS).swapaxes(0, 1)
    starts = jnp.arange(nb, dtype=jnp.int32) * Q_BLOCK

    def block(args):
        q_blk, c_blk, start = args
        return fox_attend(q_blk, k, v, c_blk, c, start + jnp.arange(Q_BLOCK, dtype=jnp.int32))

    o = lax.map(block, (qb, cb, starts))
    return o.swapaxes(0, 1).reshape(b, t, ATTN_WIDTH)


def fox_sample(q, k, v, logf, cache_k, cache_v, cache_logf, page_table):
    db, t = q.shape[:2]
    past_len = page_table.shape[1] * cache_k.shape[1]
    pk = cache_k[page_table].reshape(db, past_len, N_HEADS, HEAD_DIM)
    pv = cache_v[page_table].reshape(db, past_len, N_HEADS, HEAD_DIM)
    pf = cache_logf[page_table].reshape(db, past_len, N_HEADS).astype(jnp.float32)
    k_all = jnp.concatenate([pk, k.astype(pk.dtype)], axis=1)
    v_all = jnp.concatenate([pv, v.astype(pv.dtype)], axis=1)
    c = jnp.cumsum(jnp.concatenate([pf, logf], axis=1), axis=1)
    q_pos = past_len + jnp.arange(t, dtype=jnp.int32)
    o = fox_attend(q, k_all, v_all, c[:, past_len:], c, q_pos)
    return o.reshape(db, t, ATTN_WIDTH).astype(q.dtype)


def s5_discretize(lam_re, lam_im, log_dt, b_re, b_im):
    dt = jnp.exp(log_dt.astype(jnp.float32))[:, None]
    lr = lam_re.astype(jnp.float32)
    li = lam_im.astype(jnp.float32)
    mag = jnp.exp(lr * dt)
    a_re = mag * jnp.cos(li * dt)
    a_im = mag * jnp.sin(li * dt)
    nr = a_re - 1.0
    ni = a_im
    den = lr * lr + li * li
    coef_re = (nr * lr + ni * li) / den
    coef_im = (ni * lr - nr * li) / den
    br = b_re.astype(jnp.float32)
    bi = b_im.astype(jnp.float32)
    bb_re = coef_re[..., None] * br - coef_im[..., None] * bi
    bb_im = coef_re[..., None] * bi + coef_im[..., None] * br
    return a_re, a_im, bb_re, bb_im


def s5_combine(e1, e2):
    a1r, a1i, b1r, b1i = e1
    a2r, a2i, b2r, b2i = e2
    return (a2r * a1r - a2i * a1i,
            a2r * a1i + a2i * a1r,
            a2r * b1r - a2i * b1i + b2r,
            a2r * b1i + a2i * b1r + b2i)


def s5_mixer(u, h0_re, h0_im, lam_re, lam_im, log_dt, b_re, b_im, c_re, c_im, d_skip, w_glu, b_glu):
    b, t = u.shape[:2]
    ug = u.reshape(b, t, N_GROUPS, SSM_GROUP).astype(jnp.float32)
    a_re, a_im, bb_re, bb_im = s5_discretize(lam_re, lam_im, log_dt, b_re, b_im)
    bu_re = jnp.einsum('gph,btgh->btgp', bb_re, ug)
    bu_im = jnp.einsum('gph,btgh->btgp', bb_im, ug)
    ar = jnp.broadcast_to(a_re, bu_re.shape)
    ai = jnp.broadcast_to(a_im, bu_re.shape)
    acr, aci, bcr, bci = lax.associative_scan(s5_combine, (ar, ai, bu_re, bu_im), axis=1)
    h0r = h0_re.astype(jnp.float32)[:, None]
    h0i = h0_im.astype(jnp.float32)[:, None]
    hr = acr * h0r - aci * h0i + bcr
    hi = acr * h0i + aci * h0r + bci
    y = (jnp.einsum('ghp,btgp->btgh', c_re.astype(jnp.float32), hr)
         - jnp.einsum('ghp,btgp->btgh', c_im.astype(jnp.float32), hi)
         + d_skip.astype(jnp.float32) * ug)
    y = jax.nn.gelu(y.reshape(b, t, SSM_WIDTH))
    y = y * jax.nn.sigmoid(y @ w_glu.astype(jnp.float32) + b_glu.astype(jnp.float32))
    return y.astype(u.dtype), hr[:, -1], hi[:, -1]


def post_mixer(h, mix, p, w_out, ln1_g, ln1_b, w_up, w_down, w_pe, w_pg, b_pg, ln2_g, ln2_b):
    h1 = layer_norm(ALPHA * h + mix @ w_out, ln1_g, ln1_b)
    ff = jnp.square(jax.nn.relu(h1 @ w_up)) @ w_down
    e = jax.nn.sigmoid(h1 @ w_pg + b_pg) * (p @ w_pe)
    return layer_norm(ALPHA * h1 + ff + e, ln2_g, ln2_b)


def setup_inputs(seed: int = 0) -> dict:
    key = jax.random.key(seed)
    ks = iter(jax.random.split(key, 40))
    f32 = jnp.float32
    n_pages = PAST_LEN // PAGE_SIZE
    n_used = DEC_BATCH * n_pages
    n_phys = n_used + n_used // 4

    def nrm(shape, scale=1.0):
        return jax.random.normal(next(ks), shape, f32) * scale

    x_prompt = nrm((BATCH, SEQ, D_MODEL))
    x_sample = nrm((DEC_BATCH, DEC_SEQ, D_MODEL))
    cache_k = nrm((DEPTH, n_phys, PAGE_SIZE, N_HEADS, HEAD_DIM))
    cache_v = nrm((DEPTH, n_phys, PAGE_SIZE, N_HEADS, HEAD_DIM))
    cache_logf = jax.nn.log_sigmoid(FGATE_BIAS + nrm((DEPTH, n_phys, PAGE_SIZE, N_HEADS)))
    state_re = nrm((DEPTH, DEC_BATCH, N_GROUPS, STATE_DIM), 0.5)
    state_im = nrm((DEPTH, DEC_BATCH, N_GROUPS, STATE_DIM), 0.5)
    page_table = jax.random.permutation(next(ks), n_phys)[:n_used].reshape(DEC_BATCH, n_pages).astype(jnp.int32)
    p_prompt = nrm((DEPTH, BATCH, SEQ, PLE_DIM))
    p_sample = nrm((DEPTH, DEC_BATCH, DEC_SEQ, PLE_DIM))

    ln_in_g = 1.0 + nrm((D_MODEL,), 0.02)
    ln_in_b = nrm((D_MODEL,), 0.02)
    w_in = nrm((DEPTH, D_MODEL, IN_WIDTH), D_MODEL ** -0.5)
    b_f = FGATE_BIAS + nrm((DEPTH, N_HEADS), 0.1)
    n_idx = jnp.arange(STATE_DIM, dtype=f32)
    lam_re = -0.5 + nrm((DEPTH, N_GROUPS, STATE_DIM), 0.01)
    lam_im = math.pi * n_idx + nrm((DEPTH, N_GROUPS, STATE_DIM), 0.01)
    log_dt = jax.random.uniform(next(ks), (DEPTH, N_GROUPS), f32, math.log(DT_MIN), math.log(DT_MAX))
    b_re = nrm((DEPTH, N_GROUPS, STATE_DIM, SSM_GROUP), (2.0 * SSM_GROUP) ** -0.5)
    b_im = nrm((DEPTH, N_GROUPS, STATE_DIM, SSM_GROUP), (2.0 * SSM_GROUP) ** -0.5)
    c_re = nrm((DEPTH, N_GROUPS, SSM_GROUP, STATE_DIM), (2.0 * STATE_DIM) ** -0.5)
    c_im = nrm((DEPTH, N_GROUPS, SSM_GROUP, STATE_DIM), (2.0 * STATE_DIM) ** -0.5)
    d_skip = nrm((DEPTH, N_GROUPS, SSM_GROUP))
    w_glu = nrm((DEPTH, SSM_WIDTH, SSM_WIDTH), SSM_WIDTH ** -0.5)
    b_glu = nrm((DEPTH, SSM_WIDTH), 0.01)
    w_out = nrm((DEPTH, MIX_WIDTH, D_MODEL), BETA * MIX_WIDTH ** -0.5)
    ln1_g = 1.0 + nrm((DEPTH, D_MODEL), 0.02)
    ln1_b = nrm((DEPTH, D_MODEL), 0.02)
    w_up = nrm((DEPTH, D_MODEL, D_FF), D_MODEL ** -0.5)
    w_down = nrm((DEPTH, D_FF, D_MODEL), BETA * D_FF ** -0.5)
    w_pe = nrm((DEPTH, PLE_DIM, D_MODEL), BETA * PLE_DIM ** -0.5)
    w_pg = nrm((DEPTH, D_MODEL, D_MODEL), D_MODEL ** -0.5)
    b_pg = nrm((DEPTH, D_MODEL), 0.01)
    ln2_g = 1.0 + nrm((DEPTH, D_MODEL), 0.02)
    ln2_b = nrm((DEPTH, D_MODEL), 0.02)
    return {"x_prompt": x_prompt, "x_sample": x_sample, "cache_k": cache_k, "cache_v": cache_v,
            "cache_logf": cache_logf, "state_re": state_re, "state_im": state_im, "page_table": page_table,
            "p_prompt": p_prompt, "p_sample": p_sample, "ln_in_g": ln_in_g, "ln_in_b": ln_in_b,
            "w_in": w_in, "b_f": b_f, "lam_re": lam_re, "lam_im": lam_im, "log_dt": log_dt,
            "b_re": b_re, "b_im": b_im, "c_re": c_re, "c_im": c_im, "d_skip": d_skip,
            "w_glu": w_glu, "b_glu": b_glu, "w_out": w_out, "ln1_g": ln1_g, "ln1_b": ln1_b,
            "w_up": w_up, "w_down": w_down, "w_pe": w_pe, "w_pg": w_pg, "b_pg": b_pg,
            "ln2_g": ln2_g, "ln2_b": ln2_b}


def reference(x_prompt, x_sample, cache_k, cache_v, cache_logf, state_re, state_im, page_table,
              p_prompt, p_sample, ln_in_g, ln_in_b, w_in, b_f, lam_re, lam_im, log_dt,
              b_re, b_im, c_re, c_im, d_skip, w_glu, b_glu, w_out, ln1_g, ln1_b,
              w_up, w_down, w_pe, w_pg, b_pg, ln2_g, ln2_b):
    hp = layer_norm(x_prompt, ln_in_g, ln_in_b)
    hs = layer_norm(x_sample, ln_in_g, ln_in_b)
    kp_l, vp_l, fp_l, srp_l, sip_l = [], [], [], [], []
    ks_l, vs_l, fs_l, srs_l, sis_l = [], [], [], [], []
    for i in range(DEPTH):
        ssm_args = (lam_re[i], lam_im[i], log_dt[i], b_re[i], b_im[i], c_re[i], c_im[i], d_skip[i], w_glu[i], b_glu[i])
        post_args = (w_out[i], ln1_g[i], ln1_b[i], w_up[i], w_down[i], w_pe[i], w_pg[i], b_pg[i], ln2_g[i], ln2_b[i])
        q, k, v, logf, u = project_in(hp, w_in[i], b_f[i])
        att = fox_prompt(q, k, v, logf)
        h0 = jnp.zeros((hp.shape[0], N_GROUPS, STATE_DIM), jnp.float32)
        ssm, sr, si = s5_mixer(u, h0, h0, *ssm_args)
        hp = post_mixer(hp, jnp.concatenate([att, ssm], axis=-1), p_prompt[i], *post_args)
        kp_l.append(k); vp_l.append(v); fp_l.append(logf); srp_l.append(sr); sip_l.append(si)
        q, k, v, logf, u = project_in(hs, w_in[i], b_f[i])
        att = fox_sample(q, k, v, logf, cache_k[i], cache_v[i], cache_logf[i], page_table)
        ssm, sr, si = s5_mixer(u, state_re[i], state_im[i], *ssm_args)
        hs = post_mixer(hs, jnp.concatenate([att, ssm], axis=-1), p_sample[i], *post_args)
        ks_l.append(k); vs_l.append(v); fs_l.append(logf); srs_l.append(sr); sis_l.append(si)
    return (hp, hs,
            jnp.stack(kp_l), jnp.stack(vp_l), jnp.stack(fp_l), jnp.stack(srp_l), jnp.stack(sip_l),
            jnp.stack(ks_l), jnp.stack(vs_l), jnp.stack(fs_l), jnp.stack(srs_l), jnp.stack(sis_l))
```

```python
import functools
import math

import jax
import jax.numpy as jnp
from jax import lax
from jax.experimental import pallas as pl
from jax.experimental.pallas import tpu as pltpu

F32 = jnp.float32
BF16 = jnp.bfloat16

D_MODEL = 1024
ATTN_WIDTH = 512
SSM_WIDTH = 512
HEAD_DIM = 64
N_HEADS = 8
SSM_GROUP = 16
N_GROUPS = 32
STATE_DIM = 64
D_FF = 4096
PLE_DIM = 256
PAGE_SIZE = 128
ALPHA = 2.0 ** 0.25
LN_EPS = 1e-5
NEG_INF = -1e30
QK_SCALE = HEAD_DIM ** -0.5

VMEM_LIMIT_BYTES = 56 * 1024 * 1024

_NT = (((1,), (1,)), ((), ()))


def _params(*sem):
    return pltpu.CompilerParams(dimension_semantics=sem, vmem_limit_bytes=VMEM_LIMIT_BYTES)


def _const_spec(shape):
    return pl.BlockSpec(shape, lambda *_: (0,) * len(shape))


def _layer_norm(x, g, b):
    mu = jnp.mean(x, axis=-1, keepdims=True)
    xc = x - mu
    var = jnp.mean(xc * xc, axis=-1, keepdims=True)
    return xc * lax.rsqrt(var + LN_EPS) * g + b


def _log_sigmoid(x):
    return jnp.minimum(x, 0.0) - jnp.log1p(jnp.exp(-jnp.abs(x)))


def _sigmoid(x):
    return 1.0 / (1.0 + jnp.exp(-x))


def _gelu_tanh(x):
    return 0.5 * x * (1.0 + jnp.tanh(math.sqrt(2.0 / math.pi) * (x + 0.044715 * (x * x * x))))


def _dot(a, b):
    return jnp.dot(a, b, preferred_element_type=F32)


def _dot_nt(a, b):
    return lax.dot_general(a, b, _NT, preferred_element_type=F32)


def _split3(x):
    hi = x.astype(BF16)
    r1 = x - hi.astype(F32)
    mid = r1.astype(BF16)
    lo = (r1 - mid.astype(F32)).astype(BF16)
    return hi, mid, lo


def _dot_exact01(x, m01):
    hi, mid, lo = _split3(x)
    return _dot(hi, m01) + _dot(mid, m01) + _dot(lo, m01)


def _in_proj_kernel(*refs, tm, attn_layouts):
    if attn_layouts:
        (x_ref, g_ref, b_ref, wq_ref, wk_ref, wv_ref, wu_ref, wf_ref, bf_ref, wkT_ref, wfT_ref, bfc_ref, tri_ref,
         q_ref, k_ref, v_ref, lf_ref, u_ref, vb_ref, kT_ref, cT_ref, carry_ref) = refs
    else:
        (x_ref, g_ref, b_ref, wq_ref, wk_ref, wv_ref, wu_ref, wf_ref, bf_ref,
         q_ref, k_ref, v_ref, lf_ref, u_ref) = refs
    hb = _layer_norm(x_ref[0], g_ref[...], b_ref[...]).astype(BF16)
    q_ref[0] = (_dot(hb, wq_ref[...]) * QK_SCALE).astype(q_ref.dtype)
    k_ref[0] = _dot(hb, wk_ref[...])
    v = _dot(hb, wv_ref[...])
    v_ref[0] = v
    u_ref[0] = _dot(hb, wu_ref[...])
    lf_ref[0] = _log_sigmoid(_dot(hb, wf_ref[...]) + bf_ref[...])
    if attn_layouts:
        vb_ref[0] = v.astype(BF16)
        kT_ref[0] = _dot_nt(wkT_ref[...], hb).astype(BF16)
        lfT = _log_sigmoid(_dot_nt(wfT_ref[...], hb) + bfc_ref[...])

        @pl.when(pl.program_id(1) == 0)
        def _():
            carry_ref[...] = jnp.zeros_like(carry_ref)

        c = carry_ref[...] + _dot_exact01(lfT, tri_ref[...])
        cT_ref[0] = c
        carry_ref[...] = c[:, tm - 1:tm]


def _in_proj(x3, ln_g, ln_b, w_in, b_f, *, tm, attn_layouts, q_dtype):
    nb, t, _ = x3.shape
    wq = w_in[:, :ATTN_WIDTH].astype(BF16)
    wk = w_in[:, ATTN_WIDTH:2 * ATTN_WIDTH].astype(BF16)
    wv = w_in[:, 2 * ATTN_WIDTH:3 * ATTN_WIDTH].astype(BF16)
    wf = w_in[:, 3 * ATTN_WIDTH:3 * ATTN_WIDTH + N_HEADS].astype(BF16)
    wu = w_in[:, 3 * ATTN_WIDTH + N_HEADS:].astype(BF16)
    tok = lambda w: pl.BlockSpec((1, tm, w), lambda b, i: (b, i, 0))
    ins = [x3, ln_g.reshape(1, -1), ln_b.reshape(1, -1), wq, wk, wv, wu, wf, b_f.reshape(1, N_HEADS)]
    in_specs = [tok(D_MODEL), _const_spec((1, D_MODEL)), _const_spec((1, D_MODEL))]
    in_specs += [_const_spec((D_MODEL, ATTN_WIDTH))] * 3 + [_const_spec((D_MODEL, SSM_WIDTH))]
    in_specs += [_const_spec((D_MODEL, N_HEADS)), _const_spec((1, N_HEADS))]
    out_shape = [jax.ShapeDtypeStruct((nb, t, ATTN_WIDTH), q_dtype),
                 jax.ShapeDtypeStruct((nb, t, ATTN_WIDTH), F32),
                 jax.ShapeDtypeStruct((nb, t, ATTN_WIDTH), F32),
                 jax.ShapeDtypeStruct((nb, t, N_HEADS), F32),
                 jax.ShapeDtypeStruct((nb, t, SSM_WIDTH), F32)]
    out_specs = [tok(ATTN_WIDTH), tok(ATTN_WIDTH), tok(ATTN_WIDTH), tok(N_HEADS), tok(SSM_WIDTH)]
    scratch = []
    if attn_layouts:
        tri = (jnp.arange(tm)[:, None] <= jnp.arange(tm)[None, :]).astype(BF16)
        ins += [wk.T, wf.T, b_f.reshape(N_HEADS, 1), tri]
        in_specs += [_const_spec((ATTN_WIDTH, D_MODEL)), _const_spec((N_HEADS, D_MODEL)),
                     _const_spec((N_HEADS, 1)), _const_spec((tm, tm))]
        out_shape += [jax.ShapeDtypeStruct((nb, t, ATTN_WIDTH), BF16),
                      jax.ShapeDtypeStruct((nb, ATTN_WIDTH, t), BF16),
                      jax.ShapeDtypeStruct((nb, N_HEADS, t), F32)]
        out_specs += [tok(ATTN_WIDTH),
                      pl.BlockSpec((1, ATTN_WIDTH, tm), lambda b, i: (b, 0, i)),
                      pl.BlockSpec((1, N_HEADS, tm), lambda b, i: (b, 0, i))]
        scratch = [pltpu.VMEM((N_HEADS, 1), F32)]
    return pl.pallas_call(
        functools.partial(_in_proj_kernel, tm=tm, attn_layouts=attn_layouts),
        grid=(nb, t // tm), in_specs=in_specs, out_specs=out_specs, out_shape=out_shape,
        scratch_shapes=scratch, compiler_params=_params("arbitrary", "arbitrary"),
        name="in_proj_prompt" if attn_layouts else "in_proj_sample")(*ins)


def _fox_prompt_kernel(q_ref, kT_ref, v_ref, cT_ref, o_ref, *, tq):
    qi = pl.program_id(1)
    q0 = pl.multiple_of(qi * tq, tq)
    rows = lax.broadcasted_iota(jnp.int32, (tq, tq), 0)
    cols = lax.broadcasted_iota(jnp.int32, (tq, tq), 1)
    for h in range(N_HEADS):
        hs = slice(h * HEAD_DIM, (h + 1) * HEAD_DIM)
        qh = q_ref[0, :, hs]
        c_ref0 = cT_ref[0, h:h + 1, pl.ds(q0, tq)][:, 0:1]

        def tile(kt, carry, diagonal, hs=hs, qh=qh, c_ref0=c_ref0, h=h):
            m, l, acc = carry
            ks = pl.multiple_of(kt * tq, tq)
            s = _dot(qh, kT_ref[0, hs, pl.ds(ks, tq)])
            s = s - (cT_ref[0, h:h + 1, pl.ds(ks, tq)] - c_ref0)
            if diagonal:
                s = jnp.where(cols <= rows, s, NEG_INF)
            m_new = jnp.maximum(m, jnp.max(s, axis=-1, keepdims=True))
            alpha = jnp.exp(m - m_new)
            p = jnp.exp(s - m_new)
            l = alpha * l + jnp.sum(p, axis=-1, keepdims=True)
            acc = alpha * acc + _dot(p.astype(BF16), v_ref[0, pl.ds(ks, tq), hs])
            return m_new, l, acc

        init = (jnp.full((tq, 1), NEG_INF, F32), jnp.zeros((tq, 1), F32), jnp.zeros((tq, HEAD_DIM), F32))
        carry = lax.fori_loop(0, qi, functools.partial(tile, diagonal=False), init)
        _, l, acc = tile(qi, carry, True)
        o_ref[0, :, hs] = (acc / l).astype(o_ref.dtype)


def _fox_prompt(q, kT, vb, cT, *, tq):
    nb, t, _ = q.shape
    return pl.pallas_call(
        functools.partial(_fox_prompt_kernel, tq=tq),
        grid=(nb, t // tq),
        in_specs=[pl.BlockSpec((1, tq, ATTN_WIDTH), lambda b, i: (b, i, 0)),
                  pl.BlockSpec((1, ATTN_WIDTH, t), lambda b, i: (b, 0, 0)),
                  pl.BlockSpec((1, t, ATTN_WIDTH), lambda b, i: (b, 0, 0)),
                  pl.BlockSpec((1, N_HEADS, t), lambda b, i: (b, 0, 0))],
        out_specs=pl.BlockSpec((1, tq, ATTN_WIDTH), lambda b, i: (b, i, 0)),
        out_shape=jax.ShapeDtypeStruct((nb, t, ATTN_WIDTH), BF16),
        compiler_params=_params("arbitrary", "arbitrary"), name="fox_prompt")(q, kT, vb, cT)


def _fox_sample_kernel(pt_ref, q_ref, kn_ref, vn_ref, lfn_ref, sl_ref, *refs, npg, nq):
    k_refs, v_refs, pf_refs = refs[:npg], refs[npg:2 * npg], refs[2 * npg:3 * npg]
    o_ref, kpad_ref, vpad_ref, m_ref, l_ref, acc_ref, tail_ref = refs[3 * npg:]
    step = pl.program_id(1)
    qrow = lax.broadcasted_iota(jnp.int32, (8, PAGE_SIZE), 0)
    kcol = lax.broadcasted_iota(jnp.int32, (8, PAGE_SIZE), 1)
    q8 = q_ref[0].astype(BF16)

    def suffix(pf):
        sfx = _dot_exact01(pf, sl_ref[...])
        return sfx, sfx[:, 0:1] + pf[:, 0:1]

    def attend(pages):
        for h in range(N_HEADS):
            hs = slice(h * HEAD_DIM, (h + 1) * HEAD_DIM)
            qh = q8[:, hs]
            ss = []
            for k_of, _, e, mask in pages:
                s = _dot_nt(qh, k_of(h).astype(BF16)) + e[h:h + 1, :]
                ss.append(s if mask is None else jnp.where(mask, s, NEG_INF))
            m_old = m_ref[h]
            m_new = m_old
            for s in ss:
                m_new = jnp.maximum(m_new, jnp.max(s, axis=-1, keepdims=True))
            alpha = jnp.exp(m_old - m_new)
            l = alpha * l_ref[h]
            acc = alpha * acc_ref[h]
            for s, (_, v_of, _, _) in zip(ss, pages):
                p = jnp.exp(s - m_new)
                l = l + jnp.sum(p, axis=-1, keepdims=True)
                acc = acc + _dot(p.astype(BF16), v_of(h).astype(BF16))
            m_ref[h] = m_new
            l_ref[h] = l
            acc_ref[h] = acc

    @pl.when(step == 0)
    def _():
        m_ref[...] = jnp.full_like(m_ref, NEG_INF)
        l_ref[...] = jnp.zeros_like(l_ref)
        acc_ref[...] = jnp.zeros_like(acc_ref)
        kpad_ref[...] = jnp.zeros_like(kpad_ref)
        vpad_ref[...] = jnp.zeros_like(vpad_ref)
        kpad_ref[0:nq, :] = kn_ref[0]
        vpad_ref[0:nq, :] = vn_ref[0]
        e_new, total = suffix(lfn_ref[0])
        tail_ref[...] = total
        causal = kcol <= qrow
        attend([(lambda h: kpad_ref[:, h * HEAD_DIM:(h + 1) * HEAD_DIM],
                 lambda h: vpad_ref[:, h * HEAD_DIM:(h + 1) * HEAD_DIM], e_new, causal)])

    pages = []
    tail = tail_ref[...]
    for j in reversed(range(npg)):
        sfx, total = suffix(pf_refs[j][0, 0])
        pages.append((lambda h, r=k_refs[j]: r[0, 0, :, h, :], lambda h, r=v_refs[j]: r[0, 0, :, h, :],
                      sfx + tail, None))
        tail = tail + total
    tail_ref[...] = tail
    attend(pages)

    @pl.when(step == pl.num_programs(1) - 1)
    def _():
        for h in range(N_HEADS):
            o_ref[0, :, h * HEAD_DIM:(h + 1) * HEAD_DIM] = acc_ref[h] / l_ref[h]


def _fox_sample(q, k_new, v_new, logf_new, cache_k, cache_v, cache_logf, page_table, *, npg):
    db, nq, _ = q.shape
    n_pages = page_table.shape[1]
    n_steps = n_pages // npg
    pf_t = jnp.swapaxes(cache_logf, 2, 3)
    lfn_t = jnp.pad(jnp.swapaxes(logf_new, 1, 2), ((0, 0), (0, 0), (0, PAGE_SIZE - nq)))
    strict_lower = (jnp.arange(PAGE_SIZE)[:, None] > jnp.arange(PAGE_SIZE)[None, :]).astype(BF16)
    q8 = jnp.pad(q, ((0, 0), (0, 8 - nq), (0, 0)))

    def page_of(j):
        return lambda b, s, pt: (0, pt[b, n_pages - (s + 1) * npg + j], 0, 0, 0)

    def page_of4(j):
        return lambda b, s, pt: (0, pt[b, n_pages - (s + 1) * npg + j], 0, 0)

    seq = lambda w: pl.BlockSpec((1, nq, w), lambda b, s, pt: (b, 0, 0))
    in_specs = [pl.BlockSpec((1, 8, ATTN_WIDTH), lambda b, s, pt: (b, 0, 0)), seq(ATTN_WIDTH), seq(ATTN_WIDTH),
                pl.BlockSpec((1, N_HEADS, PAGE_SIZE), lambda b, s, pt: (b, 0, 0)),
                pl.BlockSpec((PAGE_SIZE, PAGE_SIZE), lambda b, s, pt: (0, 0))]
    in_specs += [pl.BlockSpec((1, 1, PAGE_SIZE, N_HEADS, HEAD_DIM), page_of(j)) for j in range(npg)]
    in_specs += [pl.BlockSpec((1, 1, PAGE_SIZE, N_HEADS, HEAD_DIM), page_of(j)) for j in range(npg)]
    in_specs += [pl.BlockSpec((1, 1, N_HEADS, PAGE_SIZE), page_of4(j)) for j in range(npg)]
    return pl.pallas_call(
        functools.partial(_fox_sample_kernel, npg=npg, nq=nq),
        grid_spec=pltpu.PrefetchScalarGridSpec(
            num_scalar_prefetch=1, grid=(db, n_steps), in_specs=in_specs,
            out_specs=pl.BlockSpec((1, 8, ATTN_WIDTH), lambda b, s, pt: (b, 0, 0)),
            scratch_shapes=[pltpu.VMEM((PAGE_SIZE, ATTN_WIDTH), F32), pltpu.VMEM((PAGE_SIZE, ATTN_WIDTH), F32),
                            pltpu.VMEM((N_HEADS, 8, 1), F32), pltpu.VMEM((N_HEADS, 8, 1), F32),
                            pltpu.VMEM((N_HEADS, 8, HEAD_DIM), F32), pltpu.VMEM((N_HEADS, 1), F32)]),
        out_shape=jax.ShapeDtypeStruct((db, 8, ATTN_WIDTH), F32),
        compiler_params=_params("arbitrary", "arbitrary"), name="fox_sample")(
            page_table, q8, k_new, v_new, lfn_t, strict_lower,
            *([cache_k] * npg), *([cache_v] * npg), *([pf_t] * npg))[:, :nq].astype(BF16)


def _cmul(ar, ai, br, bi):
    return ar * br - ai * bi, ar * bi + ai * br


def _cpow_by_bits(ar, ai, n, nbits):
    pr = jnp.ones(jnp.broadcast_shapes(ar.shape, n.shape), F32)
    pi = jnp.zeros_like(pr)
    for b in range(nbits):
        bit = ((n >> b) & 1) == 1
        fr = jnp.where(bit, ar, 1.0)
        fi = jnp.where(bit, ai, 0.0)
        pr, pi = _cmul(pr, pi, fr, fi)
        ar, ai = _cmul(ar, ai, ar, ai)
    return pr, pi


def _ssm_prep_kernel(ldt_ref, lrc_ref, lic_ref, lrr_ref, lir_ref, btr_ref, bti_ref, cre_ref, cim_ref,
                     ctr_ref, cti_ref, toep_ref, ctlr_ref, ctli_ref, obsr_ref, obsi_ref, alr_ref, ali_ref, *, chunk):
    lw = chunk * SSM_GROUP
    nbits = chunk.bit_length()
    dt = jnp.exp(ldt_ref[0])

    def discretise(lr, li):
        mag = jnp.exp(lr * dt)
        return mag * jnp.cos(li * dt), mag * jnp.sin(li * dt)

    lr, li = lrc_ref[0], lic_ref[0]
    ar, ai = discretise(lr, li)
    den = lr * lr + li * li
    nr, ni = ar - 1.0, ai
    cr, ci = (nr * lr + ni * li) / den, (ni * lr - nr * li) / den
    bbr, bbi = _cmul(cr, ci, btr_ref[0], bti_ref[0])
    sig = lax.broadcasted_iota(jnp.int32, (1, 2 * lw), 1) // SSM_GROUP
    er, ei = _cpow_by_bits(ar, ai, jnp.maximum(chunk - 1 - sig, 0), nbits)
    hr, hi = _cmul(er, ei, bbr, bbi)
    hr = jnp.where(sig < chunk, hr, 0.0)
    hi = jnp.where(sig < chunk, hi, 0.0)
    ctlr_ref[0] = hr[:, :lw]
    ctli_ref[0] = hi[:, :lw]
    c_re, c_im = cre_ref[0], cim_ref[0]
    hp = lax.Precision.HIGHEST
    for t in range(chunk):
        off = (chunk - 1 - t) * SSM_GROUP
        toep_ref[0, t * SSM_GROUP:(t + 1) * SSM_GROUP, :] = (
            jnp.dot(c_re, hr[:, off:off + lw], precision=hp, preferred_element_type=F32)
            - jnp.dot(c_im, hi[:, off:off + lw], precision=hp, preferred_element_type=F32))

    arr, air = discretise(lrr_ref[0], lir_ref[0])
    tp1 = lax.broadcasted_iota(jnp.int32, (lw, 1), 0) // SSM_GROUP + 1
    pr, pi = _cpow_by_bits(arr, air, tp1, nbits)
    c_r, c_i = ctr_ref[0], cti_ref[0]
    obsr_ref[0] = c_r * pr - c_i * pi
    obsi_ref[0] = -(c_r * pi + c_i * pr)
    lr_, li_ = arr, air
    for _ in range(chunk.bit_length() - 1):
        lr_, li_ = _cmul(lr_, li_, lr_, li_)
    alr_ref[0] = lr_
    ali_ref[0] = li_


def _ssm_prep(lam_re, lam_im, log_dt, b_re, b_im, c_re, c_im, *, chunk):
    assert chunk & (chunk - 1) == 0
    g, p, j = N_GROUPS, STATE_DIM, SSM_GROUP
    lw = chunk * j
    tile_b = lambda b: jnp.tile(b, (1, 1, 2 * chunk))
    tile_c = lambda c: jnp.tile(c, (1, chunk, 1))
    ins = [log_dt.reshape(g, 1, 1), lam_re.reshape(g, p, 1), lam_im.reshape(g, p, 1),
           lam_re.reshape(g, 1, p), lam_im.reshape(g, 1, p), tile_b(b_re), tile_b(b_im), c_re, c_im,
           tile_c(c_re), tile_c(c_im)]
    grp = lambda a, b: pl.BlockSpec((1, a, b), lambda i: (i, 0, 0))
    in_specs = [grp(1, 1), grp(p, 1), grp(p, 1), grp(1, p), grp(1, p), grp(p, 2 * lw), grp(p, 2 * lw),
                grp(j, p), grp(j, p), grp(lw, p), grp(lw, p)]
    shapes = [(lw, lw), (p, lw), (p, lw), (lw, p), (lw, p), (1, p), (1, p)]
    return pl.pallas_call(
        functools.partial(_ssm_prep_kernel, chunk=chunk), grid=(g,), in_specs=in_specs,
        out_specs=[grp(*s) for s in shapes],
        out_shape=[jax.ShapeDtypeStruct((g,) + s, F32) for s in shapes],
        compiler_params=_params("arbitrary"), name=f"ssm_prep_{chunk}")(*ins)


def _ssm_chunk_kernel(*refs, n_chunks, n_seq, has_h0, precise):
    if has_h0:
        (u_ref, toep_ref, ctlr_ref, ctli_ref, obsr_ref, obsi_ref, alr_ref, ali_ref, h0r_ref, h0i_ref,
         y_ref, hr_ref, hi_ref) = refs
    else:
        (u_ref, toep_ref, ctlr_ref, ctli_ref, obsr_ref, obsi_ref, alr_ref, ali_ref,
         y_ref, hr_ref, hi_ref) = refs
    if precise:
        cast = lambda x: x
        mm = lambda a, b: lax.dot_general(a, b, _NT, precision=lax.Precision.HIGHEST, preferred_element_type=F32)
    else:
        cast = lambda x: x.astype(BF16)
        mm = _dot_nt
    u = cast(u_ref[0])
    y = mm(u, cast(toep_ref[0]))
    hr = mm(u, cast(ctlr_ref[0]))
    hi = mm(u, cast(ctli_ref[0]))
    ar, ai = alr_ref[0], ali_ref[0]
    if n_chunks == 1:
        pr, pi = h0r_ref[0], h0i_ref[0]
        er, ei = _cmul(ar, ai, pr, pi)
        hr, hi = hr + er, hi + ei
        hr_ref[0] = hr
        hi_ref[0] = hi
    else:
        assert not has_h0
        rows = hr.shape[0]
        cidx = lax.broadcasted_iota(jnp.int32, (rows, 1), 0) % n_chunks

        def shifted(x, d):
            return jnp.where(cidx >= d, pltpu.roll(x, d, axis=0), 0.0)

        d = 1
        while d < n_chunks:
            sr, si = shifted(hr, d), shifted(hi, d)
            er, ei = _cmul(ar, ai, sr, si)
            hr, hi = hr + er, hi + ei
            ar, ai = _cmul(ar, ai, ar, ai)
            d *= 2
        for b in range(n_seq):
            last = (b + 1) * n_chunks - 1
            hr_ref[0, b:b + 1, :] = hr[last:last + 1, :]
            hi_ref[0, b:b + 1, :] = hi[last:last + 1, :]
        pr, pi = shifted(hr, 1), shifted(hi, 1)
    y = y + mm(cast(pr), cast(obsr_ref[0])) + mm(cast(pi), cast(obsi_ref[0]))
    y_ref[0] = y


def _ssm_chunked(u, ops, h0, *, chunk, precise):
    n_seq, t, _ = u.shape
    g, p, j = N_GROUPS, STATE_DIM, SSM_GROUP
    n_chunks = t // chunk
    rows, lw = n_seq * n_chunks, chunk * j
    ug = u.reshape(rows, chunk, g, j).transpose(2, 0, 1, 3).reshape(g, rows, lw)
    if not precise:
        ug = ug.astype(BF16)
    grp = lambda a, b: pl.BlockSpec((1, a, b), lambda i: (i, 0, 0))
    ins = [ug, *ops]
    in_specs = [grp(rows, lw), grp(lw, lw), grp(p, lw), grp(p, lw), grp(lw, p), grp(lw, p), grp(1, p), grp(1, p)]
    if h0 is not None:
        ins += [jnp.swapaxes(h0[0], 0, 1), jnp.swapaxes(h0[1], 0, 1)]
        in_specs += [grp(n_seq, p), grp(n_seq, p)]
    y, hr, hi = pl.pallas_call(
        functools.partial(_ssm_chunk_kernel, n_chunks=n_chunks, n_seq=n_seq, has_h0=h0 is not None, precise=precise),
        grid=(g,), in_specs=in_specs,
        out_specs=[grp(rows, lw), grp(n_seq, p), grp(n_seq, p)],
        out_shape=[jax.ShapeDtypeStruct((g, rows, lw), F32), jax.ShapeDtypeStruct((g, n_seq, p), F32),
                   jax.ShapeDtypeStruct((g, n_seq, p), F32)],
        compiler_params=_params("arbitrary"), name=f"ssm_chunk_{chunk}")(*ins)
    y = y.reshape(g, rows, chunk, j).transpose(1, 2, 0, 3).reshape(n_seq, t, SSM_WIDTH)
    return y, jnp.swapaxes(hr, 0, 1), jnp.swapaxes(hi, 0, 1)


FF_CHUNK = 1024


def _post_kernel(x_ref, att_ref, ys_ref, u_ref, p_ref, lng_ref, lnb_ref, dsk_ref, wglu_ref, bglu_ref,
                 woa_ref, wos_ref, l1g_ref, l1b_ref, wup_ref, wdn_ref, wpe_ref, wpg_ref, bpg_ref,
                 l2g_ref, l2b_ref, o_ref):
    h = _layer_norm(x_ref[...], lng_ref[...], lnb_ref[...])
    y = _gelu_tanh(ys_ref[...] + dsk_ref[...] * u_ref[...])
    y = y * _sigmoid(_dot(y.astype(BF16), wglu_ref[...]) + bglu_ref[...])
    mix = _dot(att_ref[...], woa_ref[...]) + _dot(y.astype(BF16), wos_ref[...])
    h1 = _layer_norm(ALPHA * h + mix, l1g_ref[...], l1b_ref[...])
    h1b = h1.astype(BF16)
    e = _sigmoid(_dot(h1b, wpg_ref[...]) + bpg_ref[...]) * _dot(p_ref[...].astype(BF16), wpe_ref[...])
    acc = ALPHA * h1 + e
    for c in range(D_FF // FF_CHUNK):
        cs = slice(c * FF_CHUNK, (c + 1) * FF_CHUNK)
        a = jnp.maximum(_dot(h1b, wup_ref[:, cs]), 0.0)
        acc = acc + _dot((a * a).astype(BF16), wdn_ref[cs, :])
    o_ref[...] = _layer_norm(acc, l2g_ref[...], l2b_ref[...])


def _post_mixer(x, att, ys, u, p, weights, *, tm):
    n = x.shape[0]
    tok = lambda w: pl.BlockSpec((tm, w), lambda i: (i, 0))
    const = lambda a: pl.BlockSpec(a.shape, lambda i: (0, 0), pipeline_mode=pl.Buffered(1))
    return pl.pallas_call(
        _post_kernel, grid=(n // tm,),
        in_specs=[tok(D_MODEL), tok(ATTN_WIDTH), tok(SSM_WIDTH), tok(SSM_WIDTH), tok(PLE_DIM)]
        + [const(w) for w in weights],
        out_specs=tok(D_MODEL), out_shape=jax.ShapeDtypeStruct((n, D_MODEL), F32),
        compiler_params=_params("arbitrary"), name="post_mixer")(x, att, ys, u, p, *weights)


def kernel(x_prompt, x_sample, cache_k, cache_v, cache_logf, state_re, state_im, page_table, p_prompt, p_sample,
           ln_in_g, ln_in_b, w_in, b_f, lam_re, lam_im, log_dt, b_re, b_im, c_re, c_im, d_skip, w_glu, b_glu,
           w_out, ln1_g, ln1_b, w_up, w_down, w_pe, w_pg, b_pg, ln2_g, ln2_b):
    assert w_in.shape[0] == 1, "one trunk layer"
    nb, t, _ = x_prompt.shape
    db, nq, _ = x_sample.shape
    row = lambda a: a.reshape(1, -1)
    post_w = [row(ln_in_g), row(ln_in_b), row(d_skip[0]), w_glu[0].astype(BF16), row(b_glu[0]),
              w_out[0, :ATTN_WIDTH].astype(BF16), w_out[0, ATTN_WIDTH:].astype(BF16), row(ln1_g[0]), row(ln1_b[0]),
              w_up[0].astype(BF16), w_down[0].astype(BF16), w_pe[0].astype(BF16), w_pg[0].astype(BF16),
              row(b_pg[0]), row(ln2_g[0]), row(ln2_b[0])]
    ssm_par = (lam_re[0], lam_im[0], log_dt[0], b_re[0], b_im[0], c_re[0], c_im[0])

    q, k, v, logf, u, vb, kT, cT = _in_proj(x_prompt, ln_in_g, ln_in_b, w_in[0], b_f[0],
                                             tm=512, attn_layouts=True, q_dtype=BF16)
    att = _fox_prompt(q, kT, vb, cT, tq=256)
    ys, sr, si = _ssm_chunked(u, _ssm_prep(*ssm_par, chunk=16), None, chunk=16, precise=False)
    y_prompt = _post_mixer(x_prompt.reshape(nb * t, D_MODEL), att.reshape(nb * t, ATTN_WIDTH),
                           ys.reshape(nb * t, SSM_WIDTH), u.reshape(nb * t, SSM_WIDTH),
                           p_prompt[0].reshape(nb * t, PLE_DIM), post_w, tm=512).reshape(nb, t, D_MODEL)
    prompt_out = (k.reshape(1, nb, t, N_HEADS, HEAD_DIM), v.reshape(1, nb, t, N_HEADS, HEAD_DIM),
                  logf[None], sr[None], si[None])

    qs, ks, vs, logfs, us = _in_proj(x_sample.reshape(1, db * nq, D_MODEL), ln_in_g, ln_in_b, w_in[0], b_f[0],
                                     tm=db * nq, attn_layouts=False, q_dtype=F32)
    seq = lambda a: a.reshape(db, nq, a.shape[-1])
    att_s = _fox_sample(seq(qs), seq(ks), seq(vs), seq(logfs), cache_k, cache_v, cache_logf, page_table, npg=4)
    ys_s, sr_s, si_s = _ssm_chunked(seq(us), _ssm_prep(*ssm_par, chunk=nq), (state_re[0], state_im[0]),
                                    chunk=nq, precise=True)
    y_sample = _post_mixer(x_sample.reshape(db * nq, D_MODEL), att_s.reshape(db * nq, ATTN_WIDTH),
                           ys_s.reshape(db * nq, SSM_WIDTH), us.reshape(db * nq, SSM_WIDTH),
                           p_sample[0].reshape(db * nq, PLE_DIM), post_w, tm=db * nq).reshape(db, nq, D_MODEL)
    sample_out = (ks.reshape(1, db, nq, N_HEADS, HEAD_DIM), vs.reshape(1, db, nq, N_HEADS, HEAD_DIM),
                  logfs.reshape(1, db, nq, N_HEADS), sr_s[None], si_s[None])
    return (y_prompt, y_sample) + prompt_out + sample_out
```

```python
import functools
import math

import jax
import jax.numpy as jnp
from jax import lax
from jax.experimental import pallas as pl
from jax.experimental.pallas import tpu as pltpu

F32 = jnp.float32
BF16 = jnp.bfloat16

D_MODEL = 1024
ATTN_WIDTH = 512
SSM_WIDTH = 512
HEAD_DIM = 64
N_HEADS = 8
SSM_GROUP = 16
N_GROUPS = 32
STATE_DIM = 64
D_FF = 4096
PLE_DIM = 256
PAGE_SIZE = 128
ALPHA = 2.0 ** 0.25
LN_EPS = 1e-5
NEG_INF = -1e30
QK_SCALE = HEAD_DIM ** -0.5
LOG2E = math.log2(math.e)

VMEM_LIMIT_BYTES = 56 * 1024 * 1024

_NT = (((1,), (1,)), ((), ()))


def _params(*sem):
    return pltpu.CompilerParams(dimension_semantics=sem, vmem_limit_bytes=VMEM_LIMIT_BYTES)


def _const_spec(shape):
    return pl.BlockSpec(shape, lambda *_: (0,) * len(shape))


def _layer_norm(x, g, b):
    mu = jnp.mean(x, axis=-1, keepdims=True)
    xc = x - mu
    var = jnp.mean(xc * xc, axis=-1, keepdims=True)
    return xc * lax.rsqrt(var + LN_EPS) * g + b


def _log_sigmoid(x):
    return jnp.minimum(x, 0.0) - jnp.log1p(jnp.exp(-jnp.abs(x)))


def _sigmoid(x):
    return 1.0 / (1.0 + jnp.exp(-x))


def _gelu_tanh(x):
    return 0.5 * x * (1.0 + jnp.tanh(math.sqrt(2.0 / math.pi) * (x + 0.044715 * (x * x * x))))


def _dot(a, b):
    return jnp.dot(a, b, preferred_element_type=F32)


def _dot_nt(a, b):
    return lax.dot_general(a, b, _NT, preferred_element_type=F32)


def _split3(x):
    hi = x.astype(BF16)
    r1 = x - hi.astype(F32)
    mid = r1.astype(BF16)
    lo = (r1 - mid.astype(F32)).astype(BF16)
    return hi, mid, lo


def _dot_exact01(x, m01):
    hi, mid, lo = _split3(x)
    return _dot(hi, m01) + _dot(mid, m01) + _dot(lo, m01)


N_BIAS_ROWS = 128
ONES_ROW = 3 * N_HEADS


def _in_proj_kernel(*refs, tm, q_scale, attn_layouts):
    x_ref, g_ref, b_ref, wu_ref = refs[:4]
    if attn_layouts:
        (wqT_ref, wkT_ref, wvT_ref, wfT_ref, bfc_ref, tri_ref,
         u_ref, kT_ref, vT_ref, lfT_ref, cT_ref, qTb_ref, kb_ref, vTb_ref, cp_ref, carry_ref) = refs[4:]
    else:
        wq_ref, wk_ref, wv_ref, wf_ref, bf_ref, u_ref, q_ref, k_ref, v_ref, lf_ref = refs[4:]
    hb = _layer_norm(x_ref[0], g_ref[...], b_ref[...]).astype(BF16)
    u_ref[0] = _dot(hb, wu_ref[...])
    if attn_layouts:
        qTb_ref[0] = (_dot_nt(wqT_ref[...], hb) * q_scale).astype(BF16)
        kT = _dot_nt(wkT_ref[...], hb)
        kT_ref[0] = kT
        kb_ref[0] = kT.T.astype(BF16)
        vT = _dot_nt(wvT_ref[...], hb)
        vT_ref[0] = vT
        vTb_ref[0] = vT.astype(BF16)
        lfT = _log_sigmoid(_dot_nt(wfT_ref[...], hb) + bfc_ref[...])
        lfT_ref[0] = lfT

        @pl.when(pl.program_id(1) == 0)
        def _():
            carry_ref[...] = jnp.zeros_like(carry_ref)

        c = carry_ref[...] + _dot_exact01(lfT, tri_ref[...])
        cT_ref[0] = c
        carry_ref[...] = c[:, tm - 1:tm]
        hi, mid, lo = _split3(c * LOG2E)
        row = lax.broadcasted_iota(jnp.int32, (N_BIAS_ROWS - ONES_ROW, tm), 0)
        ones = jnp.where(row < 3, 1.0, 0.0)
        pieces = jnp.concatenate([hi.astype(F32), mid.astype(F32), lo.astype(F32), ones], axis=0)
        cp_ref[0] = pieces.T.astype(BF16)
    else:
        q_ref[0] = _dot(hb, wq_ref[...]) * q_scale
        k_ref[0] = _dot(hb, wk_ref[...])
        v_ref[0] = _dot(hb, wv_ref[...])
        lf_ref[0] = _log_sigmoid(_dot(hb, wf_ref[...]) + bf_ref[...])


def _in_proj(x3, ln_g, ln_b, w_in, b_f, *, tm, q_scale, attn_layouts):
    nb, t, _ = x3.shape
    wq = w_in[:, :ATTN_WIDTH].astype(BF16)
    wk = w_in[:, ATTN_WIDTH:2 * ATTN_WIDTH].astype(BF16)
    wv = w_in[:, 2 * ATTN_WIDTH:3 * ATTN_WIDTH].astype(BF16)
    wf = w_in[:, 3 * ATTN_WIDTH:3 * ATTN_WIDTH + N_HEADS].astype(BF16)
    wu = w_in[:, 3 * ATTN_WIDTH + N_HEADS:].astype(BF16)
    tok = lambda w: pl.BlockSpec((1, tm, w), lambda b, i: (b, i, 0))
    feat = lambda w: pl.BlockSpec((1, w, tm), lambda b, i: (b, 0, i))
    sds = jax.ShapeDtypeStruct
    ins = [x3, ln_g.reshape(1, -1), ln_b.reshape(1, -1), wu]
    in_specs = [tok(D_MODEL), _const_spec((1, D_MODEL)), _const_spec((1, D_MODEL)), _const_spec((D_MODEL, SSM_WIDTH))]
    out_shape = [sds((nb, t, SSM_WIDTH), F32)]
    out_specs = [tok(SSM_WIDTH)]
    scratch = []
    if attn_layouts:
        tri = (jnp.arange(tm)[:, None] <= jnp.arange(tm)[None, :]).astype(BF16)
        ins += [wq.T, wk.T, wv.T, wf.T, b_f.reshape(N_HEADS, 1), tri]
        in_specs += [_const_spec((ATTN_WIDTH, D_MODEL))] * 3
        in_specs += [_const_spec((N_HEADS, D_MODEL)), _const_spec((N_HEADS, 1)), _const_spec((tm, tm))]
        out_shape += [sds((nb, ATTN_WIDTH, t), F32), sds((nb, ATTN_WIDTH, t), F32),
                      sds((nb, N_HEADS, t), F32), sds((nb, N_HEADS, t), F32),
                      sds((nb, ATTN_WIDTH, t), BF16), sds((nb, t, ATTN_WIDTH), BF16), sds((nb, ATTN_WIDTH, t), BF16),
                      sds((nb, t, N_BIAS_ROWS), BF16)]
        out_specs += [feat(ATTN_WIDTH), feat(ATTN_WIDTH), feat(N_HEADS), feat(N_HEADS),
                      feat(ATTN_WIDTH), tok(ATTN_WIDTH), feat(ATTN_WIDTH), tok(N_BIAS_ROWS)]
        scratch = [pltpu.VMEM((N_HEADS, 1), F32)]
    else:
        ins += [wq, wk, wv, wf, b_f.reshape(1, N_HEADS)]
        in_specs += [_const_spec((D_MODEL, ATTN_WIDTH))] * 3 + [_const_spec((D_MODEL, N_HEADS)), _const_spec((1, N_HEADS))]
        out_shape += [sds((nb, t, ATTN_WIDTH), F32)] * 3 + [sds((nb, t, N_HEADS), F32)]
        out_specs += [tok(ATTN_WIDTH)] * 3 + [tok(N_HEADS)]
    return pl.pallas_call(
        functools.partial(_in_proj_kernel, tm=tm, q_scale=q_scale, attn_layouts=attn_layouts),
        grid=(nb, t // tm), in_specs=in_specs, out_specs=out_specs, out_shape=out_shape,
        scratch_shapes=scratch, compiler_params=_params("arbitrary", "arbitrary"),
        name="in_proj_prompt" if attn_layouts else "in_proj_sample")(*ins)


def _fox_prompt_kernel(qT_ref, k_ref, cp_ref, vT_ref, cT_ref, o_ref, bq_ref, m_ref, l_ref, acc_ref, *, tq):
    qi = pl.program_id(1)
    q0 = pl.multiple_of(qi * tq, tq)
    heads = [slice(h * HEAD_DIM, (h + 1) * HEAD_DIM) for h in range(N_HEADS)]
    pairs = [slice(2 * HEAD_DIM * (h // 2), 2 * HEAD_DIM * (h // 2 + 1)) for h in range(N_HEADS)]

    c0 = cT_ref[0, :, pl.ds(q0, tq)][:, 0:1] * LOG2E
    c0_pieces = [p.astype(F32) for p in _split3(c0)]
    r = lax.broadcasted_iota(jnp.int32, (2 * HEAD_DIM, tq), 0)
    for h in range(N_HEADS):
        mine = (r < HEAD_DIM) if h % 2 == 0 else (r >= HEAD_DIM)
        bq_ref[h, 0:2 * HEAD_DIM, :] = jnp.where(mine, qT_ref[0, pairs[h], :], jnp.zeros((), BF16))
        sel = jnp.where((r == h) | (r == N_HEADS + h) | (r == 2 * N_HEADS + h), -1.0, 0.0)
        for j, piece in enumerate(c0_pieces):
            sel = jnp.where(r == ONES_ROW + j, piece[h:h + 1, :], sel)
        bq_ref[h, 2 * HEAD_DIM:, :] = sel.astype(BF16)

    key_row = lax.broadcasted_iota(jnp.int32, (tq, tq), 0)
    qry_col = lax.broadcasted_iota(jnp.int32, (tq, tq), 1)

    m_ref[...] = jnp.full_like(m_ref, NEG_INF)
    l_ref[...] = jnp.zeros_like(l_ref)
    acc_ref[...] = jnp.zeros_like(acc_ref)

    def tile(kt, diagonal):
        ks = pl.multiple_of(kt * tq, tq)
        cp = cp_ref[0, pl.ds(ks, tq), :]

        def scores(h):
            a = jnp.concatenate([k_ref[0, pl.ds(ks, tq), pairs[h]], cp], axis=1)
            s = _dot(a, bq_ref[h])
            return jnp.where(key_row <= qry_col, s, NEG_INF) if diagonal else s

        def softmax(h, s):
            m_new = jnp.maximum(m_ref[h], jnp.max(s, axis=0, keepdims=True))
            alpha = jnp.exp2(m_ref[h] - m_new)
            p = jnp.exp2(s - m_new)
            l_ref[h] = alpha * l_ref[h] + jnp.sum(p, axis=0, keepdims=True)
            m_ref[h] = m_new
            return alpha, p.astype(BF16)

        def values(h, alpha, p):
            acc_ref[h] = alpha * acc_ref[h] + _dot(vT_ref[0, heads[h], pl.ds(ks, tq)], p)

        s, ap = {}, {}
        for step in range(N_HEADS + 2):
            if step < N_HEADS:
                s[step] = scores(step)
            if 0 <= step - 1 < N_HEADS:
                ap[step - 1] = softmax(step - 1, s.pop(step - 1))
            if 0 <= step - 2 < N_HEADS:
                values(step - 2, *ap.pop(step - 2))

    def body(kt, carry):
        tile(kt, False)
        return carry

    lax.fori_loop(0, qi, body, 0)
    tile(qi, True)
    o_ref[0] = jnp.concatenate([acc_ref[h] / l_ref[h] for h in range(N_HEADS)], axis=0).T.astype(o_ref.dtype)


def _fox_prompt(qT, k, cp, vT, cT, *, tq):
    nb, t, _ = k.shape
    assert N_BIAS_ROWS == 2 * HEAD_DIM
    whole_f = lambda w: pl.BlockSpec((1, w, t), lambda b, i: (b, 0, 0))
    whole_t = lambda w: pl.BlockSpec((1, t, w), lambda b, i: (b, 0, 0))
    return pl.pallas_call(
        functools.partial(_fox_prompt_kernel, tq=tq),
        grid=(nb, t // tq),
        in_specs=[pl.BlockSpec((1, ATTN_WIDTH, tq), lambda b, i: (b, 0, i)),
                  whole_t(ATTN_WIDTH), whole_t(N_BIAS_ROWS), whole_f(ATTN_WIDTH), whole_f(N_HEADS)],
        out_specs=pl.BlockSpec((1, tq, ATTN_WIDTH), lambda b, i: (b, i, 0)),
        out_shape=jax.ShapeDtypeStruct((nb, t, ATTN_WIDTH), BF16),
        scratch_shapes=[pltpu.VMEM((N_HEADS, 2 * HEAD_DIM + N_BIAS_ROWS, tq), BF16),
                        pltpu.VMEM((N_HEADS, 1, tq), F32), pltpu.VMEM((N_HEADS, 1, tq), F32),
                        pltpu.VMEM((N_HEADS, HEAD_DIM, tq), F32)],
        compiler_params=_params("arbitrary", "arbitrary"), name="fox_prompt")(qT, k, cp, vT, cT)


def _fox_sample_kernel(pt_ref, q_ref, kn_ref, vn_ref, lfn_ref, sl_ref, *refs, n_pages, nq):
    del pt_ref
    k_refs, v_refs, pf_refs = refs[:n_pages], refs[n_pages:2 * n_pages], refs[2 * n_pages:3 * n_pages]
    o_ref = refs[3 * n_pages]
    nr = nq * N_HEADS
    head_of_lane = lax.broadcasted_iota(jnp.int32, (N_HEADS, ATTN_WIDTH), 1) // HEAD_DIM
    own = head_of_lane == lax.broadcasted_iota(jnp.int32, (N_HEADS, ATTN_WIDTH), 0)
    q = q_ref[0]
    qbd32 = jnp.concatenate([jnp.where(own, q[i:i + 1, :], 0.0) for i in range(nq)], axis=0)
    qbd = qbd32.astype(BF16)
    rep = lambda e: jnp.concatenate([e] * nq, axis=0)
    rows = lambda j: slice(j * N_HEADS, (j + 1) * N_HEADS)

    lfn = lfn_ref[0]
    e_new = rep(_dot_exact01(lfn, sl_ref[...]))
    kn, vn = kn_ref[0], vn_ref[0]
    qidx = lax.broadcasted_iota(jnp.int32, (nr, 1), 0) // N_HEADS
    s_new = [jnp.where(qidx >= j, jnp.sum(qbd32 * kn[j:j + 1, :], axis=-1, keepdims=True) + e_new[:, j:j + 1], NEG_INF)
             for j in range(nq)]

    pf_all = jnp.concatenate([r[0, 0] for r in pf_refs], axis=0)
    sfx_all = _dot_exact01(pf_all, sl_ref[...])
    tot_all = sfx_all[:, 0:1] + pf_all[:, 0:1]
    tail = jnp.sum(lfn, axis=-1, keepdims=True)
    scores = [None] * n_pages
    for j in reversed(range(n_pages)):
        kt = k_refs[j][0, 0].reshape(ATTN_WIDTH, PAGE_SIZE).astype(BF16)
        scores[j] = _dot(qbd, kt) + rep(sfx_all[rows(j)] + tail)
        tail = tail + tot_all[rows(j)]
    s_all = jnp.concatenate(scores, axis=1)
    m = jnp.max(s_all, axis=-1, keepdims=True)
    for s in s_new:
        m = jnp.maximum(m, s)
    p_all = jnp.exp(s_all - m)
    l = jnp.sum(p_all, axis=-1, keepdims=True)
    acc = jnp.zeros((nr, ATTN_WIDTH), F32)
    for j, s in enumerate(s_new):
        p = jnp.exp(s - m)
        l = l + p
        acc = acc + p * vn[j:j + 1, :]
    p_all = p_all.astype(BF16)
    for j in range(n_pages):
        vt = v_refs[j][0, 0].reshape(ATTN_WIDTH, PAGE_SIZE).astype(BF16)
        acc = acc + _dot_nt(p_all[:, j * PAGE_SIZE:(j + 1) * PAGE_SIZE], vt)
    acc = acc / l
    for i in range(nq):
        o_ref[0, i:i + 1, :] = jnp.sum(jnp.where(own, acc[i * N_HEADS:(i + 1) * N_HEADS], 0.0), axis=0, keepdims=True)


def _fox_sample(q, k_new, v_new, logf_new, cache_k, cache_v, cache_logf, page_table):
    db, nq, _ = q.shape
    n_pages = page_table.shape[1]
    kc = jnp.transpose(cache_k, (0, 2, 3, 1))
    vc = jnp.transpose(cache_v, (0, 2, 3, 1))
    pf = jnp.transpose(cache_logf, (0, 2, 1))
    lfn_t = jnp.swapaxes(jnp.pad(logf_new, ((0, 0), (0, PAGE_SIZE - nq), (0, 0))), 1, 2)
    strict_lower = (jnp.arange(PAGE_SIZE)[:, None] > jnp.arange(PAGE_SIZE)[None, :]).astype(BF16)
    seq = lambda r, w: pl.BlockSpec((1, r, w), lambda b, pt: (b, 0, 0))
    page = lambda j: pl.BlockSpec((1, 1, N_HEADS, HEAD_DIM, PAGE_SIZE), lambda b, pt: (0, pt[b, j], 0, 0, 0))
    page_f = lambda j: pl.BlockSpec((1, 1, N_HEADS, PAGE_SIZE), lambda b, pt: (0, pt[b, j], 0, 0))
    in_specs = [seq(nq, ATTN_WIDTH), seq(nq, ATTN_WIDTH), seq(nq, ATTN_WIDTH), seq(N_HEADS, PAGE_SIZE),
                pl.BlockSpec((PAGE_SIZE, PAGE_SIZE), lambda b, pt: (0, 0))]
    in_specs += [page(j) for j in range(n_pages)] * 2 + [page_f(j) for j in range(n_pages)]
    return pl.pallas_call(
        functools.partial(_fox_sample_kernel, n_pages=n_pages, nq=nq),
        grid_spec=pltpu.PrefetchScalarGridSpec(
            num_scalar_prefetch=1, grid=(db,), in_specs=in_specs, out_specs=seq(nq, ATTN_WIDTH)),
        out_shape=jax.ShapeDtypeStruct((db, nq, ATTN_WIDTH), F32),
        compiler_params=_params("arbitrary"), name="fox_sample")(
            page_table, q, k_new, v_new, lfn_t, strict_lower,
            *([kc[None]] * n_pages), *([vc[None]] * n_pages), *([pf[None]] * n_pages))


def _cmul(ar, ai, br, bi):
    return ar * br - ai * bi, ar * bi + ai * br


def _cpow_by_bits(ar, ai, n, nbits):
    pr = jnp.ones(jnp.broadcast_shapes(ar.shape, n.shape), F32)
    pi = jnp.zeros_like(pr)
    for b in range(nbits):
        bit = ((n >> b) & 1) == 1
        fr = jnp.where(bit, ar, 1.0)
        fi = jnp.where(bit, ai, 0.0)
        pr, pi = _cmul(pr, pi, fr, fi)
        ar, ai = _cmul(ar, ai, ar, ai)
    return pr, pi


def _ssm_prep_kernel(ldt_ref, lrc_ref, lic_ref, lrr_ref, lir_ref, btr_ref, bti_ref, cre_ref, cim_ref,
                     ctr_ref, cti_ref, toep_ref, ctlr_ref, ctli_ref, obsr_ref, obsi_ref, alr_ref, ali_ref, *, chunk):
    lw = chunk * SSM_GROUP
    nbits = chunk.bit_length()
    dt = jnp.exp(ldt_ref[0])

    def discretise(lr, li):
        mag = jnp.exp(lr * dt)
        return mag * jnp.cos(li * dt), mag * jnp.sin(li * dt)

    lr, li = lrc_ref[0], lic_ref[0]
    ar, ai = discretise(lr, li)
    den = lr * lr + li * li
    nr, ni = ar - 1.0, ai
    cr, ci = (nr * lr + ni * li) / den, (ni * lr - nr * li) / den
    bbr, bbi = _cmul(cr, ci, btr_ref[0], bti_ref[0])
    sig = lax.broadcasted_iota(jnp.int32, (1, 2 * lw), 1) // SSM_GROUP
    er, ei = _cpow_by_bits(ar, ai, jnp.maximum(chunk - 1 - sig, 0), nbits)
    hr, hi = _cmul(er, ei, bbr, bbi)
    hr = jnp.where(sig < chunk, hr, 0.0)
    hi = jnp.where(sig < chunk, hi, 0.0)
    ctlr_ref[0] = hr[:, :lw]
    ctli_ref[0] = hi[:, :lw]
    c_re, c_im = cre_ref[0], cim_ref[0]
    hp = lax.Precision.HIGHEST
    for t in range(chunk):
        off = (chunk - 1 - t) * SSM_GROUP
        toep_ref[0, t * SSM_GROUP:(t + 1) * SSM_GROUP, :] = (
            jnp.dot(c_re, hr[:, off:off + lw], precision=hp, preferred_element_type=F32)
            - jnp.dot(c_im, hi[:, off:off + lw], precision=hp, preferred_element_type=F32))

    arr, air = discretise(lrr_ref[0], lir_ref[0])
    tp1 = lax.broadcasted_iota(jnp.int32, (lw, 1), 0) // SSM_GROUP + 1
    pr, pi = _cpow_by_bits(arr, air, tp1, nbits)
    c_r, c_i = ctr_ref[0], cti_ref[0]
    obsr_ref[0] = c_r * pr - c_i * pi
    obsi_ref[0] = -(c_r * pi + c_i * pr)
    lr_, li_ = arr, air
    for _ in range(chunk.bit_length() - 1):
        lr_, li_ = _cmul(lr_, li_, lr_, li_)
    alr_ref[0] = lr_
    ali_ref[0] = li_


def _ssm_prep(lam_re, lam_im, log_dt, b_re, b_im, c_re, c_im, *, chunk):
    assert chunk & (chunk - 1) == 0
    g, p, j = N_GROUPS, STATE_DIM, SSM_GROUP
    lw = chunk * j
    tile_b = lambda b: jnp.tile(b, (1, 1, 2 * chunk))
    tile_c = lambda c: jnp.tile(c, (1, chunk, 1))
    ins = [log_dt.reshape(g, 1, 1), lam_re.reshape(g, p, 1), lam_im.reshape(g, p, 1),
           lam_re.reshape(g, 1, p), lam_im.reshape(g, 1, p), tile_b(b_re), tile_b(b_im), c_re, c_im,
           tile_c(c_re), tile_c(c_im)]
    grp = lambda a, b: pl.BlockSpec((1, a, b), lambda i: (i, 0, 0))
    in_specs = [grp(1, 1), grp(p, 1), grp(p, 1), grp(1, p), grp(1, p), grp(p, 2 * lw), grp(p, 2 * lw),
                grp(j, p), grp(j, p), grp(lw, p), grp(lw, p)]
    shapes = [(lw, lw), (p, lw), (p, lw), (lw, p), (lw, p), (1, p), (1, p)]
    return pl.pallas_call(
        functools.partial(_ssm_prep_kernel, chunk=chunk), grid=(g,), in_specs=in_specs,
        out_specs=[grp(*s) for s in shapes],
        out_shape=[jax.ShapeDtypeStruct((g,) + s, F32) for s in shapes],
        compiler_params=_params("arbitrary"), name=f"ssm_prep_{chunk}")(*ins)


def _ssm_chunk_kernel(*refs, n_chunks, n_seq, has_h0, precise):
    if has_h0:
        (u_ref, toep_ref, ctlr_ref, ctli_ref, obsr_ref, obsi_ref, alr_ref, ali_ref, h0r_ref, h0i_ref,
         y_ref, hr_ref, hi_ref) = refs
    else:
        (u_ref, toep_ref, ctlr_ref, ctli_ref, obsr_ref, obsi_ref, alr_ref, ali_ref,
         y_ref, hr_ref, hi_ref) = refs
    if precise:
        cast = lambda x: x
        mm = lambda a, b: lax.dot_general(a, b, _NT, precision=lax.Precision.HIGHEST, preferred_element_type=F32)
    else:
        cast = lambda x: x.astype(BF16)
        mm = _dot_nt
    u = cast(u_ref[0])
    y = mm(u, cast(toep_ref[0]))
    hr = mm(u, cast(ctlr_ref[0]))
    hi = mm(u, cast(ctli_ref[0]))
    ar, ai = alr_ref[0], ali_ref[0]
    if n_chunks == 1:
        pr, pi = h0r_ref[0], h0i_ref[0]
        er, ei = _cmul(ar, ai, pr, pi)
        hr, hi = hr + er, hi + ei
        hr_ref[0] = hr
        hi_ref[0] = hi
    else:
        assert not has_h0
        rows = hr.shape[0]
        cidx = lax.broadcasted_iota(jnp.int32, (rows, 1), 0) % n_chunks

        def shifted(x, d):
            return jnp.where(cidx >= d, pltpu.roll(x, d, axis=0), 0.0)

        d = 1
        while d < n_chunks:
            sr, si = shifted(hr, d), shifted(hi, d)
            er, ei = _cmul(ar, ai, sr, si)
            hr, hi = hr + er, hi + ei
            ar, ai = _cmul(ar, ai, ar, ai)
            d *= 2
        for b in range(n_seq):
            last = (b + 1) * n_chunks - 1
            hr_ref[0, b:b + 1, :] = hr[last:last + 1, :]
            hi_ref[0, b:b + 1, :] = hi[last:last + 1, :]
        pr, pi = shifted(hr, 1), shifted(hi, 1)
    y = y + mm(cast(pr), cast(obsr_ref[0])) + mm(cast(pi), cast(obsi_ref[0]))
    y_ref[0] = y


def _ssm_chunked(u, ops, h0, *, chunk, precise):
    n_seq, t, _ = u.shape
    g, p, j = N_GROUPS, STATE_DIM, SSM_GROUP
    n_chunks = t // chunk
    rows, lw = n_seq * n_chunks, chunk * j
    ug = u.reshape(rows, chunk, g, j).transpose(2, 0, 1, 3).reshape(g, rows, lw)
    if not precise:
        ug = ug.astype(BF16)
    grp = lambda a, b: pl.BlockSpec((1, a, b), lambda i: (i, 0, 0))
    ins = [ug, *ops]
    in_specs = [grp(rows, lw), grp(lw, lw), grp(p, lw), grp(p, lw), grp(lw, p), grp(lw, p), grp(1, p), grp(1, p)]
    if h0 is not None:
        ins += [jnp.swapaxes(h0[0], 0, 1), jnp.swapaxes(h0[1], 0, 1)]
        in_specs += [grp(n_seq, p), grp(n_seq, p)]
    y, hr, hi = pl.pallas_call(
        functools.partial(_ssm_chunk_kernel, n_chunks=n_chunks, n_seq=n_seq, has_h0=h0 is not None, precise=precise),
        grid=(g,), in_specs=in_specs,
        out_specs=[grp(rows, lw), grp(n_seq, p), grp(n_seq, p)],
        out_shape=[jax.ShapeDtypeStruct((g, rows, lw), F32), jax.ShapeDtypeStruct((g, n_seq, p), F32),
                   jax.ShapeDtypeStruct((g, n_seq, p), F32)],
        compiler_params=_params("arbitrary"), name=f"ssm_chunk_{chunk}")(*ins)
    y = y.reshape(g, rows, chunk, j).transpose(1, 2, 0, 3).reshape(n_seq, t, SSM_WIDTH)
    return y, jnp.swapaxes(hr, 0, 1), jnp.swapaxes(hi, 0, 1)


FF_CHUNK = 1024


def _post_kernel(x_ref, att_ref, ys_ref, u_ref, p_ref, lng_ref, lnb_ref, dsk_ref, wglu_ref, bglu_ref,
                 woa_ref, wos_ref, l1g_ref, l1b_ref, wup_ref, wdn_ref, wpe_ref, wpg_ref, bpg_ref,
                 l2g_ref, l2b_ref, o_ref):
    h = _layer_norm(x_ref[...], lng_ref[...], lnb_ref[...])
    y = _gelu_tanh(ys_ref[...] + dsk_ref[...] * u_ref[...])
    y = y * _sigmoid(_dot(y.astype(BF16), wglu_ref[...]) + bglu_ref[...])
    mix = _dot(att_ref[...], woa_ref[...]) + _dot(y.astype(BF16), wos_ref[...])
    h1 = _layer_norm(ALPHA * h + mix, l1g_ref[...], l1b_ref[...])
    h1b = h1.astype(BF16)
    e = _sigmoid(_dot(h1b, wpg_ref[...]) + bpg_ref[...]) * _dot(p_ref[...].astype(BF16), wpe_ref[...])
    acc = ALPHA * h1 + e
    for c in range(D_FF // FF_CHUNK):
        cs = slice(c * FF_CHUNK, (c + 1) * FF_CHUNK)
        a = jnp.maximum(_dot(h1b, wup_ref[:, cs]), 0.0)
        acc = acc + _dot((a * a).astype(BF16), wdn_ref[cs, :])
    o_ref[...] = _layer_norm(acc, l2g_ref[...], l2b_ref[...])


def _post_mixer(x, att, ys, u, p, weights, *, tm):
    n = x.shape[0]
    tok = lambda w: pl.BlockSpec((tm, w), lambda i: (i, 0))
    const = lambda a: pl.BlockSpec(a.shape, lambda i: (0, 0), pipeline_mode=pl.Buffered(1))
    return pl.pallas_call(
        _post_kernel, grid=(n // tm,),
        in_specs=[tok(D_MODEL), tok(ATTN_WIDTH), tok(SSM_WIDTH), tok(SSM_WIDTH), tok(PLE_DIM)]
        + [const(w) for w in weights],
        out_specs=tok(D_MODEL), out_shape=jax.ShapeDtypeStruct((n, D_MODEL), F32),
        compiler_params=_params("arbitrary"), name="post_mixer")(x, att, ys, u, p, *weights)


def kernel(x_prompt, x_sample, cache_k, cache_v, cache_logf, state_re, state_im, page_table, p_prompt, p_sample,
           ln_in_g, ln_in_b, w_in, b_f, lam_re, lam_im, log_dt, b_re, b_im, c_re, c_im, d_skip, w_glu, b_glu,
           w_out, ln1_g, ln1_b, w_up, w_down, w_pe, w_pg, b_pg, ln2_g, ln2_b):
    assert w_in.shape[0] == 1, "one trunk layer"
    nb, t, _ = x_prompt.shape
    db, nq, _ = x_sample.shape
    row = lambda a: a.reshape(1, -1)
    post_w = [row(ln_in_g), row(ln_in_b), row(d_skip[0]), w_glu[0].astype(BF16), row(b_glu[0]),
              w_out[0, :ATTN_WIDTH].astype(BF16), w_out[0, ATTN_WIDTH:].astype(BF16), row(ln1_g[0]), row(ln1_b[0]),
              w_up[0].astype(BF16), w_down[0].astype(BF16), w_pe[0].astype(BF16), w_pg[0].astype(BF16),
              row(b_pg[0]), row(ln2_g[0]), row(ln2_b[0])]
    ssm_par = (lam_re[0], lam_im[0], log_dt[0], b_re[0], b_im[0], c_re[0], c_im[0])

    u, kT, vT, lfT, cT, qTb, kb, vTb, cp = _in_proj(x_prompt, ln_in_g, ln_in_b, w_in[0], b_f[0],
                                                     tm=512, q_scale=QK_SCALE * LOG2E, attn_layouts=True)
    att = _fox_prompt(qTb, kb, cp, vTb, cT, tq=256)
    ys, sr, si = _ssm_chunked(u, _ssm_prep(*ssm_par, chunk=16), None, chunk=16, precise=False)
    y_prompt = _post_mixer(x_prompt.reshape(nb * t, D_MODEL), att.reshape(nb * t, ATTN_WIDTH),
                           ys.reshape(nb * t, SSM_WIDTH), u.reshape(nb * t, SSM_WIDTH),
                           p_prompt[0].reshape(nb * t, PLE_DIM), post_w, tm=512).reshape(nb, t, D_MODEL)
    heads_last = lambda a: jnp.transpose(a.reshape(nb, N_HEADS, HEAD_DIM, t), (0, 3, 1, 2))[None]
    prompt_out = (heads_last(kT), heads_last(vT), jnp.swapaxes(lfT, 1, 2)[None], sr[None], si[None])

    us, qs, ks, vs, logfs = _in_proj(x_sample.reshape(1, db * nq, D_MODEL), ln_in_g, ln_in_b, w_in[0], b_f[0],
                                     tm=db * nq, q_scale=QK_SCALE, attn_layouts=False)
    seq = lambda a: a.reshape(db, nq, a.shape[-1])
    att_s = _fox_sample(seq(qs), seq(ks), seq(vs), seq(logfs), cache_k[0], cache_v[0], cache_logf[0], page_table)
    ys_s, sr_s, si_s = _ssm_chunked(seq(us), _ssm_prep(*ssm_par, chunk=nq), (state_re[0], state_im[0]),
                                    chunk=nq, precise=True)
    y_sample = _post_mixer(x_sample.reshape(db * nq, D_MODEL), att_s.reshape(db * nq, ATTN_WIDTH).astype(BF16),
                           ys_s.reshape(db * nq, SSM_WIDTH), us.reshape(db * nq, SSM_WIDTH),
                           p_sample[0].reshape(db * nq, PLE_DIM), post_w, tm=db * nq).reshape(db, nq, D_MODEL)
    sample_out = (ks.reshape(1, db, nq, N_HEADS, HEAD_DIM), vs.reshape(1, db, nq, N_HEADS, HEAD_DIM),
                  logfs.reshape(1, db, nq, N_HEADS), sr_s[None], si_s[None])
    return (y_prompt, y_sample) + prompt_out + sample_out
```

```python
import functools
import math

import jax
import jax.numpy as jnp
from jax import lax
from jax.experimental import pallas as pl
from jax.experimental.pallas import tpu as pltpu

F32 = jnp.float32
BF16 = jnp.bfloat16

D_MODEL = 1024
ATTN_WIDTH = 512
SSM_WIDTH = 512
HEAD_DIM = 64
N_HEADS = 8
SSM_GROUP = 16
N_GROUPS = 32
STATE_DIM = 64
D_FF = 4096
PLE_DIM = 256
PAGE_SIZE = 128
ALPHA = 2.0 ** 0.25
LN_EPS = 1e-5
NEG_INF = -1e30
QK_SCALE = HEAD_DIM ** -0.5
LOG2E = math.log2(math.e)

VMEM_LIMIT_BYTES = 56 * 1024 * 1024

_NT = (((1,), (1,)), ((), ()))


def _params(*sem):
    return pltpu.CompilerParams(dimension_semantics=sem, vmem_limit_bytes=VMEM_LIMIT_BYTES)


def _const_spec(shape):
    return pl.BlockSpec(shape, lambda *_: (0,) * len(shape))


def _layer_norm(x, g, b):
    mu = jnp.mean(x, axis=-1, keepdims=True)
    xc = x - mu
    var = jnp.mean(xc * xc, axis=-1, keepdims=True)
    return xc * lax.rsqrt(var + LN_EPS) * g + b


def _log_sigmoid(x):
    return jnp.minimum(x, 0.0) - jnp.log1p(jnp.exp(-jnp.abs(x)))


def _sigmoid(x):
    return 1.0 / (1.0 + jnp.exp(-x))


def _gelu_tanh(x):
    return 0.5 * x * (1.0 + jnp.tanh(math.sqrt(2.0 / math.pi) * (x + 0.044715 * (x * x * x))))


def _dot(a, b):
    return jnp.dot(a, b, preferred_element_type=F32)


def _dot_nt(a, b):
    return lax.dot_general(a, b, _NT, preferred_element_type=F32)


def _split3(x):
    hi = x.astype(BF16)
    r1 = x - hi.astype(F32)
    mid = r1.astype(BF16)
    lo = (r1 - mid.astype(F32)).astype(BF16)
    return hi, mid, lo


LANES = 128


def _lane_tiles_spec(tm, width):
    return pl.BlockSpec((1, width // LANES, tm, LANES), lambda b, i: (b, 0, i, 0))


def _store_lane_tiles(ref, x):
    for v in range(x.shape[-1] // LANES):
        ref[0, v] = x[:, v * LANES:(v + 1) * LANES]


def _load_lane_tiles(ref):
    return jnp.concatenate([ref[0, v] for v in range(ref.shape[1])], axis=-1)


def _dot_exact01(x, m01):
    hi, mid, lo = _split3(x)
    return _dot(hi, m01) + _dot(mid, m01) + _dot(lo, m01)


N_BIAS_ROWS = 128
ONES_ROW = 3 * N_HEADS


def _in_proj_kernel(*refs, tm, q_scale, attn_layouts):
    x_ref, g_ref, b_ref, wu_ref = refs[:4]
    if attn_layouts:
        (wqT_ref, wkT_ref, wvT_ref, wfT_ref, bfc_ref, tri_ref,
         u_ref, kT_ref, vT_ref, lfT_ref, cT_ref, qTb_ref, kb_ref, vTb_ref, cp_ref, carry_ref) = refs[4:]
    else:
        wq_ref, wk_ref, wv_ref, wf_ref, bf_ref, u_ref, q_ref, k_ref, v_ref, lf_ref = refs[4:]
    hb = _layer_norm(x_ref[0], g_ref[...], b_ref[...]).astype(BF16)
    _store_lane_tiles(u_ref, _dot(hb, wu_ref[...]))
    if attn_layouts:
        qTb_ref[0] = (_dot_nt(wqT_ref[...], hb) * q_scale).astype(BF16)
        kT = _dot_nt(wkT_ref[...], hb)
        kT_ref[0] = kT
        kb_ref[0] = kT.T.astype(BF16)
        vT = _dot_nt(wvT_ref[...], hb)
        vT_ref[0] = vT
        vTb_ref[0] = vT.astype(BF16)
        lfT = _log_sigmoid(_dot_nt(wfT_ref[...], hb) + bfc_ref[...])
        lfT_ref[0] = lfT

        @pl.when(pl.program_id(1) == 0)
        def _():
            carry_ref[...] = jnp.zeros_like(carry_ref)

        c = carry_ref[...] + _dot_exact01(lfT, tri_ref[...])
        cT_ref[0] = c
        carry_ref[...] = c[:, tm - 1:tm]
        hi, mid, lo = _split3(c * LOG2E)
        row = lax.broadcasted_iota(jnp.int32, (N_BIAS_ROWS - ONES_ROW, tm), 0)
        ones = jnp.where(row < 3, 1.0, 0.0)
        pieces = jnp.concatenate([hi.astype(F32), mid.astype(F32), lo.astype(F32), ones], axis=0)
        cp_ref[0] = pieces.T.astype(BF16)
    else:
        q_ref[0] = _dot(hb, wq_ref[...]) * q_scale
        k_ref[0] = _dot(hb, wk_ref[...])
        v_ref[0] = _dot(hb, wv_ref[...])
        lf_ref[0] = _log_sigmoid(_dot(hb, wf_ref[...]) + bf_ref[...])


def _in_proj(x3, ln_g, ln_b, w_in, b_f, *, tm, q_scale, attn_layouts):
    nb, t, _ = x3.shape
    wq = w_in[:, :ATTN_WIDTH].astype(BF16)
    wk = w_in[:, ATTN_WIDTH:2 * ATTN_WIDTH].astype(BF16)
    wv = w_in[:, 2 * ATTN_WIDTH:3 * ATTN_WIDTH].astype(BF16)
    wf = w_in[:, 3 * ATTN_WIDTH:3 * ATTN_WIDTH + N_HEADS].astype(BF16)
    wu = w_in[:, 3 * ATTN_WIDTH + N_HEADS:].astype(BF16)
    tok = lambda w: pl.BlockSpec((1, tm, w), lambda b, i: (b, i, 0))
    feat = lambda w: pl.BlockSpec((1, w, tm), lambda b, i: (b, 0, i))
    sds = jax.ShapeDtypeStruct
    ins = [x3, ln_g.reshape(1, -1), ln_b.reshape(1, -1), wu]
    in_specs = [tok(D_MODEL), _const_spec((1, D_MODEL)), _const_spec((1, D_MODEL)), _const_spec((D_MODEL, SSM_WIDTH))]
    out_shape = [sds((nb, SSM_WIDTH // LANES, t, LANES), F32)]
    out_specs = [_lane_tiles_spec(tm, SSM_WIDTH)]
    scratch = []
    if attn_layouts:
        tri = (jnp.arange(tm)[:, None] <= jnp.arange(tm)[None, :]).astype(BF16)
        ins += [wq.T, wk.T, wv.T, wf.T, b_f.reshape(N_HEADS, 1), tri]
        in_specs += [_const_spec((ATTN_WIDTH, D_MODEL))] * 3
        in_specs += [_const_spec((N_HEADS, D_MODEL)), _const_spec((N_HEADS, 1)), _const_spec((tm, tm))]
        out_shape += [sds((nb, ATTN_WIDTH, t), F32), sds((nb, ATTN_WIDTH, t), F32),
                      sds((nb, N_HEADS, t), F32), sds((nb, N_HEADS, t), F32),
                      sds((nb, ATTN_WIDTH, t), BF16), sds((nb, t, ATTN_WIDTH), BF16), sds((nb, ATTN_WIDTH, t), BF16),
                      sds((nb, t, N_BIAS_ROWS), BF16)]
        out_specs += [feat(ATTN_WIDTH), feat(ATTN_WIDTH), feat(N_HEADS), feat(N_HEADS),
                      feat(ATTN_WIDTH), tok(ATTN_WIDTH), feat(ATTN_WIDTH), tok(N_BIAS_ROWS)]
        scratch = [pltpu.VMEM((N_HEADS, 1), F32)]
    else:
        ins += [wq, wk, wv, wf, b_f.reshape(1, N_HEADS)]
        in_specs += [_const_spec((D_MODEL, ATTN_WIDTH))] * 3 + [_const_spec((D_MODEL, N_HEADS)), _const_spec((1, N_HEADS))]
        out_shape += [sds((nb, t, ATTN_WIDTH), F32)] * 3 + [sds((nb, t, N_HEADS), F32)]
        out_specs += [tok(ATTN_WIDTH)] * 3 + [tok(N_HEADS)]
    return pl.pallas_call(
        functools.partial(_in_proj_kernel, tm=tm, q_scale=q_scale, attn_layouts=attn_layouts),
        grid=(nb, t // tm), in_specs=in_specs, out_specs=out_specs, out_shape=out_shape,
        scratch_shapes=scratch, compiler_params=_params("arbitrary", "arbitrary"),
        name="in_proj_prompt" if attn_layouts else "in_proj_sample")(*ins)


def _fox_prompt_kernel(qT_ref, k_ref, cp_ref, vT_ref, cT_ref, o_ref, bq_ref, m_ref, l_ref, acc_ref, *, tq):
    qi = pl.program_id(1)
    q0 = pl.multiple_of(qi * tq, tq)
    heads = [slice(h * HEAD_DIM, (h + 1) * HEAD_DIM) for h in range(N_HEADS)]
    pairs = [slice(2 * HEAD_DIM * (h // 2), 2 * HEAD_DIM * (h // 2 + 1)) for h in range(N_HEADS)]

    c0 = cT_ref[0, :, pl.ds(q0, tq)][:, 0:1] * LOG2E
    c0_pieces = [p.astype(F32) for p in _split3(c0)]
    r = lax.broadcasted_iota(jnp.int32, (2 * HEAD_DIM, tq), 0)
    for h in range(N_HEADS):
        mine = (r < HEAD_DIM) if h % 2 == 0 else (r >= HEAD_DIM)
        bq_ref[h, 0:2 * HEAD_DIM, :] = jnp.where(mine, qT_ref[0, pairs[h], :], jnp.zeros((), BF16))
        sel = jnp.where((r == h) | (r == N_HEADS + h) | (r == 2 * N_HEADS + h), -1.0, 0.0)
        for j, piece in enumerate(c0_pieces):
            sel = jnp.where(r == ONES_ROW + j, piece[h:h + 1, :], sel)
        bq_ref[h, 2 * HEAD_DIM:, :] = sel.astype(BF16)

    key_row = lax.broadcasted_iota(jnp.int32, (tq, tq), 0)
    qry_col = lax.broadcasted_iota(jnp.int32, (tq, tq), 1)

    m_ref[...] = jnp.full_like(m_ref, NEG_INF)
    l_ref[...] = jnp.zeros_like(l_ref)
    acc_ref[...] = jnp.zeros_like(acc_ref)

    def tile(kt, diagonal):
        ks = pl.multiple_of(kt * tq, tq)
        cp = cp_ref[0, pl.ds(ks, tq), :]

        def scores(h):
            a = jnp.concatenate([k_ref[0, pl.ds(ks, tq), pairs[h]], cp], axis=1)
            s = _dot(a, bq_ref[h])
            return jnp.where(key_row <= qry_col, s, NEG_INF) if diagonal else s

        def softmax(h, s):
            m_new = jnp.maximum(m_ref[h], jnp.max(s, axis=0, keepdims=True))
            alpha = jnp.exp2(m_ref[h] - m_new)
            p = jnp.exp2(s - m_new)
            l_ref[h] = alpha * l_ref[h] + jnp.sum(p, axis=0, keepdims=True)
            m_ref[h] = m_new
            return alpha, p.astype(BF16)

        def values(h, alpha, p):
            acc_ref[h] = alpha * acc_ref[h] + _dot(vT_ref[0, heads[h], pl.ds(ks, tq)], p)

        s, ap = {}, {}
        for step in range(N_HEADS + 2):
            if step < N_HEADS:
                s[step] = scores(step)
            if 0 <= step - 1 < N_HEADS:
                ap[step - 1] = softmax(step - 1, s.pop(step - 1))
            if 0 <= step - 2 < N_HEADS:
                values(step - 2, *ap.pop(step - 2))

    def body(kt, carry):
        tile(kt, False)
        return carry

    lax.fori_loop(0, qi, body, 0)
    tile(qi, True)
    o_ref[0] = jnp.concatenate([acc_ref[h] / l_ref[h] for h in range(N_HEADS)], axis=0).T.astype(o_ref.dtype)


def _fox_prompt(qT, k, cp, vT, cT, *, tq):
    nb, t, _ = k.shape
    assert N_BIAS_ROWS == 2 * HEAD_DIM
    whole_f = lambda w: pl.BlockSpec((1, w, t), lambda b, i: (b, 0, 0))
    whole_t = lambda w: pl.BlockSpec((1, t, w), lambda b, i: (b, 0, 0))
    return pl.pallas_call(
        functools.partial(_fox_prompt_kernel, tq=tq),
        grid=(nb, t // tq),
        in_specs=[pl.BlockSpec((1, ATTN_WIDTH, tq), lambda b, i: (b, 0, i)),
                  whole_t(ATTN_WIDTH), whole_t(N_BIAS_ROWS), whole_f(ATTN_WIDTH), whole_f(N_HEADS)],
        out_specs=pl.BlockSpec((1, tq, ATTN_WIDTH), lambda b, i: (b, i, 0)),
        out_shape=jax.ShapeDtypeStruct((nb, t, ATTN_WIDTH), BF16),
        scratch_shapes=[pltpu.VMEM((N_HEADS, 2 * HEAD_DIM + N_BIAS_ROWS, tq), BF16),
                        pltpu.VMEM((N_HEADS, 1, tq), F32), pltpu.VMEM((N_HEADS, 1, tq), F32),
                        pltpu.VMEM((N_HEADS, HEAD_DIM, tq), F32)],
        compiler_params=_params("arbitrary", "arbitrary"), name="fox_prompt")(qT, k, cp, vT, cT)


def _fox_sample_kernel(pt_ref, q_ref, kn_ref, vn_ref, lfn_ref, sl_ref, *refs, n_pages, nq):
    del pt_ref
    k_refs, v_refs, pf_refs = refs[:n_pages], refs[n_pages:2 * n_pages], refs[2 * n_pages:3 * n_pages]
    o_ref = refs[3 * n_pages]
    nr = nq * N_HEADS
    head_of_lane = lax.broadcasted_iota(jnp.int32, (N_HEADS, ATTN_WIDTH), 1) // HEAD_DIM
    own = head_of_lane == lax.broadcasted_iota(jnp.int32, (N_HEADS, ATTN_WIDTH), 0)
    q = q_ref[0]
    qbd32 = jnp.concatenate([jnp.where(own, q[i:i + 1, :], 0.0) for i in range(nq)], axis=0)
    qbd = qbd32.astype(BF16)
    rep = lambda e: jnp.concatenate([e] * nq, axis=0)
    rows = lambda j: slice(j * N_HEADS, (j + 1) * N_HEADS)

    lfn = lfn_ref[0]
    e_new = rep(_dot_exact01(lfn, sl_ref[...]))
    kn, vn = kn_ref[0], vn_ref[0]
    qidx = lax.broadcasted_iota(jnp.int32, (nr, 1), 0) // N_HEADS
    s_new = [jnp.where(qidx >= j, jnp.sum(qbd32 * kn[j:j + 1, :], axis=-1, keepdims=True) + e_new[:, j:j + 1], NEG_INF)
             for j in range(nq)]

    pf_all = jnp.concatenate([r[0, 0] for r in pf_refs], axis=0)
    sfx_all = _dot_exact01(pf_all, sl_ref[...])
    tot_all = sfx_all[:, 0:1] + pf_all[:, 0:1]
    tail = jnp.sum(lfn, axis=-1, keepdims=True)
    scores = [None] * n_pages
    for j in reversed(range(n_pages)):
        kt = k_refs[j][0, 0].reshape(ATTN_WIDTH, PAGE_SIZE).astype(BF16)
        scores[j] = _dot(qbd, kt) + rep(sfx_all[rows(j)] + tail)
        tail = tail + tot_all[rows(j)]
    s_all = jnp.concatenate(scores, axis=1)
    m = jnp.max(s_all, axis=-1, keepdims=True)
    for s in s_new:
        m = jnp.maximum(m, s)
    p_all = jnp.exp(s_all - m)
    l = jnp.sum(p_all, axis=-1, keepdims=True)
    acc = jnp.zeros((nr, ATTN_WIDTH), F32)
    for j, s in enumerate(s_new):
        p = jnp.exp(s - m)
        l = l + p
        acc = acc + p * vn[j:j + 1, :]
    p_all = p_all.astype(BF16)
    for j in range(n_pages):
        vt = v_refs[j][0, 0].reshape(ATTN_WIDTH, PAGE_SIZE).astype(BF16)
        acc = acc + _dot_nt(p_all[:, j * PAGE_SIZE:(j + 1) * PAGE_SIZE], vt)
    acc = acc / l
    for i in range(nq):
        o_ref[0, i:i + 1, :] = jnp.sum(jnp.where(own, acc[i * N_HEADS:(i + 1) * N_HEADS], 0.0), axis=0, keepdims=True)


def _fox_sample(q, k_new, v_new, logf_new, cache_k, cache_v, cache_logf, page_table):
    db, nq, _ = q.shape
    n_pages = page_table.shape[1]
    kc = jnp.transpose(cache_k, (0, 2, 3, 1))
    vc = jnp.transpose(cache_v, (0, 2, 3, 1))
    pf = jnp.transpose(cache_logf, (0, 2, 1))
    lfn_t = jnp.swapaxes(jnp.pad(logf_new, ((0, 0), (0, PAGE_SIZE - nq), (0, 0))), 1, 2)
    strict_lower = (jnp.arange(PAGE_SIZE)[:, None] > jnp.arange(PAGE_SIZE)[None, :]).astype(BF16)
    seq = lambda r, w: pl.BlockSpec((1, r, w), lambda b, pt: (b, 0, 0))
    page = lambda j: pl.BlockSpec((1, 1, N_HEADS, HEAD_DIM, PAGE_SIZE), lambda b, pt: (0, pt[b, j], 0, 0, 0))
    page_f = lambda j: pl.BlockSpec((1, 1, N_HEADS, PAGE_SIZE), lambda b, pt: (0, pt[b, j], 0, 0))
    in_specs = [seq(nq, ATTN_WIDTH), seq(nq, ATTN_WIDTH), seq(nq, ATTN_WIDTH), seq(N_HEADS, PAGE_SIZE),
                pl.BlockSpec((PAGE_SIZE, PAGE_SIZE), lambda b, pt: (0, 0))]
    in_specs += [page(j) for j in range(n_pages)] * 2 + [page_f(j) for j in range(n_pages)]
    return pl.pallas_call(
        functools.partial(_fox_sample_kernel, n_pages=n_pages, nq=nq),
        grid_spec=pltpu.PrefetchScalarGridSpec(
            num_scalar_prefetch=1, grid=(db,), in_specs=in_specs, out_specs=seq(nq, ATTN_WIDTH)),
        out_shape=jax.ShapeDtypeStruct((db, nq, ATTN_WIDTH), F32),
        compiler_params=_params("arbitrary"), name="fox_sample")(
            page_table, q, k_new, v_new, lfn_t, strict_lower,
            *([kc[None]] * n_pages), *([vc[None]] * n_pages), *([pf[None]] * n_pages))


def _cmul(ar, ai, br, bi):
    return ar * br - ai * bi, ar * bi + ai * br


def _cpow_by_bits(ar, ai, n, nbits):
    pr = jnp.ones(jnp.broadcast_shapes(ar.shape, n.shape), F32)
    pi = jnp.zeros_like(pr)
    for b in range(nbits):
        bit = ((n >> b) & 1) == 1
        fr = jnp.where(bit, ar, 1.0)
        fi = jnp.where(bit, ai, 0.0)
        pr, pi = _cmul(pr, pi, fr, fi)
        ar, ai = _cmul(ar, ai, ar, ai)
    return pr, pi


def _ssm_prep_kernel(ldt_ref, lrc_ref, lic_ref, lrr_ref, lir_ref, btr_ref, bti_ref, cre_ref, cim_ref,
                     ctr_ref, cti_ref, toep_ref, ctlr_ref, ctli_ref, obsr_ref, obsi_ref, alr_ref, ali_ref, *, chunk):
    lw = chunk * SSM_GROUP
    nbits = chunk.bit_length()
    dt = jnp.exp(ldt_ref[0])

    def discretise(lr, li):
        mag = jnp.exp(lr * dt)
        return mag * jnp.cos(li * dt), mag * jnp.sin(li * dt)

    lr, li = lrc_ref[0], lic_ref[0]
    ar, ai = discretise(lr, li)
    den = lr * lr + li * li
    nr, ni = ar - 1.0, ai
    cr, ci = (nr * lr + ni * li) / den, (ni * lr - nr * li) / den
    bbr, bbi = _cmul(cr, ci, btr_ref[0], bti_ref[0])
    sig = lax.broadcasted_iota(jnp.int32, (1, 2 * lw), 1) // SSM_GROUP
    er, ei = _cpow_by_bits(ar, ai, jnp.maximum(chunk - 1 - sig, 0), nbits)
    hr, hi = _cmul(er, ei, bbr, bbi)
    hr = jnp.where(sig < chunk, hr, 0.0)
    hi = jnp.where(sig < chunk, hi, 0.0)
    ctlr_ref[0] = hr[:, :lw]
    ctli_ref[0] = hi[:, :lw]
    c_re, c_im = cre_ref[0], cim_ref[0]
    hp = lax.Precision.HIGHEST
    for t in range(chunk):
        off = (chunk - 1 - t) * SSM_GROUP
        toep_ref[0, t * SSM_GROUP:(t + 1) * SSM_GROUP, :] = (
            jnp.dot(c_re, hr[:, off:off + lw], precision=hp, preferred_element_type=F32)
            - jnp.dot(c_im, hi[:, off:off + lw], precision=hp, preferred_element_type=F32))

    arr, air = discretise(lrr_ref[0], lir_ref[0])
    tp1 = lax.broadcasted_iota(jnp.int32, (lw, 1), 0) // SSM_GROUP + 1
    pr, pi = _cpow_by_bits(arr, air, tp1, nbits)
    c_r, c_i = ctr_ref[0], cti_ref[0]
    obsr_ref[0] = c_r * pr - c_i * pi
    obsi_ref[0] = -(c_r * pi + c_i * pr)
    lr_, li_ = arr, air
    for _ in range(chunk.bit_length() - 1):
        lr_, li_ = _cmul(lr_, li_, lr_, li_)
    alr_ref[0] = lr_
    ali_ref[0] = li_


def _ssm_prep(lam_re, lam_im, log_dt, b_re, b_im, c_re, c_im, *, chunk):
    assert chunk & (chunk - 1) == 0
    g, p, j = N_GROUPS, STATE_DIM, SSM_GROUP
    lw = chunk * j
    tile_b = lambda b: jnp.tile(b, (1, 1, 2 * chunk))
    tile_c = lambda c: jnp.tile(c, (1, chunk, 1))
    ins = [log_dt.reshape(g, 1, 1), lam_re.reshape(g, p, 1), lam_im.reshape(g, p, 1),
           lam_re.reshape(g, 1, p), lam_im.reshape(g, 1, p), tile_b(b_re), tile_b(b_im), c_re, c_im,
           tile_c(c_re), tile_c(c_im)]
    grp = lambda a, b: pl.BlockSpec((1, a, b), lambda i: (i, 0, 0))
    in_specs = [grp(1, 1), grp(p, 1), grp(p, 1), grp(1, p), grp(1, p), grp(p, 2 * lw), grp(p, 2 * lw),
                grp(j, p), grp(j, p), grp(lw, p), grp(lw, p)]
    shapes = [(lw, lw), (p, lw), (p, lw), (lw, p), (lw, p), (1, p), (1, p)]
    return pl.pallas_call(
        functools.partial(_ssm_prep_kernel, chunk=chunk), grid=(g,), in_specs=in_specs,
        out_specs=[grp(*s) for s in shapes],
        out_shape=[jax.ShapeDtypeStruct((g,) + s, F32) for s in shapes],
        compiler_params=_params("arbitrary"), name=f"ssm_prep_{chunk}")(*ins)


def _ssm_chunk_kernel(*refs, n_chunks, n_seq, has_h0, precise):
    if has_h0:
        (u_ref, toep_ref, ctlr_ref, ctli_ref, obsr_ref, obsi_ref, alr_ref, ali_ref, h0r_ref, h0i_ref,
         y_ref, hr_ref, hi_ref) = refs
    else:
        (u_ref, toep_ref, ctlr_ref, ctli_ref, obsr_ref, obsi_ref, alr_ref, ali_ref,
         y_ref, hr_ref, hi_ref) = refs
    if precise:
        cast = lambda x: x
        mm = lambda a, b: lax.dot_general(a, b, _NT, precision=lax.Precision.HIGHEST, preferred_element_type=F32)
    else:
        cast = lambda x: x.astype(BF16)
        mm = _dot_nt
    u = cast(u_ref[0])
    y = mm(u, cast(toep_ref[0]))
    hr = mm(u, cast(ctlr_ref[0]))
    hi = mm(u, cast(ctli_ref[0]))
    ar, ai = alr_ref[0], ali_ref[0]
    if n_chunks == 1:
        pr, pi = h0r_ref[0], h0i_ref[0]
        er, ei = _cmul(ar, ai, pr, pi)
        hr, hi = hr + er, hi + ei
        hr_ref[0] = hr
        hi_ref[0] = hi
    else:
        assert not has_h0
        rows = hr.shape[0]
        cidx = lax.broadcasted_iota(jnp.int32, (rows, 1), 0) % n_chunks

        def shifted(x, d):
            return jnp.where(cidx >= d, pltpu.roll(x, d, axis=0), 0.0)

        d = 1
        while d < n_chunks:
            sr, si = shifted(hr, d), shifted(hi, d)
            er, ei = _cmul(ar, ai, sr, si)
            hr, hi = hr + er, hi + ei
            ar, ai = _cmul(ar, ai, ar, ai)
            d *= 2
        for b in range(n_seq):
            last = (b + 1) * n_chunks - 1
            hr_ref[0, b:b + 1, :] = hr[last:last + 1, :]
            hi_ref[0, b:b + 1, :] = hi[last:last + 1, :]
        pr, pi = shifted(hr, 1), shifted(hi, 1)
    y = y + mm(cast(pr), cast(obsr_ref[0])) + mm(cast(pi), cast(obsi_ref[0]))
    y_ref[0] = y


def _ssm_chunked(u, ops, h0, *, chunk, precise):
    n_seq, t, _ = u.shape
    g, p, j = N_GROUPS, STATE_DIM, SSM_GROUP
    n_chunks = t // chunk
    rows, lw = n_seq * n_chunks, chunk * j
    ug = u.reshape(rows, chunk, g, j).transpose(2, 0, 1, 3).reshape(g, rows, lw)
    if not precise:
        ug = ug.astype(BF16)
    grp = lambda a, b: pl.BlockSpec((1, a, b), lambda i: (i, 0, 0))
    ins = [ug, *ops]
    in_specs = [grp(rows, lw), grp(lw, lw), grp(p, lw), grp(p, lw), grp(lw, p), grp(lw, p), grp(1, p), grp(1, p)]
    if h0 is not None:
        ins += [jnp.swapaxes(h0[0], 0, 1), jnp.swapaxes(h0[1], 0, 1)]
        in_specs += [grp(n_seq, p), grp(n_seq, p)]
    y, hr, hi = pl.pallas_call(
        functools.partial(_ssm_chunk_kernel, n_chunks=n_chunks, n_seq=n_seq, has_h0=h0 is not None, precise=precise),
        grid=(g,), in_specs=in_specs,
        out_specs=[grp(rows, lw), grp(n_seq, p), grp(n_seq, p)],
        out_shape=[jax.ShapeDtypeStruct((g, rows, lw), F32), jax.ShapeDtypeStruct((g, n_seq, p), F32),
                   jax.ShapeDtypeStruct((g, n_seq, p), F32)],
        compiler_params=_params("arbitrary"), name=f"ssm_chunk_{chunk}")(*ins)
    y = y.reshape(g, rows, chunk, j).transpose(1, 2, 0, 3).reshape(n_seq, t, SSM_WIDTH)
    return y, jnp.swapaxes(hr, 0, 1), jnp.swapaxes(hi, 0, 1)


GROUPS_PER_TILE = LANES // SSM_GROUP


def _block_transpose(xs):
    n = len(xs)
    blk = lax.broadcasted_iota(jnp.int32, xs[0].shape, 1) // SSM_GROUP
    xs = list(xs)
    d = n // 2
    while d:
        upper = (blk & d) != 0
        for i in range(n):
            if not i & d:
                lo, hi = xs[i], xs[i + d]
                xs[i] = jnp.where(upper, pltpu.roll(hi, SSM_GROUP * d, axis=1), lo)
                xs[i + d] = jnp.where(upper, hi, pltpu.roll(lo, LANES - SSM_GROUP * d, axis=1))
        d //= 2
    return xs


def _ssm_seq_kernel(u_ref, toep_ref, ctlr_ref, ctli_ref, obsr_ref, obsi_ref, alr_ref, ali_ref,
                    y_ref, h_ref, ug_ref, yg_ref, *, chunk):
    t = u_ref.shape[2]
    rows = t // chunk
    lw = chunk * SSM_GROUP
    n = GROUPS_PER_TILE
    p = STATE_DIM
    ridx = lax.broadcasted_iota(jnp.int32, (rows, 1), 0)
    re_half = lax.broadcasted_iota(jnp.int32, (1, 2 * p), 1) < p

    def shifted(x, d):
        return jnp.where(ridx >= d, pltpu.roll(x, d, axis=0), 0.0)

    def group_tile(v, carry):
        for w in range(lw // LANES):
            xs = [u_ref[0, v, pl.ds(n * w + k, rows, stride=chunk), :] for k in range(n)]
            for k, x in enumerate(_block_transpose(xs)):
                ug_ref[k, :, w * LANES:(w + 1) * LANES] = x.astype(BF16)

        def group(k, c):
            g = v * n + k
            ug = ug_ref[k]
            y = _dot_nt(ug, toep_ref[g])
            ctl = jnp.concatenate([ctlr_ref[g], ctli_ref[g]], axis=0)
            h = _dot_nt(ug, ctl)
            ar = jnp.concatenate([alr_ref[g], alr_ref[g]], axis=1)
            ai = jnp.concatenate([ali_ref[g], ali_ref[g]], axis=1)
            d = 1
            while d < rows:
                s = shifted(h, d)
                h = h + ar * s + jnp.where(re_half, -ai, ai) * pltpu.roll(s, p, axis=1)
                ar, ai = ar * ar - ai * ai, 2.0 * ar * ai
                d *= 2
            h_ref[0, pl.ds(g, 1), :] = h[rows - 1:rows, :]
            obs = jnp.concatenate([obsr_ref[g], obsi_ref[g]], axis=1)
            yg_ref[k] = y + _dot_nt(shifted(h, 1).astype(BF16), obs)
            return c

        lax.fori_loop(0, n, group, 0)
        for w in range(lw // LANES):
            ys = [yg_ref[k, :, w * LANES:(w + 1) * LANES] for k in range(n)]
            for k, y in enumerate(_block_transpose(ys)):
                y_ref[0, v, pl.ds(n * w + k, rows, stride=chunk), :] = y
        return carry

    lax.fori_loop(0, N_GROUPS // n, group_tile, 0)


def _ssm_seq(u, ops, *, chunk):
    n_seq, _, t, _ = u.shape
    g, p = N_GROUPS, STATE_DIM
    lw = chunk * SSM_GROUP
    assert lw % LANES == 0 and chunk == GROUPS_PER_TILE * (lw // LANES) and SSM_WIDTH == g * SSM_GROUP
    ops = [o.astype(BF16) for o in ops[:5]] + list(ops[5:])
    whole = lambda a: pl.BlockSpec(a.shape, lambda b: (0, 0, 0), pipeline_mode=pl.Buffered(1))
    seq = pl.BlockSpec((1, SSM_WIDTH // LANES, t, LANES), lambda b: (b, 0, 0, 0))
    y, h = pl.pallas_call(
        functools.partial(_ssm_seq_kernel, chunk=chunk), grid=(n_seq,),
        in_specs=[seq] + [whole(o) for o in ops],
        out_specs=[seq, pl.BlockSpec((1, g, 2 * p), lambda b: (b, 0, 0))],
        out_shape=[jax.ShapeDtypeStruct(u.shape, F32), jax.ShapeDtypeStruct((n_seq, g, 2 * p), F32)],
        scratch_shapes=[pltpu.VMEM((GROUPS_PER_TILE, t // chunk, lw), BF16),
                        pltpu.VMEM((GROUPS_PER_TILE, t // chunk, lw), F32)],
        compiler_params=_params("arbitrary"), name="ssm_seq")(u, *ops)
    return y, h[:, :, :p], h[:, :, p:]


FF_CHUNK = 1024


def _post_kernel(x_ref, att_ref, ys_ref, u_ref, p_ref, lng_ref, lnb_ref, dsk_ref, wglu_ref, bglu_ref,
                 woa_ref, wos_ref, l1g_ref, l1b_ref, wup_ref, wdn_ref, wpe_ref, wpg_ref, bpg_ref,
                 l2g_ref, l2b_ref, o_ref):
    h = _layer_norm(x_ref[0], lng_ref[...], lnb_ref[...])
    y = _gelu_tanh(_load_lane_tiles(ys_ref) + dsk_ref[...] * _load_lane_tiles(u_ref))
    y = y * _sigmoid(_dot(y.astype(BF16), wglu_ref[...]) + bglu_ref[...])
    mix = _dot(att_ref[0], woa_ref[...]) + _dot(y.astype(BF16), wos_ref[...])
    h1 = _layer_norm(ALPHA * h + mix, l1g_ref[...], l1b_ref[...])
    h1b = h1.astype(BF16)
    e = _sigmoid(_dot(h1b, wpg_ref[...]) + bpg_ref[...]) * _dot(p_ref[0].astype(BF16), wpe_ref[...])
    acc = ALPHA * h1 + e
    for c in range(D_FF // FF_CHUNK):
        cs = slice(c * FF_CHUNK, (c + 1) * FF_CHUNK)
        a = jnp.maximum(_dot(h1b, wup_ref[:, cs]), 0.0)
        acc = acc + _dot((a * a).astype(BF16), wdn_ref[cs, :])
    o_ref[0] = _layer_norm(acc, l2g_ref[...], l2b_ref[...])


def _post_mixer(x, att, ys, u, p, weights, *, tm):
    nb, t, _ = x.shape
    tok = lambda w: pl.BlockSpec((1, tm, w), lambda b, i: (b, i, 0))
    const = lambda a: pl.BlockSpec(a.shape, lambda b, i: (0, 0), pipeline_mode=pl.Buffered(1))
    return pl.pallas_call(
        _post_kernel, grid=(nb, t // tm),
        in_specs=[tok(D_MODEL), tok(ATTN_WIDTH), _lane_tiles_spec(tm, SSM_WIDTH), _lane_tiles_spec(tm, SSM_WIDTH),
                  tok(PLE_DIM)] + [const(w) for w in weights],
        out_specs=tok(D_MODEL), out_shape=jax.ShapeDtypeStruct((nb, t, D_MODEL), F32),
        compiler_params=_params("arbitrary", "arbitrary"), name="post_mixer")(x, att, ys, u, p, *weights)


def kernel(x_prompt, x_sample, cache_k, cache_v, cache_logf, state_re, state_im, page_table, p_prompt, p_sample,
           ln_in_g, ln_in_b, w_in, b_f, lam_re, lam_im, log_dt, b_re, b_im, c_re, c_im, d_skip, w_glu, b_glu,
           w_out, ln1_g, ln1_b, w_up, w_down, w_pe, w_pg, b_pg, ln2_g, ln2_b):
    assert w_in.shape[0] == 1, "one trunk layer"
    nb, t, _ = x_prompt.shape
    db, nq, _ = x_sample.shape
    row = lambda a: a.reshape(1, -1)
    post_w = [row(ln_in_g), row(ln_in_b), row(d_skip[0]), w_glu[0].astype(BF16), row(b_glu[0]),
              w_out[0, :ATTN_WIDTH].astype(BF16), w_out[0, ATTN_WIDTH:].astype(BF16), row(ln1_g[0]), row(ln1_b[0]),
              w_up[0].astype(BF16), w_down[0].astype(BF16), w_pe[0].astype(BF16), w_pg[0].astype(BF16),
              row(b_pg[0]), row(ln2_g[0]), row(ln2_b[0])]
    ssm_par = (lam_re[0], lam_im[0], log_dt[0], b_re[0], b_im[0], c_re[0], c_im[0])

    u, kT, vT, lfT, cT, qTb, kb, vTb, cp = _in_proj(x_prompt, ln_in_g, ln_in_b, w_in[0], b_f[0],
                                                     tm=512, q_scale=QK_SCALE * LOG2E, attn_layouts=True)
    att = _fox_prompt(qTb, kb, cp, vTb, cT, tq=256)
    ys, sr, si = _ssm_seq(u, _ssm_prep(*ssm_par, chunk=16), chunk=16)
    y_prompt = _post_mixer(x_prompt, att, ys, u, p_prompt[0], post_w, tm=512)
    heads_last = lambda a: jnp.transpose(a.reshape(nb, N_HEADS, HEAD_DIM, t), (0, 3, 1, 2))[None]
    prompt_out = (heads_last(kT), heads_last(vT), jnp.swapaxes(lfT, 1, 2)[None], sr[None], si[None])

    us, qs, ks, vs, logfs = _in_proj(x_sample.reshape(1, db * nq, D_MODEL), ln_in_g, ln_in_b, w_in[0], b_f[0],
                                     tm=db * nq, q_scale=QK_SCALE, attn_layouts=False)
    seq = lambda a: a.reshape(db, nq, a.shape[-1])
    to_tiles = lambda a: jnp.swapaxes(a.reshape(1, db * nq, SSM_WIDTH // LANES, LANES), 1, 2)
    from_tiles = lambda a: jnp.swapaxes(a, 1, 2).reshape(db, nq, SSM_WIDTH)
    att_s = _fox_sample(seq(qs), seq(ks), seq(vs), seq(logfs), cache_k[0], cache_v[0], cache_logf[0], page_table)
    ys_s, sr_s, si_s = _ssm_chunked(from_tiles(us), _ssm_prep(*ssm_par, chunk=nq), (state_re[0], state_im[0]),
                                    chunk=nq, precise=True)
    flat = lambda a: a.reshape(1, db * nq, a.shape[-1])
    y_sample = _post_mixer(flat(x_sample), flat(att_s).astype(BF16), to_tiles(ys_s), us, flat(p_sample[0]),
                           post_w, tm=db * nq).reshape(db, nq, D_MODEL)
    sample_out = (ks.reshape(1, db, nq, N_HEADS, HEAD_DIM), vs.reshape(1, db, nq, N_HEADS, HEAD_DIM),
                  logfs.reshape(1, db, nq, N_HEADS), sr_s[None], si_s[None])
    return (y_prompt, y_sample) + prompt_out + sample_out
```

```python
import functools
import math

import jax
import jax.numpy as jnp
from jax import lax
from jax.experimental import pallas as pl
from jax.experimental.pallas import tpu as pltpu

F32 = jnp.float32
BF16 = jnp.bfloat16

D_MODEL = 1024
ATTN_WIDTH = 512
SSM_WIDTH = 512
HEAD_DIM = 64
N_HEADS = 8
SSM_GROUP = 16
N_GROUPS = 32
STATE_DIM = 64
D_FF = 4096
PLE_DIM = 256
PAGE_SIZE = 128
ALPHA = 2.0 ** 0.25
LN_EPS = 1e-5
NEG_INF = -1e30
QK_SCALE = HEAD_DIM ** -0.5
LOG2E = math.log2(math.e)

VMEM_LIMIT_BYTES = 56 * 1024 * 1024

_NT = (((1,), (1,)), ((), ()))


def _params(*sem):
    return pltpu.CompilerParams(dimension_semantics=sem, vmem_limit_bytes=VMEM_LIMIT_BYTES)


def _const_spec(shape):
    return pl.BlockSpec(shape, lambda *_: (0,) * len(shape))


def _layer_norm(x, g, b):
    mu = jnp.mean(x, axis=-1, keepdims=True)
    xc = x - mu
    var = jnp.mean(xc * xc, axis=-1, keepdims=True)
    return xc * lax.rsqrt(var + LN_EPS) * g + b


def _log_sigmoid(x):
    return jnp.minimum(x, 0.0) - jnp.log1p(jnp.exp(-jnp.abs(x)))


def _sigmoid(x):
    return 1.0 / (1.0 + jnp.exp(-x))


def _gelu_tanh(x):
    return 0.5 * x * (1.0 + jnp.tanh(math.sqrt(2.0 / math.pi) * (x + 0.044715 * (x * x * x))))


def _dot(a, b):
    return jnp.dot(a, b, preferred_element_type=F32)


def _dot_nt(a, b):
    return lax.dot_general(a, b, _NT, preferred_element_type=F32)


def _split3(x):
    hi = x.astype(BF16)
    r1 = x - hi.astype(F32)
    mid = r1.astype(BF16)
    lo = (r1 - mid.astype(F32)).astype(BF16)
    return hi, mid, lo


LANES = 128


def _lane_tiles_spec(tm, width):
    return pl.BlockSpec((1, width // LANES, tm, LANES), lambda b, i: (b, 0, i, 0))


def _store_lane_tiles(ref, x):
    for v in range(x.shape[-1] // LANES):
        ref[0, v] = x[:, v * LANES:(v + 1) * LANES]


def _load_lane_tiles(ref):
    return jnp.concatenate([ref[0, v] for v in range(ref.shape[1])], axis=-1)


def _dot_exact01(x, m01):
    hi, mid, lo = _split3(x)
    return _dot(hi, m01) + _dot(mid, m01) + _dot(lo, m01)


N_BIAS_ROWS = 128
ONES_ROW = 3 * N_HEADS
SUM_ROWS = 16


def _in_proj_kernel(*refs, tm, q_scale, attn_layouts):
    x_ref, g_ref, b_ref, wu_ref = refs[:4]
    if attn_layouts:
        (wqT_ref, wkT_ref, wvT_ref, wfT_ref, bfc_ref, tri_ref,
         u_ref, kT_ref, vT_ref, lfT_ref, cT_ref, qTb_ref, kb_ref, vTb_ref, cp_ref, carry_ref) = refs[4:]
    else:
        wq_ref, wk_ref, wv_ref, wf_ref, bf_ref, u_ref, q_ref, k_ref, v_ref, lf_ref = refs[4:]
    hb = _layer_norm(x_ref[0], g_ref[...], b_ref[...]).astype(BF16)
    _store_lane_tiles(u_ref, _dot(hb, wu_ref[...]))
    if attn_layouts:
        qTb_ref[0] = (_dot_nt(wqT_ref[...], hb) * q_scale).astype(BF16)
        kT = _dot_nt(wkT_ref[...], hb)
        kT_ref[0] = kT
        kb_ref[0] = kT.T.astype(BF16)
        vT = _dot_nt(wvT_ref[...], hb)
        vT_ref[0] = vT
        vTb_ref[0] = vT.astype(BF16)
        lfT = _log_sigmoid(_dot_nt(wfT_ref[...], hb) + bfc_ref[...])
        lfT_ref[0] = lfT

        @pl.when(pl.program_id(1) == 0)
        def _():
            carry_ref[...] = jnp.zeros_like(carry_ref)

        c = carry_ref[...] + _dot_exact01(lfT, tri_ref[...])
        cT_ref[0] = c
        carry_ref[...] = c[:, tm - 1:tm]
        hi, mid, lo = _split3(c * LOG2E)
        row = lax.broadcasted_iota(jnp.int32, (N_BIAS_ROWS - ONES_ROW, tm), 0)
        ones = jnp.where(row < 3, 1.0, 0.0)
        pieces = jnp.concatenate([hi.astype(F32), mid.astype(F32), lo.astype(F32), ones], axis=0)
        cp_ref[0] = pieces.T.astype(BF16)
    else:
        q_ref[0] = _dot(hb, wq_ref[...]) * q_scale
        k_ref[0] = _dot(hb, wk_ref[...])
        v_ref[0] = _dot(hb, wv_ref[...])
        lf_ref[0] = _log_sigmoid(_dot(hb, wf_ref[...]) + bf_ref[...])


def _in_proj(x3, ln_g, ln_b, w_in, b_f, *, tm, q_scale, attn_layouts):
    nb, t, _ = x3.shape
    wq = w_in[:, :ATTN_WIDTH].astype(BF16)
    wk = w_in[:, ATTN_WIDTH:2 * ATTN_WIDTH].astype(BF16)
    wv = w_in[:, 2 * ATTN_WIDTH:3 * ATTN_WIDTH].astype(BF16)
    wf = w_in[:, 3 * ATTN_WIDTH:3 * ATTN_WIDTH + N_HEADS].astype(BF16)
    wu = w_in[:, 3 * ATTN_WIDTH + N_HEADS:].astype(BF16)
    tok = lambda w: pl.BlockSpec((1, tm, w), lambda b, i: (b, i, 0))
    feat = lambda w: pl.BlockSpec((1, w, tm), lambda b, i: (b, 0, i))
    sds = jax.ShapeDtypeStruct
    ins = [x3, ln_g.reshape(1, -1), ln_b.reshape(1, -1), wu]
    in_specs = [tok(D_MODEL), _const_spec((1, D_MODEL)), _const_spec((1, D_MODEL)), _const_spec((D_MODEL, SSM_WIDTH))]
    out_shape = [sds((nb, SSM_WIDTH // LANES, t, LANES), F32)]
    out_specs = [_lane_tiles_spec(tm, SSM_WIDTH)]
    scratch = []
    if attn_layouts:
        tri = (jnp.arange(tm)[:, None] <= jnp.arange(tm)[None, :]).astype(BF16)
        ins += [wq.T, wk.T, wv.T, wf.T, b_f.reshape(N_HEADS, 1), tri]
        in_specs += [_const_spec((ATTN_WIDTH, D_MODEL))] * 3
        in_specs += [_const_spec((N_HEADS, D_MODEL)), _const_spec((N_HEADS, 1)), _const_spec((tm, tm))]
        out_shape += [sds((nb, ATTN_WIDTH, t), F32), sds((nb, ATTN_WIDTH, t), F32),
                      sds((nb, N_HEADS, t), F32), sds((nb, N_HEADS, t), F32),
                      sds((nb, ATTN_WIDTH, t), BF16), sds((nb, t, ATTN_WIDTH), BF16), sds((nb, ATTN_WIDTH, t), BF16),
                      sds((nb, t, N_BIAS_ROWS), BF16)]
        out_specs += [feat(ATTN_WIDTH), feat(ATTN_WIDTH), feat(N_HEADS), feat(N_HEADS),
                      feat(ATTN_WIDTH), tok(ATTN_WIDTH), feat(ATTN_WIDTH), tok(N_BIAS_ROWS)]
        scratch = [pltpu.VMEM((N_HEADS, 1), F32)]
    else:
        ins += [wq, wk, wv, wf, b_f.reshape(1, N_HEADS)]
        in_specs += [_const_spec((D_MODEL, ATTN_WIDTH))] * 3 + [_const_spec((D_MODEL, N_HEADS)), _const_spec((1, N_HEADS))]
        out_shape += [sds((nb, t, ATTN_WIDTH), F32)] * 3 + [sds((nb, t, N_HEADS), F32)]
        out_specs += [tok(ATTN_WIDTH)] * 3 + [tok(N_HEADS)]
    return pl.pallas_call(
        functools.partial(_in_proj_kernel, tm=tm, q_scale=q_scale, attn_layouts=attn_layouts),
        grid=(nb, t // tm), in_specs=in_specs, out_specs=out_specs, out_shape=out_shape,
        scratch_shapes=scratch, compiler_params=_params("arbitrary", "arbitrary"),
        name="in_proj_prompt" if attn_layouts else "in_proj_sample")(*ins)


def _fox_prompt_kernel(qT_ref, k_ref, cp_ref, vT_ref, cT_ref, o_ref, bq_ref, m_ref, acc_ref, s0_ref, s1_ref, *, tq):
    qi = pl.program_id(1)
    q0 = pl.multiple_of(qi * tq, tq)
    heads = [slice(h * HEAD_DIM, (h + 1) * HEAD_DIM) for h in range(N_HEADS)]
    pairs = [slice(2 * HEAD_DIM * (h // 2), 2 * HEAD_DIM * (h // 2 + 1)) for h in range(N_HEADS)]

    c0 = cT_ref[0, :, pl.ds(q0, tq)][:, 0:1] * LOG2E
    c0_pieces = [p.astype(F32) for p in _split3(c0)]
    r = lax.broadcasted_iota(jnp.int32, (2 * HEAD_DIM, tq), 0)
    for h in range(N_HEADS):
        mine = (r < HEAD_DIM) if h % 2 == 0 else (r >= HEAD_DIM)
        bq_ref[h, 0:2 * HEAD_DIM, :] = jnp.where(mine, qT_ref[0, pairs[h], :], jnp.zeros((), BF16))
        sel = jnp.where((r == h) | (r == N_HEADS + h) | (r == 2 * N_HEADS + h), -1.0, 0.0)
        for j, piece in enumerate(c0_pieces):
            sel = jnp.where(r == ONES_ROW + j, piece[h:h + 1, :], sel)
        bq_ref[h, 2 * HEAD_DIM:, :] = sel.astype(BF16)

    key_row = lax.broadcasted_iota(jnp.int32, (tq, tq), 0)
    qry_col = lax.broadcasted_iota(jnp.int32, (tq, tq), 1)

    m_ref[...] = jnp.full_like(m_ref, NEG_INF)
    acc_ref[...] = jnp.zeros_like(acc_ref)
    ones = jnp.ones((SUM_ROWS, tq), BF16)

    s_refs = (s0_ref, s1_ref)

    def scores(kt, h, buf):
        ks = pl.multiple_of(kt * tq, tq)
        a = jnp.concatenate([k_ref[0, pl.ds(ks, tq), pairs[h]], cp_ref[0, pl.ds(ks, tq), :]], axis=1)
        s_refs[buf][h] = _dot(a, bq_ref[h])

    def accumulate(kt, h, buf, diagonal):
        ks = pl.multiple_of(kt * tq, tq)
        s = s_refs[buf][h]
        if diagonal:
            s = jnp.where(key_row <= qry_col, s, NEG_INF)
        m_new = jnp.maximum(m_ref[h], jnp.max(s, axis=0, keepdims=True))
        alpha = jnp.exp2(m_ref[h] - m_new)
        m_ref[h] = m_new
        p = jnp.exp2((s - m_new).astype(BF16))
        v1 = jnp.concatenate([vT_ref[0, heads[h], pl.ds(ks, tq)], ones], axis=0)
        acc_ref[h] = alpha * acc_ref[h] + _dot(v1, p)

    def stage(kt, cur, *, diagonal=False, issue_next=True):
        for h in range(N_HEADS + 1):
            if issue_next and h < N_HEADS:
                scores(kt + 1, h, 1 - cur)
            if h >= 1:
                accumulate(kt, h - 1, cur, diagonal)

    for h in range(N_HEADS):
        scores(0, h, 0)

    def body(j, carry):
        stage(2 * j, 0)
        stage(2 * j + 1, 1)
        return carry

    lax.fori_loop(0, qi // 2, body, 0)

    @pl.when(qi % 2 == 0)
    def _():
        stage(qi, 0, diagonal=True, issue_next=False)

    @pl.when(qi % 2 == 1)
    def _():
        stage(qi - 1, 0)
        stage(qi, 1, diagonal=True, issue_next=False)

    outs = [acc_ref[h, 0:HEAD_DIM, :] / acc_ref[h, HEAD_DIM:HEAD_DIM + 1, :] for h in range(N_HEADS)]
    o_ref[0] = jnp.concatenate(outs, axis=0).T.astype(o_ref.dtype)


def _fox_prompt(qT, k, cp, vT, cT, *, tq):
    nb, t, _ = k.shape
    assert N_BIAS_ROWS == 2 * HEAD_DIM
    whole_f = lambda w: pl.BlockSpec((1, w, t), lambda b, i: (b, 0, 0))
    whole_t = lambda w: pl.BlockSpec((1, t, w), lambda b, i: (b, 0, 0))
    return pl.pallas_call(
        functools.partial(_fox_prompt_kernel, tq=tq),
        grid=(nb, t // tq),
        in_specs=[pl.BlockSpec((1, ATTN_WIDTH, tq), lambda b, i: (b, 0, i)),
                  whole_t(ATTN_WIDTH), whole_t(N_BIAS_ROWS), whole_f(ATTN_WIDTH), whole_f(N_HEADS)],
        out_specs=pl.BlockSpec((1, tq, ATTN_WIDTH), lambda b, i: (b, i, 0)),
        out_shape=jax.ShapeDtypeStruct((nb, t, ATTN_WIDTH), BF16),
        scratch_shapes=[pltpu.VMEM((N_HEADS, 2 * HEAD_DIM + N_BIAS_ROWS, tq), BF16),
                        pltpu.VMEM((N_HEADS, 1, tq), F32), pltpu.VMEM((N_HEADS, HEAD_DIM + SUM_ROWS, tq), F32),
                        pltpu.VMEM((N_HEADS, tq, tq), F32), pltpu.VMEM((N_HEADS, tq, tq), F32)],
        compiler_params=_params("arbitrary", "arbitrary"), name="fox_prompt")(qT, k, cp, vT, cT)


def _fox_sample_kernel(pt_ref, q_ref, kn_ref, vn_ref, lfn_ref, sl_ref, *refs, n_pages, nq):
    del pt_ref
    k_refs, v_refs, pf_refs = refs[:n_pages], refs[n_pages:2 * n_pages], refs[2 * n_pages:3 * n_pages]
    o_ref = refs[3 * n_pages]
    nr = nq * N_HEADS
    head_of_lane = lax.broadcasted_iota(jnp.int32, (N_HEADS, ATTN_WIDTH), 1) // HEAD_DIM
    own = head_of_lane == lax.broadcasted_iota(jnp.int32, (N_HEADS, ATTN_WIDTH), 0)
    q = q_ref[0]
    qbd32 = jnp.concatenate([jnp.where(own, q[i:i + 1, :], 0.0) for i in range(nq)], axis=0)
    qbd = qbd32.astype(BF16)
    rep = lambda e: jnp.concatenate([e] * nq, axis=0)
    rows = lambda j: slice(j * N_HEADS, (j + 1) * N_HEADS)

    lfn = lfn_ref[0]
    e_new = rep(_dot_exact01(lfn, sl_ref[...]))
    kn, vn = kn_ref[0], vn_ref[0]
    qidx = lax.broadcasted_iota(jnp.int32, (nr, 1), 0) // N_HEADS
    s_new = [jnp.where(qidx >= j, jnp.sum(qbd32 * kn[j:j + 1, :], axis=-1, keepdims=True) + e_new[:, j:j + 1], NEG_INF)
             for j in range(nq)]

    pf_all = jnp.concatenate([r[0, 0] for r in pf_refs], axis=0)
    sfx_all = _dot_exact01(pf_all, sl_ref[...])
    tot_all = sfx_all[:, 0:1] + pf_all[:, 0:1]
    tail = jnp.sum(lfn, axis=-1, keepdims=True)
    scores = [None] * n_pages
    for j in reversed(range(n_pages)):
        kt = k_refs[j][0, 0].reshape(ATTN_WIDTH, PAGE_SIZE).astype(BF16)
        scores[j] = _dot(qbd, kt) + rep(sfx_all[rows(j)] + tail)
        tail = tail + tot_all[rows(j)]
    s_all = jnp.concatenate(scores, axis=1)
    m = jnp.max(s_all, axis=-1, keepdims=True)
    for s in s_new:
        m = jnp.maximum(m, s)
    p_all = jnp.exp(s_all - m)
    l = jnp.sum(p_all, axis=-1, keepdims=True)
    acc = jnp.zeros((nr, ATTN_WIDTH), F32)
    for j, s in enumerate(s_new):
        p = jnp.exp(s - m)
        l = l + p
        acc = acc + p * vn[j:j + 1, :]
    p_all = p_all.astype(BF16)
    for j in range(n_pages):
        vt = v_refs[j][0, 0].reshape(ATTN_WIDTH, PAGE_SIZE).astype(BF16)
        acc = acc + _dot_nt(p_all[:, j * PAGE_SIZE:(j + 1) * PAGE_SIZE], vt)
    acc = acc / l
    for i in range(nq):
        o_ref[0, i:i + 1, :] = jnp.sum(jnp.where(own, acc[i * N_HEADS:(i + 1) * N_HEADS], 0.0), axis=0, keepdims=True)


def _fox_sample(q, k_new, v_new, logf_new, cache_k, cache_v, cache_logf, page_table):
    db, nq, _ = q.shape
    n_pages = page_table.shape[1]
    kc = jnp.transpose(cache_k, (0, 2, 3, 1))
    vc = jnp.transpose(cache_v, (0, 2, 3, 1))
    pf = jnp.transpose(cache_logf, (0, 2, 1))
    lfn_t = jnp.swapaxes(jnp.pad(logf_new, ((0, 0), (0, PAGE_SIZE - nq), (0, 0))), 1, 2)
    strict_lower = (jnp.arange(PAGE_SIZE)[:, None] > jnp.arange(PAGE_SIZE)[None, :]).astype(BF16)
    seq = lambda r, w: pl.BlockSpec((1, r, w), lambda b, pt: (b, 0, 0))
    page = lambda j: pl.BlockSpec((1, 1, N_HEADS, HEAD_DIM, PAGE_SIZE), lambda b, pt: (0, pt[b, j], 0, 0, 0))
    page_f = lambda j: pl.BlockSpec((1, 1, N_HEADS, PAGE_SIZE), lambda b, pt: (0, pt[b, j], 0, 0))
    in_specs = [seq(nq, ATTN_WIDTH), seq(nq, ATTN_WIDTH), seq(nq, ATTN_WIDTH), seq(N_HEADS, PAGE_SIZE),
                pl.BlockSpec((PAGE_SIZE, PAGE_SIZE), lambda b, pt: (0, 0))]
    in_specs += [page(j) for j in range(n_pages)] * 2 + [page_f(j) for j in range(n_pages)]
    return pl.pallas_call(
        functools.partial(_fox_sample_kernel, n_pages=n_pages, nq=nq),
        grid_spec=pltpu.PrefetchScalarGridSpec(
            num_scalar_prefetch=1, grid=(db,), in_specs=in_specs, out_specs=seq(nq, ATTN_WIDTH)),
        out_shape=jax.ShapeDtypeStruct((db, nq, ATTN_WIDTH), F32),
        compiler_params=_params("arbitrary"), name="fox_sample")(
            page_table, q, k_new, v_new, lfn_t, strict_lower,
            *([kc[None]] * n_pages), *([vc[None]] * n_pages), *([pf[None]] * n_pages))


def _cmul(ar, ai, br, bi):
    return ar * br - ai * bi, ar * bi + ai * br


def _cpow_by_bits(ar, ai, n, nbits):
    pr = jnp.ones(jnp.broadcast_shapes(ar.shape, n.shape), F32)
    pi = jnp.zeros_like(pr)
    for b in range(nbits):
        bit = ((n >> b) & 1) == 1
        fr = jnp.where(bit, ar, 1.0)
        fi = jnp.where(bit, ai, 0.0)
        pr, pi = _cmul(pr, pi, fr, fi)
        ar, ai = _cmul(ar, ai, ar, ai)
    return pr, pi


def _ssm_prep_kernel(ldt_ref, lrc_ref, lic_ref, lrr_ref, lir_ref, btr_ref, bti_ref, cre_ref, cim_ref,
                     ctr_ref, cti_ref, toep_ref, ctlr_ref, ctli_ref, obsr_ref, obsi_ref, alr_ref, ali_ref, *, chunk):
    lw = chunk * SSM_GROUP
    nbits = chunk.bit_length()
    dt = jnp.exp(ldt_ref[0])

    def discretise(lr, li):
        mag = jnp.exp(lr * dt)
        return mag * jnp.cos(li * dt), mag * jnp.sin(li * dt)

    lr, li = lrc_ref[0], lic_ref[0]
    ar, ai = discretise(lr, li)
    den = lr * lr + li * li
    nr, ni = ar - 1.0, ai
    cr, ci = (nr * lr + ni * li) / den, (ni * lr - nr * li) / den
    bbr, bbi = _cmul(cr, ci, btr_ref[0], bti_ref[0])
    sig = lax.broadcasted_iota(jnp.int32, (1, 2 * lw), 1) // SSM_GROUP
    er, ei = _cpow_by_bits(ar, ai, jnp.maximum(chunk - 1 - sig, 0), nbits)
    hr, hi = _cmul(er, ei, bbr, bbi)
    hr = jnp.where(sig < chunk, hr, 0.0)
    hi = jnp.where(sig < chunk, hi, 0.0)
    ctlr_ref[0] = hr[:, :lw]
    ctli_ref[0] = hi[:, :lw]
    c_re, c_im = cre_ref[0], cim_ref[0]
    hp = lax.Precision.HIGHEST
    for t in range(chunk):
        off = (chunk - 1 - t) * SSM_GROUP
        toep_ref[0, t * SSM_GROUP:(t + 1) * SSM_GROUP, :] = (
            jnp.dot(c_re, hr[:, off:off + lw], precision=hp, preferred_element_type=F32)
            - jnp.dot(c_im, hi[:, off:off + lw], precision=hp, preferred_element_type=F32))

    arr, air = discretise(lrr_ref[0], lir_ref[0])
    tp1 = lax.broadcasted_iota(jnp.int32, (lw, 1), 0) // SSM_GROUP + 1
    pr, pi = _cpow_by_bits(arr, air, tp1, nbits)
    c_r, c_i = ctr_ref[0], cti_ref[0]
    obsr_ref[0] = c_r * pr - c_i * pi
    obsi_ref[0] = -(c_r * pi + c_i * pr)
    lr_, li_ = arr, air
    for _ in range(chunk.bit_length() - 1):
        lr_, li_ = _cmul(lr_, li_, lr_, li_)
    alr_ref[0] = lr_
    ali_ref[0] = li_


def _ssm_prep(lam_re, lam_im, log_dt, b_re, b_im, c_re, c_im, *, chunk):
    assert chunk & (chunk - 1) == 0
    g, p, j = N_GROUPS, STATE_DIM, SSM_GROUP
    lw = chunk * j
    tile_b = lambda b: jnp.tile(b, (1, 1, 2 * chunk))
    tile_c = lambda c: jnp.tile(c, (1, chunk, 1))
    ins = [log_dt.reshape(g, 1, 1), lam_re.reshape(g, p, 1), lam_im.reshape(g, p, 1),
           lam_re.reshape(g, 1, p), lam_im.reshape(g, 1, p), tile_b(b_re), tile_b(b_im), c_re, c_im,
           tile_c(c_re), tile_c(c_im)]
    grp = lambda a, b: pl.BlockSpec((1, a, b), lambda i: (i, 0, 0))
    in_specs = [grp(1, 1), grp(p, 1), grp(p, 1), grp(1, p), grp(1, p), grp(p, 2 * lw), grp(p, 2 * lw),
                grp(j, p), grp(j, p), grp(lw, p), grp(lw, p)]
    shapes = [(lw, lw), (p, lw), (p, lw), (lw, p), (lw, p), (1, p), (1, p)]
    return pl.pallas_call(
        functools.partial(_ssm_prep_kernel, chunk=chunk), grid=(g,), in_specs=in_specs,
        out_specs=[grp(*s) for s in shapes],
        out_shape=[jax.ShapeDtypeStruct((g,) + s, F32) for s in shapes],
        compiler_params=_params("arbitrary"), name=f"ssm_prep_{chunk}")(*ins)


def _ssm_chunk_kernel(*refs, n_chunks, n_seq, has_h0, precise):
    if has_h0:
        (u_ref, toep_ref, ctlr_ref, ctli_ref, obsr_ref, obsi_ref, alr_ref, ali_ref, h0r_ref, h0i_ref,
         y_ref, hr_ref, hi_ref) = refs
    else:
        (u_ref, toep_ref, ctlr_ref, ctli_ref, obsr_ref, obsi_ref, alr_ref, ali_ref,
         y_ref, hr_ref, hi_ref) = refs
    if precise:
        cast = lambda x: x
        mm = lambda a, b: lax.dot_general(a, b, _NT, precision=lax.Precision.HIGHEST, preferred_element_type=F32)
    else:
        cast = lambda x: x.astype(BF16)
        mm = _dot_nt
    u = cast(u_ref[0])
    y = mm(u, cast(toep_ref[0]))
    hr = mm(u, cast(ctlr_ref[0]))
    hi = mm(u, cast(ctli_ref[0]))
    ar, ai = alr_ref[0], ali_ref[0]
    if n_chunks == 1:
        pr, pi = h0r_ref[0], h0i_ref[0]
        er, ei = _cmul(ar, ai, pr, pi)
        hr, hi = hr + er, hi + ei
        hr_ref[0] = hr
        hi_ref[0] = hi
    else:
        assert not has_h0
        rows = hr.shape[0]
        cidx = lax.broadcasted_iota(jnp.int32, (rows, 1), 0) % n_chunks

        def shifted(x, d):
            return jnp.where(cidx >= d, pltpu.roll(x, d, axis=0), 0.0)

        d = 1
        while d < n_chunks:
            sr, si = shifted(hr, d), shifted(hi, d)
            er, ei = _cmul(ar, ai, sr, si)
            hr, hi = hr + er, hi + ei
            ar, ai = _cmul(ar, ai, ar, ai)
            d *= 2
        for b in range(n_seq):
            last = (b + 1) * n_chunks - 1
            hr_ref[0, b:b + 1, :] = hr[last:last + 1, :]
            hi_ref[0, b:b + 1, :] = hi[last:last + 1, :]
        pr, pi = shifted(hr, 1), shifted(hi, 1)
    y = y + mm(cast(pr), cast(obsr_ref[0])) + mm(cast(pi), cast(obsi_ref[0]))
    y_ref[0] = y


def _ssm_chunked(u, ops, h0, *, chunk, precise):
    n_seq, t, _ = u.shape
    g, p, j = N_GROUPS, STATE_DIM, SSM_GROUP
    n_chunks = t // chunk
    rows, lw = n_seq * n_chunks, chunk * j
    ug = u.reshape(rows, chunk, g, j).transpose(2, 0, 1, 3).reshape(g, rows, lw)
    if not precise:
        ug = ug.astype(BF16)
    grp = lambda a, b: pl.BlockSpec((1, a, b), lambda i: (i, 0, 0))
    ins = [ug, *ops]
    in_specs = [grp(rows, lw), grp(lw, lw), grp(p, lw), grp(p, lw), grp(lw, p), grp(lw, p), grp(1, p), grp(1, p)]
    if h0 is not None:
        ins += [jnp.swapaxes(h0[0], 0, 1), jnp.swapaxes(h0[1], 0, 1)]
        in_specs += [grp(n_seq, p), grp(n_seq, p)]
    y, hr, hi = pl.pallas_call(
        functools.partial(_ssm_chunk_kernel, n_chunks=n_chunks, n_seq=n_seq, has_h0=h0 is not None, precise=precise),
        grid=(g,), in_specs=in_specs,
        out_specs=[grp(rows, lw), grp(n_seq, p), grp(n_seq, p)],
        out_shape=[jax.ShapeDtypeStruct((g, rows, lw), F32), jax.ShapeDtypeStruct((g, n_seq, p), F32),
                   jax.ShapeDtypeStruct((g, n_seq, p), F32)],
        compiler_params=_params("arbitrary"), name=f"ssm_chunk_{chunk}")(*ins)
    y = y.reshape(g, rows, chunk, j).transpose(1, 2, 0, 3).reshape(n_seq, t, SSM_WIDTH)
    return y, jnp.swapaxes(hr, 0, 1), jnp.swapaxes(hi, 0, 1)


GROUPS_PER_TILE = LANES // SSM_GROUP


def _block_transpose(xs):
    n = len(xs)
    blk = lax.broadcasted_iota(jnp.int32, xs[0].shape, 1) // SSM_GROUP
    xs = list(xs)
    d = n // 2
    while d:
        upper = (blk & d) != 0
        for i in range(n):
            if not i & d:
                lo, hi = xs[i], xs[i + d]
                xs[i] = jnp.where(upper, pltpu.roll(hi, SSM_GROUP * d, axis=1), lo)
                xs[i + d] = jnp.where(upper, hi, pltpu.roll(lo, LANES - SSM_GROUP * d, axis=1))
        d //= 2
    return xs


def _ssm_seq_kernel(u_ref, toep_ref, ctlr_ref, ctli_ref, obsr_ref, obsi_ref, alr_ref, ali_ref,
                    y_ref, h_ref, ug_ref, yg_ref, *, chunk):
    t = u_ref.shape[2]
    rows = t // chunk
    lw = chunk * SSM_GROUP
    n = GROUPS_PER_TILE
    p = STATE_DIM
    ridx = lax.broadcasted_iota(jnp.int32, (rows, 1), 0)
    re_half = lax.broadcasted_iota(jnp.int32, (1, 2 * p), 1) < p

    def shifted(x, d):
        return jnp.where(ridx >= d, pltpu.roll(x, d, axis=0), 0.0)

    def group_tile(v, carry):
        for w in range(lw // LANES):
            xs = [u_ref[0, v, pl.ds(n * w + k, rows, stride=chunk), :] for k in range(n)]
            for k, x in enumerate(_block_transpose(xs)):
                ug_ref[k, :, w * LANES:(w + 1) * LANES] = x.astype(BF16)

        def group(k, c):
            g = v * n + k
            ug = ug_ref[k]
            y = _dot_nt(ug, toep_ref[g])
            ctl = jnp.concatenate([ctlr_ref[g], ctli_ref[g]], axis=0)
            h = _dot_nt(ug, ctl)
            ar = jnp.concatenate([alr_ref[g], alr_ref[g]], axis=1)
            ai = jnp.concatenate([ali_ref[g], ali_ref[g]], axis=1)
            d = 1
            while d < rows:
                s = shifted(h, d)
                h = h + ar * s + jnp.where(re_half, -ai, ai) * pltpu.roll(s, p, axis=1)
                ar, ai = ar * ar - ai * ai, 2.0 * ar * ai
                d *= 2
            h_ref[0, pl.ds(g, 1), :] = h[rows - 1:rows, :]
            obs = jnp.concatenate([obsr_ref[g], obsi_ref[g]], axis=1)
            yg_ref[k] = y + _dot_nt(shifted(h, 1).astype(BF16), obs)
            return c

        lax.fori_loop(0, n, group, 0)
        for w in range(lw // LANES):
            ys = [yg_ref[k, :, w * LANES:(w + 1) * LANES] for k in range(n)]
            for k, y in enumerate(_block_transpose(ys)):
                y_ref[0, v, pl.ds(n * w + k, rows, stride=chunk), :] = y
        return carry

    lax.fori_loop(0, N_GROUPS // n, group_tile, 0)


def _ssm_seq(u, ops, *, chunk):
    n_seq, _, t, _ = u.shape
    g, p = N_GROUPS, STATE_DIM
    lw = chunk * SSM_GROUP
    assert lw % LANES == 0 and chunk == GROUPS_PER_TILE * (lw // LANES) and SSM_WIDTH == g * SSM_GROUP
    ops = [o.astype(BF16) for o in ops[:5]] + list(ops[5:])
    whole = lambda a: pl.BlockSpec(a.shape, lambda b: (0, 0, 0), pipeline_mode=pl.Buffered(1))
    seq = pl.BlockSpec((1, SSM_WIDTH // LANES, t, LANES), lambda b: (b, 0, 0, 0))
    y, h = pl.pallas_call(
        functools.partial(_ssm_seq_kernel, chunk=chunk), grid=(n_seq,),
        in_specs=[seq] + [whole(o) for o in ops],
        out_specs=[seq, pl.BlockSpec((1, g, 2 * p), lambda b: (b, 0, 0))],
        out_shape=[jax.ShapeDtypeStruct(u.shape, F32), jax.ShapeDtypeStruct((n_seq, g, 2 * p), F32)],
        scratch_shapes=[pltpu.VMEM((GROUPS_PER_TILE, t // chunk, lw), BF16),
                        pltpu.VMEM((GROUPS_PER_TILE, t // chunk, lw), F32)],
        compiler_params=_params("arbitrary"), name="ssm_seq")(u, *ops)
    return y, h[:, :, :p], h[:, :, p:]


FF_CHUNK = 1024


def _post_kernel(x_ref, att_ref, ys_ref, u_ref, p_ref, lng_ref, lnb_ref, dsk_ref, wglu_ref, bglu_ref,
                 woa_ref, wos_ref, l1g_ref, l1b_ref, wup_ref, wdn_ref, wpe_ref, wpg_ref, bpg_ref,
                 l2g_ref, l2b_ref, o_ref):
    h = _layer_norm(x_ref[0], lng_ref[...], lnb_ref[...])
    y = _gelu_tanh(_load_lane_tiles(ys_ref) + dsk_ref[...] * _load_lane_tiles(u_ref))
    y = y * _sigmoid(_dot(y.astype(BF16), wglu_ref[...]) + bglu_ref[...])
    mix = _dot(att_ref[0], woa_ref[...]) + _dot(y.astype(BF16), wos_ref[...])
    h1 = _layer_norm(ALPHA * h + mix, l1g_ref[...], l1b_ref[...])
    h1b = h1.astype(BF16)
    e = _sigmoid(_dot(h1b, wpg_ref[...]) + bpg_ref[...]) * _dot(p_ref[0].astype(BF16), wpe_ref[...])
    acc = ALPHA * h1 + e
    for c in range(D_FF // FF_CHUNK):
        cs = slice(c * FF_CHUNK, (c + 1) * FF_CHUNK)
        a = jnp.maximum(_dot(h1b, wup_ref[:, cs]), 0.0)
        acc = acc + _dot((a * a).astype(BF16), wdn_ref[cs, :])
    o_ref[0] = _layer_norm(acc, l2g_ref[...], l2b_ref[...])


def _post_mixer(x, att, ys, u, p, weights, *, tm):
    nb, t, _ = x.shape
    tok = lambda w: pl.BlockSpec((1, tm, w), lambda b, i: (b, i, 0))
    const = lambda a: pl.BlockSpec(a.shape, lambda b, i: (0, 0), pipeline_mode=pl.Buffered(1))
    return pl.pallas_call(
        _post_kernel, grid=(nb, t // tm),
        in_specs=[tok(D_MODEL), tok(ATTN_WIDTH), _lane_tiles_spec(tm, SSM_WIDTH), _lane_tiles_spec(tm, SSM_WIDTH),
                  tok(PLE_DIM)] + [const(w) for w in weights],
        out_specs=tok(D_MODEL), out_shape=jax.ShapeDtypeStruct((nb, t, D_MODEL), F32),
        compiler_params=_params("arbitrary", "arbitrary"), name="post_mixer")(x, att, ys, u, p, *weights)


def kernel(x_prompt, x_sample, cache_k, cache_v, cache_logf, state_re, state_im, page_table, p_prompt, p_sample,
           ln_in_g, ln_in_b, w_in, b_f, lam_re, lam_im, log_dt, b_re, b_im, c_re, c_im, d_skip, w_glu, b_glu,
           w_out, ln1_g, ln1_b, w_up, w_down, w_pe, w_pg, b_pg, ln2_g, ln2_b):
    assert w_in.shape[0] == 1, "one trunk layer"
    nb, t, _ = x_prompt.shape
    db, nq, _ = x_sample.shape
    row = lambda a: a.reshape(1, -1)
    post_w = [row(ln_in_g), row(ln_in_b), row(d_skip[0]), w_glu[0].astype(BF16), row(b_glu[0]),
              w_out[0, :ATTN_WIDTH].astype(BF16), w_out[0, ATTN_WIDTH:].astype(BF16), row(ln1_g[0]), row(ln1_b[0]),
              w_up[0].astype(BF16), w_down[0].astype(BF16), w_pe[0].astype(BF16), w_pg[0].astype(BF16),
              row(b_pg[0]), row(ln2_g[0]), row(ln2_b[0])]
    ssm_par = (lam_re[0], lam_im[0], log_dt[0], b_re[0], b_im[0], c_re[0], c_im[0])

    u, kT, vT, lfT, cT, qTb, kb, vTb, cp = _in_proj(x_prompt, ln_in_g, ln_in_b, w_in[0], b_f[0],
                                                     tm=512, q_scale=QK_SCALE * LOG2E, attn_layouts=True)
    att = _fox_prompt(qTb, kb, cp, vTb, cT, tq=256)
    ys, sr, si = _ssm_seq(u, _ssm_prep(*ssm_par, chunk=16), chunk=16)
    y_prompt = _post_mixer(x_prompt, att, ys, u, p_prompt[0], post_w, tm=512)
    heads_last = lambda a: jnp.transpose(a.reshape(nb, N_HEADS, HEAD_DIM, t), (0, 3, 1, 2))[None]
    prompt_out = (heads_last(kT), heads_last(vT), jnp.swapaxes(lfT, 1, 2)[None], sr[None], si[None])

    us, qs, ks, vs, logfs = _in_proj(x_sample.reshape(1, db * nq, D_MODEL), ln_in_g, ln_in_b, w_in[0], b_f[0],
                                     tm=db * nq, q_scale=QK_SCALE, attn_layouts=False)
    seq = lambda a: a.reshape(db, nq, a.shape[-1])
    to_tiles = lambda a: jnp.swapaxes(a.reshape(1, db * nq, SSM_WIDTH // LANES, LANES), 1, 2)
    from_tiles = lambda a: jnp.swapaxes(a, 1, 2).reshape(db, nq, SSM_WIDTH)
    att_s = _fox_sample(seq(qs), seq(ks), seq(vs), seq(logfs), cache_k[0], cache_v[0], cache_logf[0], page_table)
    ys_s, sr_s, si_s = _ssm_chunked(from_tiles(us), _ssm_prep(*ssm_par, chunk=nq), (state_re[0], state_im[0]),
                                    chunk=nq, precise=True)
    flat = lambda a: a.reshape(1, db * nq, a.shape[-1])
    y_sample = _post_mixer(flat(x_sample), flat(att_s).astype(BF16), to_tiles(ys_s), us, flat(p_sample[0]),
                           post_w, tm=db * nq).reshape(db, nq, D_MODEL)
    sample_out = (ks.reshape(1, db, nq, N_HEADS, HEAD_DIM), vs.reshape(1, db, nq, N_HEADS, HEAD_DIM),
                  logfs.reshape(1, db, nq, N_HEADS), sr_s[None], si_s[None])
    return (y_prompt, y_sample) + prompt_out + sample_out
```

```python
import functools
import math

import jax
import jax.numpy as jnp
from jax import lax
from jax.experimental import pallas as pl
from jax.experimental.pallas import tpu as pltpu

F32 = jnp.float32
BF16 = jnp.bfloat16

D_MODEL = 1024
ATTN_WIDTH = 512
SSM_WIDTH = 512
HEAD_DIM = 64
N_HEADS = 8
SSM_GROUP = 16
N_GROUPS = 32
STATE_DIM = 64
D_FF = 4096
PLE_DIM = 256
PAGE_SIZE = 128
ALPHA = 2.0 ** 0.25
LN_EPS = 1e-5
NEG_INF = -1e30
QK_SCALE = HEAD_DIM ** -0.5
LOG2E = math.log2(math.e)

VMEM_LIMIT_BYTES = 56 * 1024 * 1024

_NT = (((1,), (1,)), ((), ()))


def _params(*sem):
    return pltpu.CompilerParams(dimension_semantics=sem, vmem_limit_bytes=VMEM_LIMIT_BYTES)


def _const_spec(shape):
    return pl.BlockSpec(shape, lambda *_: (0,) * len(shape))


def _layer_norm(x, g, b):
    mu = jnp.mean(x, axis=-1, keepdims=True)
    xc = x - mu
    var = jnp.mean(xc * xc, axis=-1, keepdims=True)
    return xc * lax.rsqrt(var + LN_EPS) * g + b


def _log_sigmoid(x):
    return jnp.minimum(x, 0.0) - jnp.log1p(jnp.exp(-jnp.abs(x)))


def _sigmoid(x):
    return 1.0 / (1.0 + jnp.exp(-x))


def _gelu_tanh(x):
    return 0.5 * x * (1.0 + jnp.tanh(math.sqrt(2.0 / math.pi) * (x + 0.044715 * (x * x * x))))


def _dot(a, b):
    return jnp.dot(a, b, preferred_element_type=F32)


def _dot_nt(a, b):
    return lax.dot_general(a, b, _NT, preferred_element_type=F32)


def _split3(x):
    hi = x.astype(BF16)
    r1 = x - hi.astype(F32)
    mid = r1.astype(BF16)
    lo = (r1 - mid.astype(F32)).astype(BF16)
    return hi, mid, lo


LANES = 128


def _lane_tiles_spec(tm, width):
    return pl.BlockSpec((1, width // LANES, tm, LANES), lambda b, i: (b, 0, i, 0))


def _store_lane_tiles(ref, x):
    for v in range(x.shape[-1] // LANES):
        ref[0, v] = x[:, v * LANES:(v + 1) * LANES]


def _load_lane_tiles(ref):
    return jnp.concatenate([ref[0, v] for v in range(ref.shape[1])], axis=-1)


def _dot_exact01(x, m01):
    hi, mid, lo = _split3(x)
    return _dot(hi, m01) + _dot(mid, m01) + _dot(lo, m01)


N_BIAS_ROWS = 128
ONES_ROW = 3 * N_HEADS
SUM_ROWS = 16


def _in_proj_kernel(*refs, tm, q_scale, attn_layouts):
    x_ref, g_ref, b_ref, wu_ref = refs[:4]
    if attn_layouts:
        (wqT_ref, wkT_ref, wvT_ref, wfT_ref, bfc_ref, tri_ref,
         u_ref, kT_ref, vT_ref, lfT_ref, cT_ref, qTb_ref, kb_ref, vTb_ref, cp_ref, carry_ref) = refs[4:]
    else:
        wq_ref, wk_ref, wv_ref, wf_ref, bf_ref, u_ref, q_ref, k_ref, v_ref, lf_ref = refs[4:]
    hb = _layer_norm(x_ref[0], g_ref[...], b_ref[...]).astype(BF16)
    _store_lane_tiles(u_ref, _dot(hb, wu_ref[...]))
    if attn_layouts:
        qTb_ref[0] = (_dot_nt(wqT_ref[...], hb) * q_scale).astype(BF16)
        kT = _dot_nt(wkT_ref[...], hb)
        kT_ref[0] = kT
        kb_ref[0] = kT.T.astype(BF16)
        vT = _dot_nt(wvT_ref[...], hb)
        vT_ref[0] = vT
        vTb_ref[0] = vT.astype(BF16)
        lfT = _log_sigmoid(_dot_nt(wfT_ref[...], hb) + bfc_ref[...])
        lfT_ref[0] = lfT

        @pl.when(pl.program_id(1) == 0)
        def _():
            carry_ref[...] = jnp.zeros_like(carry_ref)

        c = carry_ref[...] + _dot_exact01(lfT, tri_ref[...])
        cT_ref[0] = c
        carry_ref[...] = c[:, tm - 1:tm]
        hi, mid, lo = _split3(c * LOG2E)
        row = lax.broadcasted_iota(jnp.int32, (N_BIAS_ROWS - ONES_ROW, tm), 0)
        ones = jnp.where(row < 3, 1.0, 0.0)
        pieces = jnp.concatenate([hi.astype(F32), mid.astype(F32), lo.astype(F32), ones], axis=0)
        cp_ref[0] = pieces.T.astype(BF16)
    else:
        q_ref[0] = _dot(hb, wq_ref[...]) * q_scale
        k_ref[0] = _dot(hb, wk_ref[...])
        v_ref[0] = _dot(hb, wv_ref[...])
        lf_ref[0] = _log_sigmoid(_dot(hb, wf_ref[...]) + bf_ref[...])


def _in_proj(x3, ln_g, ln_b, w_in, b_f, *, tm, q_scale, attn_layouts):
    nb, t, _ = x3.shape
    wq = w_in[:, :ATTN_WIDTH].astype(BF16)
    wk = w_in[:, ATTN_WIDTH:2 * ATTN_WIDTH].astype(BF16)
    wv = w_in[:, 2 * ATTN_WIDTH:3 * ATTN_WIDTH].astype(BF16)
    wf = w_in[:, 3 * ATTN_WIDTH:3 * ATTN_WIDTH + N_HEADS].astype(BF16)
    wu = w_in[:, 3 * ATTN_WIDTH + N_HEADS:].astype(BF16)
    tok = lambda w: pl.BlockSpec((1, tm, w), lambda b, i: (b, i, 0))
    feat = lambda w: pl.BlockSpec((1, w, tm), lambda b, i: (b, 0, i))
    sds = jax.ShapeDtypeStruct
    ins = [x3, ln_g.reshape(1, -1), ln_b.reshape(1, -1), wu]
    in_specs = [tok(D_MODEL), _const_spec((1, D_MODEL)), _const_spec((1, D_MODEL)), _const_spec((D_MODEL, SSM_WIDTH))]
    out_shape = [sds((nb, SSM_WIDTH // LANES, t, LANES), F32)]
    out_specs = [_lane_tiles_spec(tm, SSM_WIDTH)]
    scratch = []
    if attn_layouts:
        tri = (jnp.arange(tm)[:, None] <= jnp.arange(tm)[None, :]).astype(BF16)
        ins += [wq.T, wk.T, wv.T, wf.T, b_f.reshape(N_HEADS, 1), tri]
        in_specs += [_const_spec((ATTN_WIDTH, D_MODEL))] * 3
        in_specs += [_const_spec((N_HEADS, D_MODEL)), _const_spec((N_HEADS, 1)), _const_spec((tm, tm))]
        out_shape += [sds((nb, ATTN_WIDTH, t), F32), sds((nb, ATTN_WIDTH, t), F32),
                      sds((nb, N_HEADS, t), F32), sds((nb, N_HEADS, t), F32),
                      sds((nb, ATTN_WIDTH, t), BF16), sds((nb, t, ATTN_WIDTH), BF16), sds((nb, ATTN_WIDTH, t), BF16),
                      sds((nb, t, N_BIAS_ROWS), BF16)]
        out_specs += [feat(ATTN_WIDTH), feat(ATTN_WIDTH), feat(N_HEADS), feat(N_HEADS),
                      feat(ATTN_WIDTH), tok(ATTN_WIDTH), feat(ATTN_WIDTH), tok(N_BIAS_ROWS)]
        scratch = [pltpu.VMEM((N_HEADS, 1), F32)]
    else:
        ins += [wq, wk, wv, wf, b_f.reshape(1, N_HEADS)]
        in_specs += [_const_spec((D_MODEL, ATTN_WIDTH))] * 3 + [_const_spec((D_MODEL, N_HEADS)), _const_spec((1, N_HEADS))]
        out_shape += [sds((nb, t, ATTN_WIDTH), F32)] * 3 + [sds((nb, t, N_HEADS), F32)]
        out_specs += [tok(ATTN_WIDTH)] * 3 + [tok(N_HEADS)]
    return pl.pallas_call(
        functools.partial(_in_proj_kernel, tm=tm, q_scale=q_scale, attn_layouts=attn_layouts),
        grid=(nb, t // tm), in_specs=in_specs, out_specs=out_specs, out_shape=out_shape,
        scratch_shapes=scratch, compiler_params=_params("arbitrary", "arbitrary"),
        name="in_proj_prompt" if attn_layouts else "in_proj_sample")(*ins)


def _fox_prompt_kernel(qT_ref, k_ref, cp_ref, vT_ref, cT_ref, o_ref, bq_ref, m_ref, acc_ref, s0_ref, s1_ref, *, tq):
    qi = pl.program_id(1)
    q0 = pl.multiple_of(qi * tq, tq)
    heads = [slice(h * HEAD_DIM, (h + 1) * HEAD_DIM) for h in range(N_HEADS)]
    pairs = [slice(2 * HEAD_DIM * (h // 2), 2 * HEAD_DIM * (h // 2 + 1)) for h in range(N_HEADS)]

    c0 = cT_ref[0, :, pl.ds(q0, tq)][:, 0:1] * LOG2E
    c0_pieces = [p.astype(F32) for p in _split3(c0)]
    r = lax.broadcasted_iota(jnp.int32, (2 * HEAD_DIM, tq), 0)
    for h in range(N_HEADS):
        mine = (r < HEAD_DIM) if h % 2 == 0 else (r >= HEAD_DIM)
        bq_ref[h, 0:2 * HEAD_DIM, :] = jnp.where(mine, qT_ref[0, pairs[h], :], jnp.zeros((), BF16))
        sel = jnp.where((r == h) | (r == N_HEADS + h) | (r == 2 * N_HEADS + h), -1.0, 0.0)
        for j, piece in enumerate(c0_pieces):
            sel = jnp.where(r == ONES_ROW + j, piece[h:h + 1, :], sel)
        bq_ref[h, 2 * HEAD_DIM:, :] = sel.astype(BF16)

    key_row = lax.broadcasted_iota(jnp.int32, (tq, tq), 0)
    qry_col = lax.broadcasted_iota(jnp.int32, (tq, tq), 1)

    m_ref[...] = jnp.full_like(m_ref, NEG_INF)
    acc_ref[...] = jnp.zeros_like(acc_ref)
    ones = jnp.ones((SUM_ROWS, tq), BF16)

    s_refs = (s0_ref, s1_ref)

    def scores(kt, h, buf):
        ks = pl.multiple_of(kt * tq, tq)
        a = jnp.concatenate([k_ref[0, pl.ds(ks, tq), pairs[h]], cp_ref[0, pl.ds(ks, tq), :]], axis=1)
        s_refs[buf][h] = _dot(a, bq_ref[h])

    def accumulate(kt, h, buf, diagonal):
        ks = pl.multiple_of(kt * tq, tq)
        s = s_refs[buf][h]
        if diagonal:
            s = jnp.where(key_row <= qry_col, s, NEG_INF)
        m_new = jnp.maximum(m_ref[h], jnp.max(s, axis=0, keepdims=True))
        alpha = jnp.exp2(m_ref[h] - m_new)
        m_ref[h] = m_new
        p = jnp.exp2((s - m_new).astype(BF16))
        v1 = jnp.concatenate([vT_ref[0, heads[h], pl.ds(ks, tq)], ones], axis=0)
        acc_ref[h] = alpha * acc_ref[h] + _dot(v1, p)

    def stage(kt, cur, *, diagonal=False, issue_next=True):
        for h in range(N_HEADS + 1):
            if issue_next and h < N_HEADS:
                scores(kt + 1, h, 1 - cur)
            if h >= 1:
                accumulate(kt, h - 1, cur, diagonal)

    for h in range(N_HEADS):
        scores(0, h, 0)

    def body(j, carry):
        stage(2 * j, 0)
        stage(2 * j + 1, 1)
        return carry

    lax.fori_loop(0, qi // 2, body, 0)

    @pl.when(qi % 2 == 0)
    def _():
        stage(qi, 0, diagonal=True, issue_next=False)

    @pl.when(qi % 2 == 1)
    def _():
        stage(qi - 1, 0)
        stage(qi, 1, diagonal=True, issue_next=False)

    outs = [acc_ref[h, 0:HEAD_DIM, :] / acc_ref[h, HEAD_DIM:HEAD_DIM + 1, :] for h in range(N_HEADS)]
    o_ref[0] = jnp.concatenate(outs, axis=0).T.astype(o_ref.dtype)


def _fox_prompt(qT, k, cp, vT, cT, *, tq):
    nb, t, _ = k.shape
    assert N_BIAS_ROWS == 2 * HEAD_DIM
    whole_f = lambda w: pl.BlockSpec((1, w, t), lambda b, i: (b, 0, 0))
    whole_t = lambda w: pl.BlockSpec((1, t, w), lambda b, i: (b, 0, 0))
    return pl.pallas_call(
        functools.partial(_fox_prompt_kernel, tq=tq),
        grid=(nb, t // tq),
        in_specs=[pl.BlockSpec((1, ATTN_WIDTH, tq), lambda b, i: (b, 0, i)),
                  whole_t(ATTN_WIDTH), whole_t(N_BIAS_ROWS), whole_f(ATTN_WIDTH), whole_f(N_HEADS)],
        out_specs=pl.BlockSpec((1, tq, ATTN_WIDTH), lambda b, i: (b, i, 0)),
        out_shape=jax.ShapeDtypeStruct((nb, t, ATTN_WIDTH), BF16),
        scratch_shapes=[pltpu.VMEM((N_HEADS, 2 * HEAD_DIM + N_BIAS_ROWS, tq), BF16),
                        pltpu.VMEM((N_HEADS, 1, tq), F32), pltpu.VMEM((N_HEADS, HEAD_DIM + SUM_ROWS, tq), F32),
                        pltpu.VMEM((N_HEADS, tq, tq), F32), pltpu.VMEM((N_HEADS, tq, tq), F32)],
        compiler_params=_params("arbitrary", "arbitrary"), name="fox_prompt")(qT, k, cp, vT, cT)


def _fox_sample_kernel(pt_ref, q_ref, kn_ref, vn_ref, lfn_ref, sl_ref, *refs, n_pages, nq):
    del pt_ref
    k_refs, v_refs, pf_refs = refs[:n_pages], refs[n_pages:2 * n_pages], refs[2 * n_pages:3 * n_pages]
    o_ref = refs[3 * n_pages]
    nr = nq * N_HEADS
    head_of_lane = lax.broadcasted_iota(jnp.int32, (N_HEADS, ATTN_WIDTH), 1) // HEAD_DIM
    own = head_of_lane == lax.broadcasted_iota(jnp.int32, (N_HEADS, ATTN_WIDTH), 0)
    q = q_ref[0]
    qbd32 = jnp.concatenate([jnp.where(own, q[i:i + 1, :], 0.0) for i in range(nq)], axis=0)
    qbd = qbd32.astype(BF16)
    rep = lambda e: jnp.concatenate([e] * nq, axis=0)
    rows = lambda j: slice(j * N_HEADS, (j + 1) * N_HEADS)

    lfn = lfn_ref[0]
    e_new = rep(_dot_exact01(lfn, sl_ref[...]))
    kn, vn = kn_ref[0], vn_ref[0]
    qidx = lax.broadcasted_iota(jnp.int32, (nr, 1), 0) // N_HEADS
    s_new = [jnp.where(qidx >= j, jnp.sum(qbd32 * kn[j:j + 1, :], axis=-1, keepdims=True) + e_new[:, j:j + 1], NEG_INF)
             for j in range(nq)]

    pf_all = jnp.concatenate([r[0, 0] for r in pf_refs], axis=0)
    sfx_all = _dot_exact01(pf_all, sl_ref[...])
    tot_all = sfx_all[:, 0:1] + pf_all[:, 0:1]
    tail = jnp.sum(lfn, axis=-1, keepdims=True)
    scores = [None] * n_pages
    for j in reversed(range(n_pages)):
        kt = k_refs[j][0, 0].reshape(ATTN_WIDTH, PAGE_SIZE).astype(BF16)
        scores[j] = _dot(qbd, kt) + rep(sfx_all[rows(j)] + tail)
        tail = tail + tot_all[rows(j)]
    s_all = jnp.concatenate(scores, axis=1)
    m = jnp.max(s_all, axis=-1, keepdims=True)
    for s in s_new:
        m = jnp.maximum(m, s)
    p_all = jnp.exp(s_all - m)
    l = jnp.sum(p_all, axis=-1, keepdims=True)
    acc = jnp.zeros((nr, ATTN_WIDTH), F32)
    for j, s in enumerate(s_new):
        p = jnp.exp(s - m)
        l = l + p
        acc = acc + p * vn[j:j + 1, :]
    p_all = p_all.astype(BF16)
    for j in range(n_pages):
        vt = v_refs[j][0, 0].reshape(ATTN_WIDTH, PAGE_SIZE).astype(BF16)
        acc = acc + _dot_nt(p_all[:, j * PAGE_SIZE:(j + 1) * PAGE_SIZE], vt)
    acc = acc / l
    for i in range(nq):
        o_ref[0, i:i + 1, :] = jnp.sum(jnp.where(own, acc[i * N_HEADS:(i + 1) * N_HEADS], 0.0), axis=0, keepdims=True)


def _fox_sample(q, k_new, v_new, logf_new, cache_k, cache_v, cache_logf, page_table):
    db, nq, _ = q.shape
    n_pages = page_table.shape[1]
    kc = jnp.transpose(cache_k, (0, 2, 3, 1))
    vc = jnp.transpose(cache_v, (0, 2, 3, 1))
    pf = jnp.transpose(cache_logf, (0, 2, 1))
    lfn_t = jnp.swapaxes(jnp.pad(logf_new, ((0, 0), (0, PAGE_SIZE - nq), (0, 0))), 1, 2)
    strict_lower = (jnp.arange(PAGE_SIZE)[:, None] > jnp.arange(PAGE_SIZE)[None, :]).astype(BF16)
    seq = lambda r, w: pl.BlockSpec((1, r, w), lambda b, pt: (b, 0, 0))
    page = lambda j: pl.BlockSpec((1, 1, N_HEADS, HEAD_DIM, PAGE_SIZE), lambda b, pt: (0, pt[b, j], 0, 0, 0))
    page_f = lambda j: pl.BlockSpec((1, 1, N_HEADS, PAGE_SIZE), lambda b, pt: (0, pt[b, j], 0, 0))
    in_specs = [seq(nq, ATTN_WIDTH), seq(nq, ATTN_WIDTH), seq(nq, ATTN_WIDTH), seq(N_HEADS, PAGE_SIZE),
                pl.BlockSpec((PAGE_SIZE, PAGE_SIZE), lambda b, pt: (0, 0))]
    in_specs += [page(j) for j in range(n_pages)] * 2 + [page_f(j) for j in range(n_pages)]
    return pl.pallas_call(
        functools.partial(_fox_sample_kernel, n_pages=n_pages, nq=nq),
        grid_spec=pltpu.PrefetchScalarGridSpec(
            num_scalar_prefetch=1, grid=(db,), in_specs=in_specs, out_specs=seq(nq, ATTN_WIDTH)),
        out_shape=jax.ShapeDtypeStruct((db, nq, ATTN_WIDTH), F32),
        compiler_params=_params("arbitrary"), name="fox_sample")(
            page_table, q, k_new, v_new, lfn_t, strict_lower,
            *([kc[None]] * n_pages), *([vc[None]] * n_pages), *([pf[None]] * n_pages))


def _cmul(ar, ai, br, bi):
    return ar * br - ai * bi, ar * bi + ai * br


def _cpow_by_bits(ar, ai, n, nbits):
    pr = jnp.ones(jnp.broadcast_shapes(ar.shape, n.shape), F32)
    pi = jnp.zeros_like(pr)
    for b in range(nbits):
        bit = ((n >> b) & 1) == 1
        fr = jnp.where(bit, ar, 1.0)
        fi = jnp.where(bit, ai, 0.0)
        pr, pi = _cmul(pr, pi, fr, fi)
        ar, ai = _cmul(ar, ai, ar, ai)
    return pr, pi


def _ssm_prep_kernel(ldt_ref, lr_ref, li_ref, bre_ref, bim_ref, cre_ref, cim_ref,
                     toep_ref, ctlr_ref, ctli_ref, obsr_ref, obsi_ref, alr_ref, ali_ref, asr_ref, asi_ref,
                     *, chunk, small):
    lw = chunk * SSM_GROUP
    p = STATE_DIM
    nbits = chunk.bit_length()
    dt = jnp.exp(ldt_ref[0])
    lr_row, li_row = lr_ref[0], li_ref[0]

    def discretise(lr, li):
        mag = jnp.exp(lr * dt)
        return mag * jnp.cos(li * dt), mag * jnp.sin(li * dt)

    def column(row):
        eye = lax.broadcasted_iota(jnp.int32, (p, p), 0) == lax.broadcasted_iota(jnp.int32, (p, p), 1)
        return jnp.sum(jnp.where(eye, row, 0.0), axis=-1, keepdims=True)

    lr, li = column(lr_row), column(li_row)
    ar, ai = discretise(lr, li)
    den = lr * lr + li * li
    nr, ni = ar - 1.0, ai
    cr, ci = (nr * lr + ni * li) / den, (ni * lr - nr * li) / den
    btr = jnp.concatenate([bre_ref[0]] * (2 * lw // LANES), axis=1)
    bti = jnp.concatenate([bim_ref[0]] * (2 * lw // LANES), axis=1)
    bbr, bbi = _cmul(cr, ci, btr, bti)
    sig = lax.broadcasted_iota(jnp.int32, (1, 2 * lw), 1) // SSM_GROUP
    er, ei = _cpow_by_bits(ar, ai, jnp.maximum(chunk - 1 - sig, 0), nbits)
    hr, hi = _cmul(er, ei, bbr, bbi)
    hr = jnp.where(sig < chunk, hr, 0.0)
    hi = jnp.where(sig < chunk, hi, 0.0)
    ctlr_ref[0] = hr[:, :lw]
    ctli_ref[0] = hi[:, :lw]
    c_re, c_im = cre_ref[0], cim_ref[0]
    hp = lax.Precision.HIGHEST
    for t in range(chunk):
        off = (chunk - 1 - t) * SSM_GROUP
        toep_ref[0, t * SSM_GROUP:(t + 1) * SSM_GROUP, :] = (
            jnp.dot(c_re, hr[:, off:off + lw], precision=hp, preferred_element_type=F32)
            - jnp.dot(c_im, hi[:, off:off + lw], precision=hp, preferred_element_type=F32))

    arr, air = discretise(lr_row, li_row)
    tp1 = lax.broadcasted_iota(jnp.int32, (lw, 1), 0) // SSM_GROUP + 1
    pr, pi = _cpow_by_bits(arr, air, tp1, nbits)
    c_r = jnp.concatenate([c_re] * chunk, axis=0)
    c_i = jnp.concatenate([c_im] * chunk, axis=0)
    obsr_ref[0] = c_r * pr - c_i * pi
    obsi_ref[0] = -(c_r * pi + c_i * pr)
    n = 1
    while n < chunk:
        arr, air = _cmul(arr, air, arr, air)
        n *= 2
        if n == small:
            asr_ref[0] = arr
            asi_ref[0] = air
    alr_ref[0] = arr
    ali_ref[0] = air


def _ssm_prep(lam_re, lam_im, log_dt, b_re, b_im, c_re, c_im, *, chunk, small):
    assert chunk & (chunk - 1) == 0 and small & (small - 1) == 0 and 1 < small < chunk
    g, p, j = N_GROUPS, STATE_DIM, SSM_GROUP
    lw, sw = chunk * j, small * j
    lane_tile = lambda b: jnp.tile(b, (1, 1, LANES // j))
    ins = [log_dt.reshape(g, 1, 1), lam_re.reshape(g, 1, p), lam_im.reshape(g, 1, p), lane_tile(b_re), lane_tile(b_im),
           c_re, c_im]
    grp = lambda a, b: pl.BlockSpec((1, a, b), lambda i: (i, 0, 0))
    in_specs = [grp(1, 1), grp(1, p), grp(1, p), grp(p, LANES), grp(p, LANES), grp(j, p), grp(j, p)]
    shapes = [(lw, lw), (p, lw), (p, lw), (lw, p), (lw, p), (1, p), (1, p), (1, p), (1, p)]
    toep, ctlr, ctli, obsr, obsi, alr, ali, asr, asi = pl.pallas_call(
        functools.partial(_ssm_prep_kernel, chunk=chunk, small=small), grid=(g,), in_specs=in_specs,
        out_specs=[grp(*s) for s in shapes],
        out_shape=[jax.ShapeDtypeStruct((g,) + s, F32) for s in shapes],
        compiler_params=_params("arbitrary"), name="ssm_prep")(*ins)
    ops_small = (toep[:, :sw, :sw], ctlr[:, :, lw - sw:], ctli[:, :, lw - sw:], obsr[:, :sw], obsi[:, :sw], asr, asi)
    return (toep, ctlr, ctli, obsr, obsi, alr, ali), ops_small


def _ssm_chunk_kernel(*refs, n_chunks, n_seq, has_h0, precise):
    if has_h0:
        (u_ref, toep_ref, ctlr_ref, ctli_ref, obsr_ref, obsi_ref, alr_ref, ali_ref, h0r_ref, h0i_ref,
         y_ref, hr_ref, hi_ref) = refs
    else:
        (u_ref, toep_ref, ctlr_ref, ctli_ref, obsr_ref, obsi_ref, alr_ref, ali_ref,
         y_ref, hr_ref, hi_ref) = refs
    if precise:
        cast = lambda x: x
        mm = lambda a, b: lax.dot_general(a, b, _NT, precision=lax.Precision.HIGHEST, preferred_element_type=F32)
    else:
        cast = lambda x: x.astype(BF16)
        mm = _dot_nt
    u = cast(u_ref[0])
    y = mm(u, cast(toep_ref[0]))
    hr = mm(u, cast(ctlr_ref[0]))
    hi = mm(u, cast(ctli_ref[0]))
    ar, ai = alr_ref[0], ali_ref[0]
    if n_chunks == 1:
        pr, pi = h0r_ref[0], h0i_ref[0]
        er, ei = _cmul(ar, ai, pr, pi)
        hr, hi = hr + er, hi + ei
        hr_ref[0] = hr
        hi_ref[0] = hi
    else:
        assert not has_h0
        rows = hr.shape[0]
        cidx = lax.broadcasted_iota(jnp.int32, (rows, 1), 0) % n_chunks

        def shifted(x, d):
            return jnp.where(cidx >= d, pltpu.roll(x, d, axis=0), 0.0)

        d = 1
        while d < n_chunks:
            sr, si = shifted(hr, d), shifted(hi, d)
            er, ei = _cmul(ar, ai, sr, si)
            hr, hi = hr + er, hi + ei
            ar, ai = _cmul(ar, ai, ar, ai)
            d *= 2
        for b in range(n_seq):
            last = (b + 1) * n_chunks - 1
            hr_ref[0, b:b + 1, :] = hr[last:last + 1, :]
            hi_ref[0, b:b + 1, :] = hi[last:last + 1, :]
        pr, pi = shifted(hr, 1), shifted(hi, 1)
    y = y + mm(cast(pr), cast(obsr_ref[0])) + mm(cast(pi), cast(obsi_ref[0]))
    y_ref[0] = y


def _ssm_chunked(u, ops, h0, *, chunk, precise):
    n_seq, t, _ = u.shape
    g, p, j = N_GROUPS, STATE_DIM, SSM_GROUP
    n_chunks = t // chunk
    rows, lw = n_seq * n_chunks, chunk * j
    ug = u.reshape(rows, chunk, g, j).transpose(2, 0, 1, 3).reshape(g, rows, lw)
    if not precise:
        ug = ug.astype(BF16)
    grp = lambda a, b: pl.BlockSpec((1, a, b), lambda i: (i, 0, 0))
    ins = [ug, *ops]
    in_specs = [grp(rows, lw), grp(lw, lw), grp(p, lw), grp(p, lw), grp(lw, p), grp(lw, p), grp(1, p), grp(1, p)]
    if h0 is not None:
        ins += [jnp.swapaxes(h0[0], 0, 1), jnp.swapaxes(h0[1], 0, 1)]
        in_specs += [grp(n_seq, p), grp(n_seq, p)]
    y, hr, hi = pl.pallas_call(
        functools.partial(_ssm_chunk_kernel, n_chunks=n_chunks, n_seq=n_seq, has_h0=h0 is not None, precise=precise),
        grid=(g,), in_specs=in_specs,
        out_specs=[grp(rows, lw), grp(n_seq, p), grp(n_seq, p)],
        out_shape=[jax.ShapeDtypeStruct((g, rows, lw), F32), jax.ShapeDtypeStruct((g, n_seq, p), F32),
                   jax.ShapeDtypeStruct((g, n_seq, p), F32)],
        compiler_params=_params("arbitrary"), name=f"ssm_chunk_{chunk}")(*ins)
    y = y.reshape(g, rows, chunk, j).transpose(1, 2, 0, 3).reshape(n_seq, t, SSM_WIDTH)
    return y, jnp.swapaxes(hr, 0, 1), jnp.swapaxes(hi, 0, 1)


GROUPS_PER_TILE = LANES // SSM_GROUP
GROUPS_PER_TRIP = 4


def _block_transpose(xs):
    n = len(xs)
    blk = lax.broadcasted_iota(jnp.int32, xs[0].shape, 1) // SSM_GROUP
    xs = list(xs)
    d = n // 2
    while d:
        upper = (blk & d) != 0
        for i in range(n):
            if not i & d:
                lo, hi = xs[i], xs[i + d]
                xs[i] = jnp.where(upper, pltpu.roll(hi, SSM_GROUP * d, axis=1), lo)
                xs[i + d] = jnp.where(upper, hi, pltpu.roll(lo, LANES - SSM_GROUP * d, axis=1))
        d //= 2
    return xs


def _ssm_seq_kernel(u_ref, toep_ref, ctlr_ref, ctli_ref, obsr_ref, obsi_ref, alr_ref, ali_ref,
                    y_ref, h_ref, ug_ref, yg_ref, *, chunk):
    t = u_ref.shape[2]
    rows = t // chunk
    lw = chunk * SSM_GROUP
    n = GROUPS_PER_TILE
    p = STATE_DIM
    ridx = lax.broadcasted_iota(jnp.int32, (rows, 1), 0)
    re_half = lax.broadcasted_iota(jnp.int32, (1, 2 * p), 1) < p

    def shifted(x, d):
        return jnp.where(ridx >= d, pltpu.roll(x, d, axis=0), 0.0)

    def group_tile(v, carry):
        for w in range(lw // LANES):
            xs = [u_ref[0, v, pl.ds(n * w + k, rows, stride=chunk), :] for k in range(n)]
            for k, x in enumerate(_block_transpose(xs)):
                ug_ref[k, :, w * LANES:(w + 1) * LANES] = x.astype(BF16)

        def groups(i, c):
            ks = [i * GROUPS_PER_TRIP + e for e in range(GROUPS_PER_TRIP)]
            gs = [v * n + k for k in ks]
            ugs = [ug_ref[k] for k in ks]
            ys = [_dot_nt(ug, toep_ref[g]) for ug, g in zip(ugs, gs)]
            hs = [_dot_nt(ug, jnp.concatenate([ctlr_ref[g], ctli_ref[g]], axis=0))
                  for ug, g in zip(ugs, gs)]
            ars = [jnp.concatenate([alr_ref[g], alr_ref[g]], axis=1) for g in gs]
            ais = [jnp.concatenate([ali_ref[g], ali_ref[g]], axis=1) for g in gs]
            d = 1
            while d < rows:
                for e in range(GROUPS_PER_TRIP):
                    sh = shifted(hs[e], d)
                    hs[e] = hs[e] + ars[e] * sh + jnp.where(re_half, -ais[e], ais[e]) * pltpu.roll(sh, p, axis=1)
                    ars[e], ais[e] = ars[e] * ars[e] - ais[e] * ais[e], 2.0 * ars[e] * ais[e]
                d *= 2
            for e, (k, g) in enumerate(zip(ks, gs)):
                h_ref[0, pl.ds(g, 1), :] = hs[e][rows - 1:rows, :]
                obs = jnp.concatenate([obsr_ref[g], obsi_ref[g]], axis=1)
                yg_ref[k] = ys[e] + _dot_nt(shifted(hs[e], 1).astype(BF16), obs)
            return c

        lax.fori_loop(0, n // GROUPS_PER_TRIP, groups, 0)
        for w in range(lw // LANES):
            ys = [yg_ref[k, :, w * LANES:(w + 1) * LANES] for k in range(n)]
            for k, y in enumerate(_block_transpose(ys)):
                y_ref[0, v, pl.ds(n * w + k, rows, stride=chunk), :] = y
        return carry

    lax.fori_loop(0, N_GROUPS // n, group_tile, 0)


def _ssm_seq(u, ops, *, chunk):
    n_seq, _, t, _ = u.shape
    g, p = N_GROUPS, STATE_DIM
    lw = chunk * SSM_GROUP
    assert lw % LANES == 0 and chunk == GROUPS_PER_TILE * (lw // LANES) and SSM_WIDTH == g * SSM_GROUP
    ops = [o.astype(BF16) for o in ops[:5]] + list(ops[5:])
    whole = lambda a: pl.BlockSpec(a.shape, lambda b: (0, 0, 0), pipeline_mode=pl.Buffered(1))
    seq = pl.BlockSpec((1, SSM_WIDTH // LANES, t, LANES), lambda b: (b, 0, 0, 0))
    y, h = pl.pallas_call(
        functools.partial(_ssm_seq_kernel, chunk=chunk), grid=(n_seq,),
        in_specs=[seq] + [whole(o) for o in ops],
        out_specs=[seq, pl.BlockSpec((1, g, 2 * p), lambda b: (b, 0, 0))],
        out_shape=[jax.ShapeDtypeStruct(u.shape, F32), jax.ShapeDtypeStruct((n_seq, g, 2 * p), F32)],
        scratch_shapes=[pltpu.VMEM((GROUPS_PER_TILE, t // chunk, lw), BF16),
                        pltpu.VMEM((GROUPS_PER_TILE, t // chunk, lw), F32)],
        compiler_params=_params("arbitrary"), name="ssm_seq")(u, *ops)
    return y, h[:, :, :p], h[:, :, p:]


FF_CHUNK = 1024


def _post_kernel(x_ref, att_ref, ys_ref, u_ref, p_ref, lng_ref, lnb_ref, dsk_ref, wglu_ref, bglu_ref,
                 woa_ref, wos_ref, l1g_ref, l1b_ref, wup_ref, wdn_ref, wpe_ref, wpg_ref, bpg_ref,
                 l2g_ref, l2b_ref, o_ref):
    h = _layer_norm(x_ref[0], lng_ref[...], lnb_ref[...])
    y = _gelu_tanh(_load_lane_tiles(ys_ref) + dsk_ref[...] * _load_lane_tiles(u_ref))
    y = y * _sigmoid(_dot(y.astype(BF16), wglu_ref[...]) + bglu_ref[...])
    mix = _dot(att_ref[0], woa_ref[...]) + _dot(y.astype(BF16), wos_ref[...])
    h1 = _layer_norm(ALPHA * h + mix, l1g_ref[...], l1b_ref[...])
    h1b = h1.astype(BF16)
    e = _sigmoid(_dot(h1b, wpg_ref[...]) + bpg_ref[...]) * _dot(p_ref[0].astype(BF16), wpe_ref[...])
    acc = ALPHA * h1 + e
    for c in range(D_FF // FF_CHUNK):
        cs = slice(c * FF_CHUNK, (c + 1) * FF_CHUNK)
        a = jnp.maximum(_dot(h1b, wup_ref[:, cs]), 0.0)
        acc = acc + _dot((a * a).astype(BF16), wdn_ref[cs, :])
    o_ref[0] = _layer_norm(acc, l2g_ref[...], l2b_ref[...])


def _post_mixer(x, att, ys, u, p, weights, *, tm):
    nb, t, _ = x.shape
    tok = lambda w: pl.BlockSpec((1, tm, w), lambda b, i: (b, i, 0))
    const = lambda a: pl.BlockSpec(a.shape, lambda b, i: (0, 0), pipeline_mode=pl.Buffered(1))
    return pl.pallas_call(
        _post_kernel, grid=(nb, t // tm),
        in_specs=[tok(D_MODEL), tok(ATTN_WIDTH), _lane_tiles_spec(tm, SSM_WIDTH), _lane_tiles_spec(tm, SSM_WIDTH),
                  tok(PLE_DIM)] + [const(w) for w in weights],
        out_specs=tok(D_MODEL), out_shape=jax.ShapeDtypeStruct((nb, t, D_MODEL), F32),
        compiler_params=_params("arbitrary", "arbitrary"), name="post_mixer")(x, att, ys, u, p, *weights)


def kernel(x_prompt, x_sample, cache_k, cache_v, cache_logf, state_re, state_im, page_table, p_prompt, p_sample,
           ln_in_g, ln_in_b, w_in, b_f, lam_re, lam_im, log_dt, b_re, b_im, c_re, c_im, d_skip, w_glu, b_glu,
           w_out, ln1_g, ln1_b, w_up, w_down, w_pe, w_pg, b_pg, ln2_g, ln2_b):
    assert w_in.shape[0] == 1, "one trunk layer"
    nb, t, _ = x_prompt.shape
    db, nq, _ = x_sample.shape
    row = lambda a: a.reshape(1, -1)
    post_w = [row(ln_in_g), row(ln_in_b), row(d_skip[0]), w_glu[0].astype(BF16), row(b_glu[0]),
              w_out[0, :ATTN_WIDTH].astype(BF16), w_out[0, ATTN_WIDTH:].astype(BF16), row(ln1_g[0]), row(ln1_b[0]),
              w_up[0].astype(BF16), w_down[0].astype(BF16), w_pe[0].astype(BF16), w_pg[0].astype(BF16),
              row(b_pg[0]), row(ln2_g[0]), row(ln2_b[0])]
    ssm_par = (lam_re[0], lam_im[0], log_dt[0], b_re[0], b_im[0], c_re[0], c_im[0])

    u, kT, vT, lfT, cT, qTb, kb, vTb, cp = _in_proj(x_prompt, ln_in_g, ln_in_b, w_in[0], b_f[0],
                                                     tm=512, q_scale=QK_SCALE * LOG2E, attn_layouts=True)
    att = _fox_prompt(qTb, kb, cp, vTb, cT, tq=256)
    ssm_ops, ssm_ops_sample = _ssm_prep(*ssm_par, chunk=16, small=nq)
    ys, sr, si = _ssm_seq(u, ssm_ops, chunk=16)
    y_prompt = _post_mixer(x_prompt, att, ys, u, p_prompt[0], post_w, tm=512)
    heads_last = lambda a: jnp.transpose(a.reshape(nb, N_HEADS, HEAD_DIM, t), (0, 3, 1, 2))[None]
    prompt_out = (heads_last(kT), heads_last(vT), jnp.swapaxes(lfT, 1, 2)[None], sr[None], si[None])

    us, qs, ks, vs, logfs = _in_proj(x_sample.reshape(1, db * nq, D_MODEL), ln_in_g, ln_in_b, w_in[0], b_f[0],
                                     tm=db * nq, q_scale=QK_SCALE, attn_layouts=False)
    seq = lambda a: a.reshape(db, nq, a.shape[-1])
    to_tiles = lambda a: jnp.swapaxes(a.reshape(1, db * nq, SSM_WIDTH // LANES, LANES), 1, 2)
    from_tiles = lambda a: jnp.swapaxes(a, 1, 2).reshape(db, nq, SSM_WIDTH)
    att_s = _fox_sample(seq(qs), seq(ks), seq(vs), seq(logfs), cache_k[0], cache_v[0], cache_logf[0], page_table)
    ys_s, sr_s, si_s = _ssm_chunked(from_tiles(us), ssm_ops_sample, (state_re[0], state_im[0]),
                                    chunk=nq, precise=True)
    flat = lambda a: a.reshape(1, db * nq, a.shape[-1])
    y_sample = _post_mixer(flat(x_sample), flat(att_s).astype(BF16), to_tiles(ys_s), us, flat(p_sample[0]),
                           post_w, tm=db * nq).reshape(db, nq, D_MODEL)
    sample_out = (ks.reshape(1, db, nq, N_HEADS, HEAD_DIM), vs.reshape(1, db, nq, N_HEADS, HEAD_DIM),
                  logfs.reshape(1, db, nq, N_HEADS), sr_s[None], si_s[None])
    return (y_prompt, y_sample) + prompt_out + sample_out
```

```python
import functools
import math

import jax
import jax.numpy as jnp
from jax import lax
from jax.experimental import pallas as pl
from jax.experimental.pallas import tpu as pltpu

F32 = jnp.float32
BF16 = jnp.bfloat16

D_MODEL = 1024
ATTN_WIDTH = 512
SSM_WIDTH = 512
HEAD_DIM = 64
N_HEADS = 8
SSM_GROUP = 16
N_GROUPS = 32
STATE_DIM = 64
D_FF = 4096
PLE_DIM = 256
PAGE_SIZE = 128
ALPHA = 2.0 ** 0.25
LN_EPS = 1e-5
NEG_INF = -1e30
QK_SCALE = HEAD_DIM ** -0.5
LOG2E = math.log2(math.e)

VMEM_LIMIT_BYTES = 56 * 1024 * 1024

_NT = (((1,), (1,)), ((), ()))


def _params(*sem):
    return pltpu.CompilerParams(dimension_semantics=sem, vmem_limit_bytes=VMEM_LIMIT_BYTES)


def _const_spec(shape):
    return pl.BlockSpec(shape, lambda *_: (0,) * len(shape))


def _layer_norm(x, g, b):
    mu = jnp.mean(x, axis=-1, keepdims=True)
    xc = x - mu
    var = jnp.mean(xc * xc, axis=-1, keepdims=True)
    return xc * lax.rsqrt(var + LN_EPS) * g + b


def _log_sigmoid(x):
    return jnp.minimum(x, 0.0) - jnp.log1p(jnp.exp(-jnp.abs(x)))


def _sigmoid(x):
    return 1.0 / (1.0 + jnp.exp(-x))


def _gelu_tanh(x):
    return 0.5 * x * (1.0 + jnp.tanh(math.sqrt(2.0 / math.pi) * (x + 0.044715 * (x * x * x))))


def _dot(a, b):
    return jnp.dot(a, b, preferred_element_type=F32)


def _dot_nt(a, b):
    return lax.dot_general(a, b, _NT, preferred_element_type=F32)


def _split3(x):
    hi = x.astype(BF16)
    r1 = x - hi.astype(F32)
    mid = r1.astype(BF16)
    lo = (r1 - mid.astype(F32)).astype(BF16)
    return hi, mid, lo


LANES = 128


def _lane_tiles_spec(tm, width):
    return pl.BlockSpec((1, width // LANES, tm, LANES), lambda b, i: (b, 0, i, 0))


def _store_lane_tiles(ref, x):
    for v in range(x.shape[-1] // LANES):
        ref[0, v] = x[:, v * LANES:(v + 1) * LANES]


def _load_lane_tiles(ref):
    return jnp.concatenate([ref[0, v] for v in range(ref.shape[1])], axis=-1)


def _dot_exact01(x, m01):
    hi, mid, lo = _split3(x)
    return _dot(hi, m01) + _dot(mid, m01) + _dot(lo, m01)


N_BIAS_ROWS = 128
ONES_ROW = 3 * N_HEADS
SUM_ROWS = 16


def _in_proj_kernel(*refs, tm, q_scale, attn_layouts):
    x_ref, g_ref, b_ref, wu_ref = refs[:4]
    if attn_layouts:
        (wqT_ref, wkT_ref, wvT_ref, wfT_ref, bfc_ref, tri_ref,
         u_ref, kT_ref, vT_ref, lfT_ref, cT_ref, qTb_ref, kb_ref, vTb_ref, cp_ref, carry_ref) = refs[4:]
    else:
        wq_ref, wk_ref, wv_ref, wf_ref, bf_ref, u_ref, q_ref, k_ref, v_ref, lf_ref = refs[4:]
    hb = _layer_norm(x_ref[0], g_ref[...], b_ref[...]).astype(BF16)
    _store_lane_tiles(u_ref, _dot(hb, wu_ref[...]))
    if attn_layouts:
        qTb_ref[0] = (_dot_nt(wqT_ref[...], hb) * q_scale).astype(BF16)
        kT = _dot_nt(wkT_ref[...], hb)
        kT_ref[0] = kT
        kb_ref[0] = kT.T.astype(BF16)
        vT = _dot_nt(wvT_ref[...], hb)
        vT_ref[0] = vT
        vTb_ref[0] = vT.astype(BF16)
        lfT = _log_sigmoid(_dot_nt(wfT_ref[...], hb) + bfc_ref[...])
        lfT_ref[0] = lfT

        @pl.when(pl.program_id(1) == 0)
        def _():
            carry_ref[...] = jnp.zeros_like(carry_ref)

        c = carry_ref[...] + _dot_exact01(lfT, tri_ref[...])
        cT_ref[0] = c
        carry_ref[...] = c[:, tm - 1:tm]
        hi, mid, lo = _split3(c * LOG2E)
        row = lax.broadcasted_iota(jnp.int32, (N_BIAS_ROWS - ONES_ROW, tm), 0)
        ones = jnp.where(row < 3, 1.0, 0.0)
        pieces = jnp.concatenate([hi.astype(F32), mid.astype(F32), lo.astype(F32), ones], axis=0)
        cp_ref[0] = pieces.T.astype(BF16)
    else:
        q_ref[0] = _dot(hb, wq_ref[...]) * q_scale
        k_ref[0] = _dot(hb, wk_ref[...])
        v_ref[0] = _dot(hb, wv_ref[...])
        lf_ref[0] = _log_sigmoid(_dot(hb, wf_ref[...]) + bf_ref[...])


def _in_proj(x3, ln_g, ln_b, w_in, b_f, *, tm, q_scale, attn_layouts):
    nb, t, _ = x3.shape
    wq = w_in[:, :ATTN_WIDTH].astype(BF16)
    wk = w_in[:, ATTN_WIDTH:2 * ATTN_WIDTH].astype(BF16)
    wv = w_in[:, 2 * ATTN_WIDTH:3 * ATTN_WIDTH].astype(BF16)
    wf = w_in[:, 3 * ATTN_WIDTH:3 * ATTN_WIDTH + N_HEADS].astype(BF16)
    wu = w_in[:, 3 * ATTN_WIDTH + N_HEADS:].astype(BF16)
    tok = lambda w: pl.BlockSpec((1, tm, w), lambda b, i: (b, i, 0))
    feat = lambda w: pl.BlockSpec((1, w, tm), lambda b, i: (b, 0, i))
    sds = jax.ShapeDtypeStruct
    ins = [x3, ln_g.reshape(1, -1), ln_b.reshape(1, -1), wu]
    in_specs = [tok(D_MODEL), _const_spec((1, D_MODEL)), _const_spec((1, D_MODEL)), _const_spec((D_MODEL, SSM_WIDTH))]
    out_shape = [sds((nb, SSM_WIDTH // LANES, t, LANES), F32)]
    out_specs = [_lane_tiles_spec(tm, SSM_WIDTH)]
    scratch = []
    if attn_layouts:
        tri = (jnp.arange(tm)[:, None] <= jnp.arange(tm)[None, :]).astype(BF16)
        ins += [wq.T, wk.T, wv.T, wf.T, b_f.reshape(N_HEADS, 1), tri]
        in_specs += [_const_spec((ATTN_WIDTH, D_MODEL))] * 3
        in_specs += [_const_spec((N_HEADS, D_MODEL)), _const_spec((N_HEADS, 1)), _const_spec((tm, tm))]
        out_shape += [sds((nb, ATTN_WIDTH, t), F32), sds((nb, ATTN_WIDTH, t), F32),
                      sds((nb, N_HEADS, t), F32), sds((nb, N_HEADS, t), F32),
                      sds((nb, ATTN_WIDTH, t), BF16), sds((nb, t, ATTN_WIDTH), BF16), sds((nb, ATTN_WIDTH, t), BF16),
                      sds((nb, t, N_BIAS_ROWS), BF16)]
        out_specs += [feat(ATTN_WIDTH), feat(ATTN_WIDTH), feat(N_HEADS), feat(N_HEADS),
                      feat(ATTN_WIDTH), tok(ATTN_WIDTH), feat(ATTN_WIDTH), tok(N_BIAS_ROWS)]
        scratch = [pltpu.VMEM((N_HEADS, 1), F32)]
    else:
        ins += [wq, wk, wv, wf, b_f.reshape(1, N_HEADS)]
        in_specs += [_const_spec((D_MODEL, ATTN_WIDTH))] * 3 + [_const_spec((D_MODEL, N_HEADS)), _const_spec((1, N_HEADS))]
        out_shape += [sds((nb, t, ATTN_WIDTH), F32)] * 3 + [sds((nb, t, N_HEADS), F32)]
        out_specs += [tok(ATTN_WIDTH)] * 3 + [tok(N_HEADS)]
    return pl.pallas_call(
        functools.partial(_in_proj_kernel, tm=tm, q_scale=q_scale, attn_layouts=attn_layouts),
        grid=(nb, t // tm), in_specs=in_specs, out_specs=out_specs, out_shape=out_shape,
        scratch_shapes=scratch, compiler_params=_params("arbitrary", "arbitrary"),
        name="in_proj_prompt" if attn_layouts else "in_proj_sample")(*ins)


def _fox_prompt_kernel(qT_ref, k_ref, cp_ref, vT_ref, cT_ref, o_ref, bq_ref, m_ref, acc_ref, s0_ref, s1_ref, *, tq):
    qi = pl.program_id(1)
    q0 = pl.multiple_of(qi * tq, tq)
    heads = [slice(h * HEAD_DIM, (h + 1) * HEAD_DIM) for h in range(N_HEADS)]
    pairs = [slice(2 * HEAD_DIM * (h // 2), 2 * HEAD_DIM * (h // 2 + 1)) for h in range(N_HEADS)]

    c0 = cT_ref[0, :, pl.ds(q0, tq)][:, 0:1] * LOG2E
    c0_pieces = [p.astype(F32) for p in _split3(c0)]
    r = lax.broadcasted_iota(jnp.int32, (2 * HEAD_DIM, tq), 0)
    for h in range(N_HEADS):
        mine = (r < HEAD_DIM) if h % 2 == 0 else (r >= HEAD_DIM)
        bq_ref[h, 0:2 * HEAD_DIM, :] = jnp.where(mine, qT_ref[0, pairs[h], :], jnp.zeros((), BF16))
        sel = jnp.where((r == h) | (r == N_HEADS + h) | (r == 2 * N_HEADS + h), -1.0, 0.0)
        for j, piece in enumerate(c0_pieces):
            sel = jnp.where(r == ONES_ROW + j, piece[h:h + 1, :], sel)
        bq_ref[h, 2 * HEAD_DIM:, :] = sel.astype(BF16)

    key_row = lax.broadcasted_iota(jnp.int32, (tq, tq), 0)
    qry_col = lax.broadcasted_iota(jnp.int32, (tq, tq), 1)

    m_ref[...] = jnp.full_like(m_ref, NEG_INF)
    acc_ref[...] = jnp.zeros_like(acc_ref)
    ones = jnp.ones((SUM_ROWS, tq), BF16)

    s_refs = (s0_ref, s1_ref)

    def scores(kt, h, buf):
        ks = pl.multiple_of(kt * tq, tq)
        a = jnp.concatenate([k_ref[0, pl.ds(ks, tq), pairs[h]], cp_ref[0, pl.ds(ks, tq), :]], axis=1)
        s_refs[buf][h] = _dot(a, bq_ref[h])

    def accumulate(kt, h, buf, diagonal):
        ks = pl.multiple_of(kt * tq, tq)
        s = s_refs[buf][h]
        if diagonal:
            s = jnp.where(key_row <= qry_col, s, NEG_INF)
        m_new = jnp.maximum(m_ref[h], jnp.max(s, axis=0, keepdims=True))
        alpha = jnp.exp2(m_ref[h] - m_new)
        m_ref[h] = m_new
        p = jnp.exp2((s - m_new).astype(BF16))
        v1 = jnp.concatenate([vT_ref[0, heads[h], pl.ds(ks, tq)], ones], axis=0)
        acc_ref[h] = alpha * acc_ref[h] + _dot(v1, p)

    def stage(kt, cur, *, diagonal=False, issue_next=True):
        for h in range(N_HEADS + 1):
            if issue_next and h < N_HEADS:
                scores(kt + 1, h, 1 - cur)
            if h >= 1:
                accumulate(kt, h - 1, cur, diagonal)

    for h in range(N_HEADS):
        scores(0, h, 0)

    def body(j, carry):
        stage(2 * j, 0)
        stage(2 * j + 1, 1)
        return carry

    lax.fori_loop(0, qi // 2, body, 0)

    @pl.when(qi % 2 == 0)
    def _():
        stage(qi, 0, diagonal=True, issue_next=False)

    @pl.when(qi % 2 == 1)
    def _():
        stage(qi - 1, 0)
        stage(qi, 1, diagonal=True, issue_next=False)

    outs = [acc_ref[h, 0:HEAD_DIM, :] / acc_ref[h, HEAD_DIM:HEAD_DIM + 1, :] for h in range(N_HEADS)]
    o_ref[0] = jnp.concatenate(outs, axis=0).T.astype(o_ref.dtype)


def _fox_prompt(qT, k, cp, vT, cT, *, tq):
    nb, t, _ = k.shape
    assert N_BIAS_ROWS == 2 * HEAD_DIM
    whole_f = lambda w: pl.BlockSpec((1, w, t), lambda b, i: (b, 0, 0))
    whole_t = lambda w: pl.BlockSpec((1, t, w), lambda b, i: (b, 0, 0))
    return pl.pallas_call(
        functools.partial(_fox_prompt_kernel, tq=tq),
        grid=(nb, t // tq),
        in_specs=[pl.BlockSpec((1, ATTN_WIDTH, tq), lambda b, i: (b, 0, i)),
                  whole_t(ATTN_WIDTH), whole_t(N_BIAS_ROWS), whole_f(ATTN_WIDTH), whole_f(N_HEADS)],
        out_specs=pl.BlockSpec((1, tq, ATTN_WIDTH), lambda b, i: (b, i, 0)),
        out_shape=jax.ShapeDtypeStruct((nb, t, ATTN_WIDTH), BF16),
        scratch_shapes=[pltpu.VMEM((N_HEADS, 2 * HEAD_DIM + N_BIAS_ROWS, tq), BF16),
                        pltpu.VMEM((N_HEADS, 1, tq), F32), pltpu.VMEM((N_HEADS, HEAD_DIM + SUM_ROWS, tq), F32),
                        pltpu.VMEM((N_HEADS, tq, tq), F32), pltpu.VMEM((N_HEADS, tq, tq), F32)],
        compiler_params=_params("arbitrary", "arbitrary"), name="fox_prompt")(qT, k, cp, vT, cT)


SAMPLE_SEQS_PER_STEP = 2

def _fox_sample_kernel(pt_ref, q_ref, kn_ref, vn_ref, lfn_ref, sl_ref, *refs, n_pages, nq, n_seq):
    del pt_ref
    o_ref = refs[3 * n_seq * n_pages]
    for e in range(n_seq):
        pages = [refs[(kind * n_seq + e) * n_pages:(kind * n_seq + e + 1) * n_pages] for kind in range(3)]
        _fox_sample_one(e, q_ref, kn_ref, vn_ref, lfn_ref, sl_ref, *pages, o_ref, n_pages=n_pages, nq=nq)


def _fox_sample_one(e, q_ref, kn_ref, vn_ref, lfn_ref, sl_ref, k_refs, v_refs, pf_refs, o_ref, *, n_pages, nq):
    nr = nq * N_HEADS
    head_of_lane = lax.broadcasted_iota(jnp.int32, (N_HEADS, ATTN_WIDTH), 1) // HEAD_DIM
    own = head_of_lane == lax.broadcasted_iota(jnp.int32, (N_HEADS, ATTN_WIDTH), 0)
    q = q_ref[e]
    qbd32 = jnp.concatenate([jnp.where(own, q[i:i + 1, :], 0.0) for i in range(nq)], axis=0)
    qbd = qbd32.astype(BF16)
    rep = lambda e: jnp.concatenate([e] * nq, axis=0)
    rows = lambda j: slice(j * N_HEADS, (j + 1) * N_HEADS)

    lfn = lfn_ref[e]
    e_new = rep(_dot_exact01(lfn, sl_ref[...]))
    kn, vn = kn_ref[e], vn_ref[e]
    qidx = lax.broadcasted_iota(jnp.int32, (nr, 1), 0) // N_HEADS
    s_new = [jnp.where(qidx >= j, jnp.sum(qbd32 * kn[j:j + 1, :], axis=-1, keepdims=True) + e_new[:, j:j + 1], NEG_INF)
             for j in range(nq)]

    pf_all = jnp.concatenate([r[0, 0] for r in pf_refs], axis=0)
    sfx_all = _dot_exact01(pf_all, sl_ref[...])
    tot_all = sfx_all[:, 0:1] + pf_all[:, 0:1]
    tail = jnp.sum(lfn, axis=-1, keepdims=True)
    scores = [None] * n_pages
    for j in reversed(range(n_pages)):
        kt = k_refs[j][0, 0].reshape(ATTN_WIDTH, PAGE_SIZE).astype(BF16)
        scores[j] = _dot(qbd, kt) + rep(sfx_all[rows(j)] + tail)
        tail = tail + tot_all[rows(j)]
    s_all = jnp.concatenate(scores, axis=1)
    m = jnp.max(s_all, axis=-1, keepdims=True)
    for s in s_new:
        m = jnp.maximum(m, s)
    p_all = jnp.exp(s_all - m)
    l = jnp.sum(p_all, axis=-1, keepdims=True)
    acc = jnp.zeros((nr, ATTN_WIDTH), F32)
    for j, s in enumerate(s_new):
        p = jnp.exp(s - m)
        l = l + p
        acc = acc + p * vn[j:j + 1, :]
    p_all = p_all.astype(BF16)
    for j in range(n_pages):
        vt = v_refs[j][0, 0].reshape(ATTN_WIDTH, PAGE_SIZE).astype(BF16)
        acc = acc + _dot_nt(p_all[:, j * PAGE_SIZE:(j + 1) * PAGE_SIZE], vt)
    acc = acc / l
    for i in range(nq):
        o_ref[e, i:i + 1, :] = jnp.sum(jnp.where(own, acc[i * N_HEADS:(i + 1) * N_HEADS], 0.0), axis=0, keepdims=True)


def _fox_sample(q, k_new, v_new, logf_new, cache_k, cache_v, cache_logf, page_table):
    db, nq, _ = q.shape
    n_pages = page_table.shape[1]
    kc = jnp.transpose(cache_k, (0, 2, 3, 1))
    vc = jnp.transpose(cache_v, (0, 2, 3, 1))
    pf = jnp.transpose(cache_logf, (0, 2, 1))
    lfn_t = jnp.swapaxes(jnp.pad(logf_new, ((0, 0), (0, PAGE_SIZE - nq), (0, 0))), 1, 2)
    strict_lower = (jnp.arange(PAGE_SIZE)[:, None] > jnp.arange(PAGE_SIZE)[None, :]).astype(BF16)
    ns = SAMPLE_SEQS_PER_STEP
    seq = lambda r, w: pl.BlockSpec((ns, r, w), lambda b, pt: (b, 0, 0))
    page = lambda e, j: pl.BlockSpec((1, 1, N_HEADS, HEAD_DIM, PAGE_SIZE), lambda b, pt: (0, pt[ns * b + e, j], 0, 0, 0))
    page_f = lambda e, j: pl.BlockSpec((1, 1, N_HEADS, PAGE_SIZE), lambda b, pt: (0, pt[ns * b + e, j], 0, 0))
    every = [(e, j) for e in range(ns) for j in range(n_pages)]
    in_specs = [seq(nq, ATTN_WIDTH), seq(nq, ATTN_WIDTH), seq(nq, ATTN_WIDTH), seq(N_HEADS, PAGE_SIZE),
                pl.BlockSpec((PAGE_SIZE, PAGE_SIZE), lambda b, pt: (0, 0))]
    in_specs += [page(e, j) for e, j in every] * 2 + [page_f(e, j) for e, j in every]
    return pl.pallas_call(
        functools.partial(_fox_sample_kernel, n_pages=n_pages, nq=nq, n_seq=ns),
        grid_spec=pltpu.PrefetchScalarGridSpec(
            num_scalar_prefetch=1, grid=(db // ns,), in_specs=in_specs, out_specs=seq(nq, ATTN_WIDTH)),
        out_shape=jax.ShapeDtypeStruct((db, nq, ATTN_WIDTH), F32),
        compiler_params=_params("arbitrary"), name="fox_sample")(
            page_table, q, k_new, v_new, lfn_t, strict_lower,
            *([kc[None]] * len(every)), *([vc[None]] * len(every)), *([pf[None]] * len(every)))


def _cmul(ar, ai, br, bi):
    return ar * br - ai * bi, ar * bi + ai * br


def _cpow_by_bits(ar, ai, n, nbits):
    pr = jnp.ones(jnp.broadcast_shapes(ar.shape, n.shape), F32)
    pi = jnp.zeros_like(pr)
    for b in range(nbits):
        bit = ((n >> b) & 1) == 1
        fr = jnp.where(bit, ar, 1.0)
        fi = jnp.where(bit, ai, 0.0)
        pr, pi = _cmul(pr, pi, fr, fi)
        ar, ai = _cmul(ar, ai, ar, ai)
    return pr, pi


def _ssm_prep_kernel(ldt_ref, lr_ref, li_ref, bre_ref, bim_ref, cre_ref, cim_ref,
                     toep_ref, ctlr_ref, ctli_ref, obsr_ref, obsi_ref, alr_ref, ali_ref, asr_ref, asi_ref,
                     *, chunk, small):
    lw = chunk * SSM_GROUP
    p = STATE_DIM
    nbits = chunk.bit_length()
    dt = jnp.exp(ldt_ref[0])
    lr_row, li_row = lr_ref[0], li_ref[0]

    def discretise(lr, li):
        mag = jnp.exp(lr * dt)
        return mag * jnp.cos(li * dt), mag * jnp.sin(li * dt)

    def column(row):
        eye = lax.broadcasted_iota(jnp.int32, (p, p), 0) == lax.broadcasted_iota(jnp.int32, (p, p), 1)
        return jnp.sum(jnp.where(eye, row, 0.0), axis=-1, keepdims=True)

    lr, li = column(lr_row), column(li_row)
    ar, ai = discretise(lr, li)
    den = lr * lr + li * li
    nr, ni = ar - 1.0, ai
    cr, ci = (nr * lr + ni * li) / den, (ni * lr - nr * li) / den
    btr = jnp.concatenate([bre_ref[0]] * (2 * lw // LANES), axis=1)
    bti = jnp.concatenate([bim_ref[0]] * (2 * lw // LANES), axis=1)
    bbr, bbi = _cmul(cr, ci, btr, bti)
    sig = lax.broadcasted_iota(jnp.int32, (1, 2 * lw), 1) // SSM_GROUP
    er, ei = _cpow_by_bits(ar, ai, jnp.maximum(chunk - 1 - sig, 0), nbits)
    hr, hi = _cmul(er, ei, bbr, bbi)
    hr = jnp.where(sig < chunk, hr, 0.0)
    hi = jnp.where(sig < chunk, hi, 0.0)
    ctlr_ref[0] = hr[:, :lw]
    ctli_ref[0] = hi[:, :lw]
    c_re, c_im = cre_ref[0], cim_ref[0]
    hp = lax.Precision.HIGHEST
    k_all = (jnp.dot(c_re, hr, precision=hp, preferred_element_type=F32)
             - jnp.dot(c_im, hi, precision=hp, preferred_element_type=F32))
    for t in range(chunk):
        off = (chunk - 1 - t) * SSM_GROUP
        toep_ref[0, t * SSM_GROUP:(t + 1) * SSM_GROUP, :] = k_all[:, off:off + lw]

    arr, air = discretise(lr_row, li_row)
    tp1 = lax.broadcasted_iota(jnp.int32, (lw, 1), 0) // SSM_GROUP + 1
    pr, pi = _cpow_by_bits(arr, air, tp1, nbits)
    c_r = jnp.concatenate([c_re] * chunk, axis=0)
    c_i = jnp.concatenate([c_im] * chunk, axis=0)
    obsr_ref[0] = c_r * pr - c_i * pi
    obsi_ref[0] = -(c_r * pi + c_i * pr)
    n = 1
    while n < chunk:
        arr, air = _cmul(arr, air, arr, air)
        n *= 2
        if n == small:
            asr_ref[0] = arr
            asi_ref[0] = air
    alr_ref[0] = arr
    ali_ref[0] = air


def _ssm_prep(lam_re, lam_im, log_dt, b_re, b_im, c_re, c_im, *, chunk, small):
    assert chunk & (chunk - 1) == 0 and small & (small - 1) == 0 and 1 < small < chunk
    g, p, j = N_GROUPS, STATE_DIM, SSM_GROUP
    lw, sw = chunk * j, small * j
    lane_tile = lambda b: jnp.tile(b, (1, 1, LANES // j))
    ins = [log_dt.reshape(g, 1, 1), lam_re.reshape(g, 1, p), lam_im.reshape(g, 1, p), lane_tile(b_re), lane_tile(b_im),
           c_re, c_im]
    grp = lambda a, b: pl.BlockSpec((1, a, b), lambda i: (i, 0, 0))
    in_specs = [grp(1, 1), grp(1, p), grp(1, p), grp(p, LANES), grp(p, LANES), grp(j, p), grp(j, p)]
    shapes = [(lw, lw), (p, lw), (p, lw), (lw, p), (lw, p), (1, p), (1, p), (1, p), (1, p)]
    toep, ctlr, ctli, obsr, obsi, alr, ali, asr, asi = pl.pallas_call(
        functools.partial(_ssm_prep_kernel, chunk=chunk, small=small), grid=(g,), in_specs=in_specs,
        out_specs=[grp(*s) for s in shapes],
        out_shape=[jax.ShapeDtypeStruct((g,) + s, F32) for s in shapes],
        compiler_params=_params("arbitrary"), name="ssm_prep")(*ins)
    ops_small = (toep[:, :sw, :sw], ctlr[:, :, lw - sw:], ctli[:, :, lw - sw:], obsr[:, :sw], obsi[:, :sw], asr, asi)
    return (toep, ctlr, ctli, obsr, obsi, alr, ali), ops_small


def _ssm_chunk_kernel(*refs, n_chunks, n_seq, has_h0, precise):
    if has_h0:
        (u_ref, toep_ref, ctlr_ref, ctli_ref, obsr_ref, obsi_ref, alr_ref, ali_ref, h0r_ref, h0i_ref,
         y_ref, hr_ref, hi_ref) = refs
    else:
        (u_ref, toep_ref, ctlr_ref, ctli_ref, obsr_ref, obsi_ref, alr_ref, ali_ref,
         y_ref, hr_ref, hi_ref) = refs
    if precise:
        cast = lambda x: x
        mm = lambda a, b: lax.dot_general(a, b, _NT, precision=lax.Precision.HIGHEST, preferred_element_type=F32)
    else:
        cast = lambda x: x.astype(BF16)
        mm = _dot_nt
    u = cast(u_ref[0])
    y = mm(u, cast(toep_ref[0]))
    hr = mm(u, cast(ctlr_ref[0]))
    hi = mm(u, cast(ctli_ref[0]))
    ar, ai = alr_ref[0], ali_ref[0]
    if n_chunks == 1:
        pr, pi = h0r_ref[0], h0i_ref[0]
        er, ei = _cmul(ar, ai, pr, pi)
        hr, hi = hr + er, hi + ei
        hr_ref[0] = hr
        hi_ref[0] = hi
    else:
        assert not has_h0
        rows = hr.shape[0]
        cidx = lax.broadcasted_iota(jnp.int32, (rows, 1), 0) % n_chunks

        def shifted(x, d):
            return jnp.where(cidx >= d, pltpu.roll(x, d, axis=0), 0.0)

        d = 1
        while d < n_chunks:
            sr, si = shifted(hr, d), shifted(hi, d)
            er, ei = _cmul(ar, ai, sr, si)
            hr, hi = hr + er, hi + ei
            ar, ai = _cmul(ar, ai, ar, ai)
            d *= 2
        for b in range(n_seq):
            last = (b + 1) * n_chunks - 1
            hr_ref[0, b:b + 1, :] = hr[last:last + 1, :]
            hi_ref[0, b:b + 1, :] = hi[last:last + 1, :]
        pr, pi = shifted(hr, 1), shifted(hi, 1)
    y = y + mm(cast(pr), cast(obsr_ref[0])) + mm(cast(pi), cast(obsi_ref[0]))
    y_ref[0] = y


def _ssm_chunked(u, ops, h0, *, chunk, precise):
    n_seq, t, _ = u.shape
    g, p, j = N_GROUPS, STATE_DIM, SSM_GROUP
    n_chunks = t // chunk
    rows, lw = n_seq * n_chunks, chunk * j
    ug = u.reshape(rows, chunk, g, j).transpose(2, 0, 1, 3).reshape(g, rows, lw)
    if not precise:
        ug = ug.astype(BF16)
    grp = lambda a, b: pl.BlockSpec((1, a, b), lambda i: (i, 0, 0))
    ins = [ug, *ops]
    in_specs = [grp(rows, lw), grp(lw, lw), grp(p, lw), grp(p, lw), grp(lw, p), grp(lw, p), grp(1, p), grp(1, p)]
    if h0 is not None:
        ins += [jnp.swapaxes(h0[0], 0, 1), jnp.swapaxes(h0[1], 0, 1)]
        in_specs += [grp(n_seq, p), grp(n_seq, p)]
    y, hr, hi = pl.pallas_call(
        functools.partial(_ssm_chunk_kernel, n_chunks=n_chunks, n_seq=n_seq, has_h0=h0 is not None, precise=precise),
        grid=(g,), in_specs=in_specs,
        out_specs=[grp(rows, lw), grp(n_seq, p), grp(n_seq, p)],
        out_shape=[jax.ShapeDtypeStruct((g, rows, lw), F32), jax.ShapeDtypeStruct((g, n_seq, p), F32),
                   jax.ShapeDtypeStruct((g, n_seq, p), F32)],
        compiler_params=_params("arbitrary"), name=f"ssm_chunk_{chunk}")(*ins)
    y = y.reshape(g, rows, chunk, j).transpose(1, 2, 0, 3).reshape(n_seq, t, SSM_WIDTH)
    return y, jnp.swapaxes(hr, 0, 1), jnp.swapaxes(hi, 0, 1)


GROUPS_PER_TILE = LANES // SSM_GROUP
GROUPS_PER_TRIP = 4


def _block_transpose(xs):
    n = len(xs)
    blk = lax.broadcasted_iota(jnp.int32, xs[0].shape, 1) // SSM_GROUP
    xs = list(xs)
    d = n // 2
    while d:
        upper = (blk & d) != 0
        for i in range(n):
            if not i & d:
                lo, hi = xs[i], xs[i + d]
                xs[i] = jnp.where(upper, pltpu.roll(hi, SSM_GROUP * d, axis=1), lo)
                xs[i + d] = jnp.where(upper, hi, pltpu.roll(lo, LANES - SSM_GROUP * d, axis=1))
        d //= 2
    return xs


def _ssm_seq_kernel(u_ref, toep_ref, ctlr_ref, ctli_ref, obsr_ref, obsi_ref, alr_ref, ali_ref,
                    y_ref, h_ref, ug_ref, yg_ref, *, chunk):
    t = u_ref.shape[2]
    rows = t // chunk
    lw = chunk * SSM_GROUP
    n = GROUPS_PER_TILE
    p = STATE_DIM
    ridx = lax.broadcasted_iota(jnp.int32, (rows, 1), 0)
    re_half = lax.broadcasted_iota(jnp.int32, (1, 2 * p), 1) < p

    def shifted(x, d):
        return jnp.where(ridx >= d, pltpu.roll(x, d, axis=0), 0.0)

    def group_tile(v, carry):
        for w in range(lw // LANES):
            xs = [u_ref[0, v, pl.ds(n * w + k, rows, stride=chunk), :] for k in range(n)]
            for k, x in enumerate(_block_transpose(xs)):
                ug_ref[k, :, w * LANES:(w + 1) * LANES] = x.astype(BF16)

        def groups(i, c):
            ks = [i * GROUPS_PER_TRIP + e for e in range(GROUPS_PER_TRIP)]
            gs = [v * n + k for k in ks]
            ugs = [ug_ref[k] for k in ks]
            ys = [_dot_nt(ug, toep_ref[g]) for ug, g in zip(ugs, gs)]
            hs = [_dot_nt(ug, jnp.concatenate([ctlr_ref[g], ctli_ref[g]], axis=0))
                  for ug, g in zip(ugs, gs)]
            ars = [jnp.concatenate([alr_ref[g], alr_ref[g]], axis=1) for g in gs]
            ais = [jnp.concatenate([ali_ref[g], ali_ref[g]], axis=1) for g in gs]
            d = 1
            while d < rows:
                for e in range(GROUPS_PER_TRIP):
                    sh = shifted(hs[e], d)
                    hs[e] = hs[e] + ars[e] * sh + jnp.where(re_half, -ais[e], ais[e]) * pltpu.roll(sh, p, axis=1)
                    ars[e], ais[e] = ars[e] * ars[e] - ais[e] * ais[e], 2.0 * ars[e] * ais[e]
                d *= 2
            for e, (k, g) in enumerate(zip(ks, gs)):
                h_ref[0, pl.ds(g, 1), :] = hs[e][rows - 1:rows, :]
                obs = jnp.concatenate([obsr_ref[g], obsi_ref[g]], axis=1)
                yg_ref[k] = ys[e] + _dot_nt(shifted(hs[e], 1).astype(BF16), obs)
            return c

        lax.fori_loop(0, n // GROUPS_PER_TRIP, groups, 0)
        for w in range(lw // LANES):
            ys = [yg_ref[k, :, w * LANES:(w + 1) * LANES] for k in range(n)]
            for k, y in enumerate(_block_transpose(ys)):
                y_ref[0, v, pl.ds(n * w + k, rows, stride=chunk), :] = y
        return carry

    lax.fori_loop(0, N_GROUPS // n, group_tile, 0)


def _ssm_seq(u, ops, *, chunk):
    n_seq, _, t, _ = u.shape
    g, p = N_GROUPS, STATE_DIM
    lw = chunk * SSM_GROUP
    assert lw % LANES == 0 and chunk == GROUPS_PER_TILE * (lw // LANES) and SSM_WIDTH == g * SSM_GROUP
    ops = [o.astype(BF16) for o in ops[:5]] + list(ops[5:])
    whole = lambda a: pl.BlockSpec(a.shape, lambda b: (0, 0, 0), pipeline_mode=pl.Buffered(1))
    seq = pl.BlockSpec((1, SSM_WIDTH // LANES, t, LANES), lambda b: (b, 0, 0, 0))
    y, h = pl.pallas_call(
        functools.partial(_ssm_seq_kernel, chunk=chunk), grid=(n_seq,),
        in_specs=[seq] + [whole(o) for o in ops],
        out_specs=[seq, pl.BlockSpec((1, g, 2 * p), lambda b: (b, 0, 0))],
        out_shape=[jax.ShapeDtypeStruct(u.shape, F32), jax.ShapeDtypeStruct((n_seq, g, 2 * p), F32)],
        scratch_shapes=[pltpu.VMEM((GROUPS_PER_TILE, t // chunk, lw), BF16),
                        pltpu.VMEM((GROUPS_PER_TILE, t // chunk, lw), F32)],
        compiler_params=_params("arbitrary"), name="ssm_seq")(u, *ops)
    return y, h[:, :, :p], h[:, :, p:]


FF_CHUNK = 1024


def _post_kernel(x_ref, att_ref, ys_ref, u_ref, p_ref, lng_ref, lnb_ref, dsk_ref, wglu_ref, bglu_ref,
                 woa_ref, wos_ref, l1g_ref, l1b_ref, wup_ref, wdn_ref, wpe_ref, wpg_ref, bpg_ref,
                 l2g_ref, l2b_ref, o_ref):
    h = _layer_norm(x_ref[0], lng_ref[...], lnb_ref[...])
    y = _gelu_tanh(_load_lane_tiles(ys_ref) + dsk_ref[...] * _load_lane_tiles(u_ref))
    y = y * _sigmoid(_dot(y.astype(BF16), wglu_ref[...]) + bglu_ref[...])
    mix = _dot(att_ref[0], woa_ref[...]) + _dot(y.astype(BF16), wos_ref[...])
    h1 = _layer_norm(ALPHA * h + mix, l1g_ref[...], l1b_ref[...])
    h1b = h1.astype(BF16)
    e = _sigmoid(_dot(h1b, wpg_ref[...]) + bpg_ref[...]) * _dot(p_ref[0].astype(BF16), wpe_ref[...])
    acc = ALPHA * h1 + e
    for c in range(D_FF // FF_CHUNK):
        cs = slice(c * FF_CHUNK, (c + 1) * FF_CHUNK)
        a = jnp.maximum(_dot(h1b, wup_ref[:, cs]), 0.0)
        acc = acc + _dot((a * a).astype(BF16), wdn_ref[cs, :])
    o_ref[0] = _layer_norm(acc, l2g_ref[...], l2b_ref[...])


def _post_mixer(x, att, ys, u, p, weights, *, tm):
    nb, t, _ = x.shape
    tok = lambda w: pl.BlockSpec((1, tm, w), lambda b, i: (b, i, 0))
    const = lambda a: pl.BlockSpec(a.shape, lambda b, i: (0, 0), pipeline_mode=pl.Buffered(1))
    return pl.pallas_call(
        _post_kernel, grid=(nb, t // tm),
        in_specs=[tok(D_MODEL), tok(ATTN_WIDTH), _lane_tiles_spec(tm, SSM_WIDTH), _lane_tiles_spec(tm, SSM_WIDTH),
                  tok(PLE_DIM)] + [const(w) for w in weights],
        out_specs=tok(D_MODEL), out_shape=jax.ShapeDtypeStruct((nb, t, D_MODEL), F32),
        compiler_params=_params("arbitrary", "arbitrary"), name="post_mixer")(x, att, ys, u, p, *weights)


def kernel(x_prompt, x_sample, cache_k, cache_v, cache_logf, state_re, state_im, page_table, p_prompt, p_sample,
           ln_in_g, ln_in_b, w_in, b_f, lam_re, lam_im, log_dt, b_re, b_im, c_re, c_im, d_skip, w_glu, b_glu,
           w_out, ln1_g, ln1_b, w_up, w_down, w_pe, w_pg, b_pg, ln2_g, ln2_b):
    assert w_in.shape[0] == 1, "one trunk layer"
    nb, t, _ = x_prompt.shape
    db, nq, _ = x_sample.shape
    row = lambda a: a.reshape(1, -1)
    post_w = [row(ln_in_g), row(ln_in_b), row(d_skip[0]), w_glu[0].astype(BF16), row(b_glu[0]),
              w_out[0, :ATTN_WIDTH].astype(BF16), w_out[0, ATTN_WIDTH:].astype(BF16), row(ln1_g[0]), row(ln1_b[0]),
              w_up[0].astype(BF16), w_down[0].astype(BF16), w_pe[0].astype(BF16), w_pg[0].astype(BF16),
              row(b_pg[0]), row(ln2_g[0]), row(ln2_b[0])]
    ssm_par = (lam_re[0], lam_im[0], log_dt[0], b_re[0], b_im[0], c_re[0], c_im[0])

    u, kT, vT, lfT, cT, qTb, kb, vTb, cp = _in_proj(x_prompt, ln_in_g, ln_in_b, w_in[0], b_f[0],
                                                     tm=512, q_scale=QK_SCALE * LOG2E, attn_layouts=True)
    att = _fox_prompt(qTb, kb, cp, vTb, cT, tq=256)
    ssm_ops, ssm_ops_sample = _ssm_prep(*ssm_par, chunk=16, small=nq)
    ys, sr, si = _ssm_seq(u, ssm_ops, chunk=16)
    y_prompt = _post_mixer(x_prompt, att, ys, u, p_prompt[0], post_w, tm=512)
    heads_last = lambda a: jnp.transpose(a.reshape(nb, N_HEADS, HEAD_DIM, t), (0, 3, 1, 2))[None]
    prompt_out = (heads_last(kT), heads_last(vT), jnp.swapaxes(lfT, 1, 2)[None], sr[None], si[None])

    us, qs, ks, vs, logfs = _in_proj(x_sample.reshape(1, db * nq, D_MODEL), ln_in_g, ln_in_b, w_in[0], b_f[0],
                                     tm=db * nq, q_scale=QK_SCALE, attn_layouts=False)
    seq = lambda a: a.reshape(db, nq, a.shape[-1])
    to_tiles = lambda a: jnp.swapaxes(a.reshape(1, db * nq, SSM_WIDTH // LANES, LANES), 1, 2)
    from_tiles = lambda a: jnp.swapaxes(a, 1, 2).reshape(db, nq, SSM_WIDTH)
    att_s = _fox_sample(seq(qs), seq(ks), seq(vs), seq(logfs), cache_k[0], cache_v[0], cache_logf[0], page_table)
    ys_s, sr_s, si_s = _ssm_chunked(from_tiles(us), ssm_ops_sample, (state_re[0], state_im[0]),
                                    chunk=nq, precise=True)
    flat = lambda a: a.reshape(1, db * nq, a.shape[-1])
    y_sample = _post_mixer(flat(x_sample), flat(att_s).astype(BF16), to_tiles(ys_s), us, flat(p_sample[0]),
                           post_w, tm=db * nq).reshape(db, nq, D_MODEL)
    sample_out = (ks.reshape(1, db, nq, N_HEADS, HEAD_DIM), vs.reshape(1, db, nq, N_HEADS, HEAD_DIM),
                  logfs.reshape(1, db, nq, N_HEADS), sr_s[None], si_s[None])
    return (y_prompt, y_sample) + prompt_out + sample_out
```

```python
import functools
import math

import jax
import jax.numpy as jnp
from jax import lax
from jax.experimental import pallas as pl
from jax.experimental.pallas import tpu as pltpu

F32 = jnp.float32
BF16 = jnp.bfloat16

D_MODEL = 1024
ATTN_WIDTH = 512
SSM_WIDTH = 512
HEAD_DIM = 64
N_HEADS = 8
SSM_GROUP = 16
N_GROUPS = 32
STATE_DIM = 64
D_FF = 4096
PLE_DIM = 256
PAGE_SIZE = 128
ALPHA = 2.0 ** 0.25
LN_EPS = 1e-5
NEG_INF = -1e30
QK_SCALE = HEAD_DIM ** -0.5
LOG2E = math.log2(math.e)

VMEM_LIMIT_BYTES = 56 * 1024 * 1024

_NT = (((1,), (1,)), ((), ()))


def _params(*sem):
    return pltpu.CompilerParams(dimension_semantics=sem, vmem_limit_bytes=VMEM_LIMIT_BYTES)


def _const_spec(shape):
    return pl.BlockSpec(shape, lambda *_: (0,) * len(shape))


def _layer_norm(x, g, b):
    mu = jnp.mean(x, axis=-1, keepdims=True)
    xc = x - mu
    var = jnp.mean(xc * xc, axis=-1, keepdims=True)
    return xc * lax.rsqrt(var + LN_EPS) * g + b


def _log_sigmoid(x):
    return jnp.minimum(x, 0.0) - jnp.log1p(jnp.exp(-jnp.abs(x)))


def _sigmoid(x):
    return 1.0 / (1.0 + jnp.exp(-x))


def _gelu_tanh(x):
    return 0.5 * x * (1.0 + jnp.tanh(math.sqrt(2.0 / math.pi) * (x + 0.044715 * (x * x * x))))


def _dot(a, b):
    return jnp.dot(a, b, preferred_element_type=F32)


def _dot_nt(a, b):
    return lax.dot_general(a, b, _NT, preferred_element_type=F32)


def _split3(x):
    hi = x.astype(BF16)
    r1 = x - hi.astype(F32)
    mid = r1.astype(BF16)
    lo = (r1 - mid.astype(F32)).astype(BF16)
    return hi, mid, lo


LANES = 128


def _lane_tiles_spec(tm, width):
    return pl.BlockSpec((1, width // LANES, tm, LANES), lambda b, i: (b, 0, i, 0))


def _store_lane_tiles(ref, x):
    for v in range(x.shape[-1] // LANES):
        ref[0, v] = x[:, v * LANES:(v + 1) * LANES]


def _load_lane_tiles(ref):
    return jnp.concatenate([ref[0, v] for v in range(ref.shape[1])], axis=-1)


def _dot_exact01(x, m01):
    hi, mid, lo = _split3(x)
    return _dot(hi, m01) + _dot(mid, m01) + _dot(lo, m01)


N_BIAS_ROWS = 128
ONES_ROW = 3 * N_HEADS
SUM_ROWS = 16


def _in_proj_kernel(*refs, tm, q_scale, attn_layouts):
    x_ref, g_ref, b_ref, wu_ref = refs[:4]
    if attn_layouts:
        (wqT_ref, wkT_ref, wvT_ref, wfT_ref, bfc_ref, tri_ref,
         u_ref, kT_ref, vT_ref, lfT_ref, cT_ref, qTb_ref, kb_ref, vTb_ref, cp_ref, carry_ref) = refs[4:]
    else:
        wq_ref, wk_ref, wv_ref, wf_ref, bf_ref, u_ref, q_ref, k_ref, v_ref, lf_ref = refs[4:]
    hb = _layer_norm(x_ref[0], g_ref[...], b_ref[...]).astype(BF16)
    _store_lane_tiles(u_ref, _dot(hb, wu_ref[...]))
    if attn_layouts:
        qTb_ref[0] = (_dot_nt(wqT_ref[...], hb) * q_scale).astype(BF16)
        kT = _dot_nt(wkT_ref[...], hb)
        kT_ref[0] = kT
        kb_ref[0] = kT.T.astype(BF16)
        vT = _dot_nt(wvT_ref[...], hb)
        vT_ref[0] = vT
        vTb_ref[0] = vT.astype(BF16)
        lfT = _log_sigmoid(_dot_nt(wfT_ref[...], hb) + bfc_ref[...])
        lfT_ref[0] = lfT

        @pl.when(pl.program_id(1) == 0)
        def _():
            carry_ref[...] = jnp.zeros_like(carry_ref)

        c = carry_ref[...] + _dot_exact01(lfT, tri_ref[...])
        cT_ref[0] = c
        carry_ref[...] = c[:, tm - 1:tm]
        hi, mid, lo = _split3(c * LOG2E)
        row = lax.broadcasted_iota(jnp.int32, (N_BIAS_ROWS - ONES_ROW, tm), 0)
        ones = jnp.where(row < 3, 1.0, 0.0)
        pieces = jnp.concatenate([hi.astype(F32), mid.astype(F32), lo.astype(F32), ones], axis=0)
        cp_ref[0] = pieces.T.astype(BF16)
    else:
        q_ref[0] = _dot(hb, wq_ref[...]) * q_scale
        k_ref[0] = _dot(hb, wk_ref[...])
        v_ref[0] = _dot(hb, wv_ref[...])
        lf_ref[0] = _log_sigmoid(_dot(hb, wf_ref[...]) + bf_ref[...])


def _in_proj(x3, ln_g, ln_b, w_in, b_f, *, tm, q_scale, attn_layouts):
    nb, t, _ = x3.shape
    wq = w_in[:, :ATTN_WIDTH].astype(BF16)
    wk = w_in[:, ATTN_WIDTH:2 * ATTN_WIDTH].astype(BF16)
    wv = w_in[:, 2 * ATTN_WIDTH:3 * ATTN_WIDTH].astype(BF16)
    wf = w_in[:, 3 * ATTN_WIDTH:3 * ATTN_WIDTH + N_HEADS].astype(BF16)
    wu = w_in[:, 3 * ATTN_WIDTH + N_HEADS:].astype(BF16)
    tok = lambda w: pl.BlockSpec((1, tm, w), lambda b, i: (b, i, 0))
    feat = lambda w: pl.BlockSpec((1, w, tm), lambda b, i: (b, 0, i))
    sds = jax.ShapeDtypeStruct
    ins = [x3, ln_g.reshape(1, -1), ln_b.reshape(1, -1), wu]
    in_specs = [tok(D_MODEL), _const_spec((1, D_MODEL)), _const_spec((1, D_MODEL)), _const_spec((D_MODEL, SSM_WIDTH))]
    out_shape = [sds((nb, SSM_WIDTH // LANES, t, LANES), F32)]
    out_specs = [_lane_tiles_spec(tm, SSM_WIDTH)]
    scratch = []
    if attn_layouts:
        tri = (jnp.arange(tm)[:, None] <= jnp.arange(tm)[None, :]).astype(BF16)
        ins += [wq.T, wk.T, wv.T, wf.T, b_f.reshape(N_HEADS, 1), tri]
        in_specs += [_const_spec((ATTN_WIDTH, D_MODEL))] * 3
        in_specs += [_const_spec((N_HEADS, D_MODEL)), _const_spec((N_HEADS, 1)), _const_spec((tm, tm))]
        out_shape += [sds((nb, ATTN_WIDTH, t), F32), sds((nb, ATTN_WIDTH, t), F32),
                      sds((nb, N_HEADS, t), F32), sds((nb, N_HEADS, t), F32),
                      sds((nb, ATTN_WIDTH, t), BF16), sds((nb, t, ATTN_WIDTH), BF16), sds((nb, ATTN_WIDTH, t), BF16),
                      sds((nb, t, N_BIAS_ROWS), BF16)]
        out_specs += [feat(ATTN_WIDTH), feat(ATTN_WIDTH), feat(N_HEADS), feat(N_HEADS),
                      feat(ATTN_WIDTH), tok(ATTN_WIDTH), feat(ATTN_WIDTH), tok(N_BIAS_ROWS)]
        scratch = [pltpu.VMEM((N_HEADS, 1), F32)]
    else:
        ins += [wq, wk, wv, wf, b_f.reshape(1, N_HEADS)]
        in_specs += [_const_spec((D_MODEL, ATTN_WIDTH))] * 3 + [_const_spec((D_MODEL, N_HEADS)), _const_spec((1, N_HEADS))]
        out_shape += [sds((nb, t, ATTN_WIDTH), F32)] * 3 + [sds((nb, t, N_HEADS), F32)]
        out_specs += [tok(ATTN_WIDTH)] * 3 + [tok(N_HEADS)]
    return pl.pallas_call(
        functools.partial(_in_proj_kernel, tm=tm, q_scale=q_scale, attn_layouts=attn_layouts),
        grid=(nb, t // tm), in_specs=in_specs, out_specs=out_specs, out_shape=out_shape,
        scratch_shapes=scratch, compiler_params=_params("arbitrary", "arbitrary"),
        name="in_proj_prompt" if attn_layouts else "in_proj_sample")(*ins)


def _fox_prompt_kernel(qT_ref, k_ref, cp_ref, vT_ref, cT_ref, o_ref, bq_ref, m_ref, acc_ref, s0_ref, s1_ref, *, tq):
    qi = pl.program_id(1)
    q0 = pl.multiple_of(qi * tq, tq)
    heads = [slice(h * HEAD_DIM, (h + 1) * HEAD_DIM) for h in range(N_HEADS)]
    pairs = [slice(2 * HEAD_DIM * (h // 2), 2 * HEAD_DIM * (h // 2 + 1)) for h in range(N_HEADS)]

    c0 = cT_ref[0, :, pl.ds(q0, tq)][:, 0:1] * LOG2E
    c0_pieces = [p.astype(F32) for p in _split3(c0)]
    r = lax.broadcasted_iota(jnp.int32, (2 * HEAD_DIM, tq), 0)
    for h in range(N_HEADS):
        mine = (r < HEAD_DIM) if h % 2 == 0 else (r >= HEAD_DIM)
        bq_ref[h, 0:2 * HEAD_DIM, :] = jnp.where(mine, qT_ref[0, pairs[h], :], jnp.zeros((), BF16))
        sel = jnp.where((r == h) | (r == N_HEADS + h) | (r == 2 * N_HEADS + h), -1.0, 0.0)
        for j, piece in enumerate(c0_pieces):
            sel = jnp.where(r == ONES_ROW + j, piece[h:h + 1, :], sel)
        bq_ref[h, 2 * HEAD_DIM:, :] = sel.astype(BF16)

    key_row = lax.broadcasted_iota(jnp.int32, (tq, tq), 0)
    qry_col = lax.broadcasted_iota(jnp.int32, (tq, tq), 1)

    m_ref[...] = jnp.full_like(m_ref, NEG_INF)
    acc_ref[...] = jnp.zeros_like(acc_ref)
    ones = jnp.ones((SUM_ROWS, tq), BF16)

    s_refs = (s0_ref, s1_ref)

    def scores(kt, h, buf):
        ks = pl.multiple_of(kt * tq, tq)
        a = jnp.concatenate([k_ref[0, pl.ds(ks, tq), pairs[h]], cp_ref[0, pl.ds(ks, tq), :]], axis=1)
        s_refs[buf][h] = _dot(a, bq_ref[h])

    def accumulate(kt, h, buf, diagonal):
        ks = pl.multiple_of(kt * tq, tq)
        s = s_refs[buf][h]
        if diagonal:
            s = jnp.where(key_row <= qry_col, s, NEG_INF)
        m_new = jnp.maximum(m_ref[h], jnp.max(s, axis=0, keepdims=True))
        alpha = jnp.exp2(m_ref[h] - m_new)
        m_ref[h] = m_new
        p = jnp.exp2((s - m_new).astype(BF16))
        v1 = jnp.concatenate([vT_ref[0, heads[h], pl.ds(ks, tq)], ones], axis=0)
        acc_ref[h] = alpha * acc_ref[h] + _dot(v1, p)

    def stage(kt, cur, *, diagonal=False, issue_next=True):
        for h in range(N_HEADS + 1):
            if issue_next and h < N_HEADS:
                scores(kt + 1, h, 1 - cur)
            if h >= 1:
                accumulate(kt, h - 1, cur, diagonal)

    for h in range(N_HEADS):
        scores(0, h, 0)

    def body(j, carry):
        stage(2 * j, 0)
        stage(2 * j + 1, 1)
        return carry

    lax.fori_loop(0, qi // 2, body, 0)

    @pl.when(qi % 2 == 0)
    def _():
        stage(qi, 0, diagonal=True, issue_next=False)

    @pl.when(qi % 2 == 1)
    def _():
        stage(qi - 1, 0)
        stage(qi, 1, diagonal=True, issue_next=False)

    outs = [acc_ref[h, 0:HEAD_DIM, :] / acc_ref[h, HEAD_DIM:HEAD_DIM + 1, :] for h in range(N_HEADS)]
    o_ref[0] = jnp.concatenate(outs, axis=0).T.astype(o_ref.dtype)


def _fox_prompt(qT, k, cp, vT, cT, *, tq):
    nb, t, _ = k.shape
    assert N_BIAS_ROWS == 2 * HEAD_DIM
    whole_f = lambda w: pl.BlockSpec((1, w, t), lambda b, i: (b, 0, 0))
    whole_t = lambda w: pl.BlockSpec((1, t, w), lambda b, i: (b, 0, 0))
    return pl.pallas_call(
        functools.partial(_fox_prompt_kernel, tq=tq),
        grid=(nb, t // tq),
        in_specs=[pl.BlockSpec((1, ATTN_WIDTH, tq), lambda b, i: (b, 0, i)),
                  whole_t(ATTN_WIDTH), whole_t(N_BIAS_ROWS), whole_f(ATTN_WIDTH), whole_f(N_HEADS)],
        out_specs=pl.BlockSpec((1, tq, ATTN_WIDTH), lambda b, i: (b, i, 0)),
        out_shape=jax.ShapeDtypeStruct((nb, t, ATTN_WIDTH), BF16),
        scratch_shapes=[pltpu.VMEM((N_HEADS, 2 * HEAD_DIM + N_BIAS_ROWS, tq), BF16),
                        pltpu.VMEM((N_HEADS, 1, tq), F32), pltpu.VMEM((N_HEADS, HEAD_DIM + SUM_ROWS, tq), F32),
                        pltpu.VMEM((N_HEADS, tq, tq), F32), pltpu.VMEM((N_HEADS, tq, tq), F32)],
        compiler_params=_params("arbitrary", "arbitrary"), name="fox_prompt")(qT, k, cp, vT, cT)


SAMPLE_SEQS_PER_STEP = 2

def _fox_sample_kernel(pt_ref, q_ref, kn_ref, vn_ref, lfn_ref, sl_ref, *refs, n_pages, nq, n_seq):
    del pt_ref
    o_ref = refs[3 * n_seq * n_pages]
    for e in range(n_seq):
        pages = [refs[(kind * n_seq + e) * n_pages:(kind * n_seq + e + 1) * n_pages] for kind in range(3)]
        _fox_sample_one(e, q_ref, kn_ref, vn_ref, lfn_ref, sl_ref, *pages, o_ref, n_pages=n_pages, nq=nq)


def _fox_sample_one(e, q_ref, kn_ref, vn_ref, lfn_ref, sl_ref, k_refs, v_refs, pf_refs, o_ref, *, n_pages, nq):
    nr = nq * N_HEADS
    head_of_lane = lax.broadcasted_iota(jnp.int32, (N_HEADS, ATTN_WIDTH), 1) // HEAD_DIM
    own = head_of_lane == lax.broadcasted_iota(jnp.int32, (N_HEADS, ATTN_WIDTH), 0)
    q = q_ref[e]
    qbd32 = jnp.concatenate([jnp.where(own, q[i:i + 1, :], 0.0) for i in range(nq)], axis=0)
    qbd = qbd32.astype(BF16)
    rep = lambda e: jnp.concatenate([e] * nq, axis=0)
    rows = lambda j: slice(j * N_HEADS, (j + 1) * N_HEADS)

    lfn = lfn_ref[e]
    e_new = rep(_dot_exact01(lfn, sl_ref[...]))
    kn, vn = kn_ref[e], vn_ref[e]
    qidx = lax.broadcasted_iota(jnp.int32, (nr, 1), 0) // N_HEADS
    s_new = [jnp.where(qidx >= j, jnp.sum(qbd32 * kn[j:j + 1, :], axis=-1, keepdims=True) + e_new[:, j:j + 1], NEG_INF)
             for j in range(nq)]

    pf_all = jnp.concatenate([r[0, 0] for r in pf_refs], axis=0)
    sfx_all = _dot_exact01(pf_all, sl_ref[...])
    tot_all = sfx_all[:, 0:1] + pf_all[:, 0:1]
    tail = jnp.sum(lfn, axis=-1, keepdims=True)
    scores = [None] * n_pages
    for j in reversed(range(n_pages)):
        kt = k_refs[j][0, 0].reshape(ATTN_WIDTH, PAGE_SIZE).astype(BF16)
        scores[j] = _dot(qbd, kt) + rep(sfx_all[rows(j)] + tail)
        tail = tail + tot_all[rows(j)]
    s_all = jnp.concatenate(scores, axis=1)
    m = jnp.max(s_all, axis=-1, keepdims=True)
    for s in s_new:
        m = jnp.maximum(m, s)
    p_all = jnp.exp(s_all - m)
    l = jnp.sum(p_all, axis=-1, keepdims=True)
    acc = jnp.zeros((nr, ATTN_WIDTH), F32)
    for j, s in enumerate(s_new):
        p = jnp.exp(s - m)
        l = l + p
        acc = acc + p * vn[j:j + 1, :]
    p_all = p_all.astype(BF16)
    for j in range(n_pages):
        vt = v_refs[j][0, 0].reshape(ATTN_WIDTH, PAGE_SIZE).astype(BF16)
        acc = acc + _dot_nt(p_all[:, j * PAGE_SIZE:(j + 1) * PAGE_SIZE], vt)
    acc = acc / l
    for i in range(nq):
        o_ref[e, i:i + 1, :] = jnp.sum(jnp.where(own, acc[i * N_HEADS:(i + 1) * N_HEADS], 0.0), axis=0, keepdims=True)


def _fox_sample(q, k_new, v_new, logf_new, cache_k, cache_v, cache_logf, page_table):
    db, nq, _ = q.shape
    n_pages = page_table.shape[1]
    kc = jnp.transpose(cache_k, (0, 2, 3, 1))
    vc = jnp.transpose(cache_v, (0, 2, 3, 1))
    pf = jnp.transpose(cache_logf, (0, 2, 1))
    lfn_t = jnp.swapaxes(jnp.pad(logf_new, ((0, 0), (0, PAGE_SIZE - nq), (0, 0))), 1, 2)
    strict_lower = (jnp.arange(PAGE_SIZE)[:, None] > jnp.arange(PAGE_SIZE)[None, :]).astype(BF16)
    ns = SAMPLE_SEQS_PER_STEP
    seq = lambda r, w: pl.BlockSpec((ns, r, w), lambda b, pt: (b, 0, 0))
    page = lambda e, j: pl.BlockSpec((1, 1, N_HEADS, HEAD_DIM, PAGE_SIZE), lambda b, pt: (0, pt[ns * b + e, j], 0, 0, 0))
    page_f = lambda e, j: pl.BlockSpec((1, 1, N_HEADS, PAGE_SIZE), lambda b, pt: (0, pt[ns * b + e, j], 0, 0))
    every = [(e, j) for e in range(ns) for j in range(n_pages)]
    in_specs = [seq(nq, ATTN_WIDTH), seq(nq, ATTN_WIDTH), seq(nq, ATTN_WIDTH), seq(N_HEADS, PAGE_SIZE),
                pl.BlockSpec((PAGE_SIZE, PAGE_SIZE), lambda b, pt: (0, 0))]
    in_specs += [page(e, j) for e, j in every] * 2 + [page_f(e, j) for e, j in every]
    return pl.pallas_call(
        functools.partial(_fox_sample_kernel, n_pages=n_pages, nq=nq, n_seq=ns),
        grid_spec=pltpu.PrefetchScalarGridSpec(
            num_scalar_prefetch=1, grid=(db // ns,), in_specs=in_specs, out_specs=seq(nq, ATTN_WIDTH)),
        out_shape=jax.ShapeDtypeStruct((db, nq, ATTN_WIDTH), F32),
        compiler_params=_params("arbitrary"), name="fox_sample")(
            page_table, q, k_new, v_new, lfn_t, strict_lower,
            *([kc[None]] * len(every)), *([vc[None]] * len(every)), *([pf[None]] * len(every)))


def _cmul(ar, ai, br, bi):
    return ar * br - ai * bi, ar * bi + ai * br


def _cpow_by_bits(ar, ai, n, nbits):
    pr = jnp.ones(jnp.broadcast_shapes(ar.shape, n.shape), F32)
    pi = jnp.zeros_like(pr)
    for b in range(nbits):
        bit = ((n >> b) & 1) == 1
        fr = jnp.where(bit, ar, 1.0)
        fi = jnp.where(bit, ai, 0.0)
        pr, pi = _cmul(pr, pi, fr, fi)
        ar, ai = _cmul(ar, ai, ar, ai)
    return pr, pi


def _ssm_prep_kernel(ldt_ref, lr_ref, li_ref, bre_ref, bim_ref, cre_ref, cim_ref,
                     toep_ref, ctl_ref, obs_ref, al_ref,
                     toeps_ref, ctlrs_ref, ctlis_ref, obsrs_ref, obsis_ref, asr_ref, asi_ref, *, chunk, small):
    lw, sw = chunk * SSM_GROUP, small * SSM_GROUP
    p = STATE_DIM
    nbits = chunk.bit_length()
    dt = jnp.exp(ldt_ref[0])
    lr_row, li_row = lr_ref[0], li_ref[0]

    def discretise(lr, li):
        mag = jnp.exp(lr * dt)
        return mag * jnp.cos(li * dt), mag * jnp.sin(li * dt)

    def column(row):
        eye = lax.broadcasted_iota(jnp.int32, (p, p), 0) == lax.broadcasted_iota(jnp.int32, (p, p), 1)
        return jnp.sum(jnp.where(eye, row, 0.0), axis=-1, keepdims=True)

    lr, li = column(lr_row), column(li_row)
    ar, ai = discretise(lr, li)
    den = lr * lr + li * li
    nr, ni = ar - 1.0, ai
    cr, ci = (nr * lr + ni * li) / den, (ni * lr - nr * li) / den
    btr = jnp.concatenate([bre_ref[0]] * (2 * lw // LANES), axis=1)
    bti = jnp.concatenate([bim_ref[0]] * (2 * lw // LANES), axis=1)
    bbr, bbi = _cmul(cr, ci, btr, bti)
    sig = lax.broadcasted_iota(jnp.int32, (1, 2 * lw), 1) // SSM_GROUP
    er, ei = _cpow_by_bits(ar, ai, jnp.maximum(chunk - 1 - sig, 0), nbits)
    hr, hi = _cmul(er, ei, bbr, bbi)
    hr = jnp.where(sig < chunk, hr, 0.0)
    hi = jnp.where(sig < chunk, hi, 0.0)
    ctl_ref[0] = jnp.concatenate([hr[:, :lw], hi[:, :lw]], axis=0).astype(BF16)
    ctlrs_ref[0] = hr[:, lw - sw:lw]
    ctlis_ref[0] = hi[:, lw - sw:lw]
    c_re, c_im = cre_ref[0], cim_ref[0]
    hp = lax.Precision.HIGHEST
    k_all = (jnp.dot(c_re, hr, precision=hp, preferred_element_type=F32)
             - jnp.dot(c_im, hi, precision=hp, preferred_element_type=F32))
    for t in range(chunk):
        off = (chunk - 1 - t) * SSM_GROUP
        toep_ref[0, t * SSM_GROUP:(t + 1) * SSM_GROUP, :] = k_all[:, off:off + lw].astype(BF16)
        if t < small:
            toeps_ref[0, t * SSM_GROUP:(t + 1) * SSM_GROUP, :] = k_all[:, off:off + sw]

    arr, air = discretise(lr_row, li_row)
    tp1 = lax.broadcasted_iota(jnp.int32, (lw, 1), 0) // SSM_GROUP + 1
    pr, pi = _cpow_by_bits(arr, air, tp1, nbits)
    c_r = jnp.concatenate([c_re] * chunk, axis=0)
    c_i = jnp.concatenate([c_im] * chunk, axis=0)
    obs_re = c_r * pr - c_i * pi
    obs_im = -(c_r * pi + c_i * pr)
    obs_ref[0] = jnp.concatenate([obs_re, obs_im], axis=1).astype(BF16)
    obsrs_ref[0] = obs_re[:sw]
    obsis_ref[0] = obs_im[:sw]
    n = 1
    while n < chunk:
        arr, air = _cmul(arr, air, arr, air)
        n *= 2
        if n == small:
            asr_ref[0] = arr
            asi_ref[0] = air
    al_ref[0] = jnp.concatenate([jnp.concatenate([arr, arr], axis=1), jnp.concatenate([-air, air], axis=1)], axis=0)


def _ssm_prep(lam_re, lam_im, log_dt, b_re, b_im, c_re, c_im, *, chunk, small):
    assert chunk & (chunk - 1) == 0 and small & (small - 1) == 0 and 1 < small < chunk
    g, p, j = N_GROUPS, STATE_DIM, SSM_GROUP
    lw, sw = chunk * j, small * j
    lane_tile = lambda b: jnp.tile(b, (1, 1, LANES // j))
    ins = [log_dt.reshape(g, 1, 1), lam_re.reshape(g, 1, p), lam_im.reshape(g, 1, p), lane_tile(b_re), lane_tile(b_im),
           c_re, c_im]
    grp = lambda a, b: pl.BlockSpec((1, a, b), lambda i: (i, 0, 0))
    in_specs = [grp(1, 1), grp(1, p), grp(1, p), grp(p, LANES), grp(p, LANES), grp(j, p), grp(j, p)]
    shapes = [((lw, lw), BF16), ((2 * p, lw), BF16), ((lw, 2 * p), BF16), ((2, 2 * p), F32),
              ((sw, sw), F32), ((p, sw), F32), ((p, sw), F32), ((sw, p), F32), ((sw, p), F32), ((1, p), F32), ((1, p), F32)]
    outs = pl.pallas_call(
        functools.partial(_ssm_prep_kernel, chunk=chunk, small=small), grid=(g,), in_specs=in_specs,
        out_specs=[grp(*s) for s, _ in shapes],
        out_shape=[jax.ShapeDtypeStruct((g,) + s, d) for s, d in shapes],
        compiler_params=_params("arbitrary"), name="ssm_prep")(*ins)
    return outs[:4], outs[4:]


def _ssm_chunk_kernel(*refs, n_chunks, n_seq, has_h0, precise):
    if has_h0:
        (u_ref, toep_ref, ctlr_ref, ctli_ref, obsr_ref, obsi_ref, alr_ref, ali_ref, h0r_ref, h0i_ref,
         y_ref, hr_ref, hi_ref) = refs
    else:
        (u_ref, toep_ref, ctlr_ref, ctli_ref, obsr_ref, obsi_ref, alr_ref, ali_ref,
         y_ref, hr_ref, hi_ref) = refs
    if precise:
        cast = lambda x: x
        mm = lambda a, b: lax.dot_general(a, b, _NT, precision=lax.Precision.HIGHEST, preferred_element_type=F32)
    else:
        cast = lambda x: x.astype(BF16)
        mm = _dot_nt
    u = cast(u_ref[0])
    y = mm(u, cast(toep_ref[0]))
    hr = mm(u, cast(ctlr_ref[0]))
    hi = mm(u, cast(ctli_ref[0]))
    ar, ai = alr_ref[0], ali_ref[0]
    if n_chunks == 1:
        pr, pi = h0r_ref[0], h0i_ref[0]
        er, ei = _cmul(ar, ai, pr, pi)
        hr, hi = hr + er, hi + ei
        hr_ref[0] = hr
        hi_ref[0] = hi
    else:
        assert not has_h0
        rows = hr.shape[0]
        cidx = lax.broadcasted_iota(jnp.int32, (rows, 1), 0) % n_chunks

        def shifted(x, d):
            return jnp.where(cidx >= d, pltpu.roll(x, d, axis=0), 0.0)

        d = 1
        while d < n_chunks:
            sr, si = shifted(hr, d), shifted(hi, d)
            er, ei = _cmul(ar, ai, sr, si)
            hr, hi = hr + er, hi + ei
            ar, ai = _cmul(ar, ai, ar, ai)
            d *= 2
        for b in range(n_seq):
            last = (b + 1) * n_chunks - 1
            hr_ref[0, b:b + 1, :] = hr[last:last + 1, :]
            hi_ref[0, b:b + 1, :] = hi[last:last + 1, :]
        pr, pi = shifted(hr, 1), shifted(hi, 1)
    y = y + mm(cast(pr), cast(obsr_ref[0])) + mm(cast(pi), cast(obsi_ref[0]))
    y_ref[0] = y


def _ssm_chunked(u, ops, h0, *, chunk, precise):
    n_seq, t, _ = u.shape
    g, p, j = N_GROUPS, STATE_DIM, SSM_GROUP
    n_chunks = t // chunk
    rows, lw = n_seq * n_chunks, chunk * j
    ug = u.reshape(rows, chunk, g, j).transpose(2, 0, 1, 3).reshape(g, rows, lw)
    if not precise:
        ug = ug.astype(BF16)
    grp = lambda a, b: pl.BlockSpec((1, a, b), lambda i: (i, 0, 0))
    ins = [ug, *ops]
    in_specs = [grp(rows, lw), grp(lw, lw), grp(p, lw), grp(p, lw), grp(lw, p), grp(lw, p), grp(1, p), grp(1, p)]
    if h0 is not None:
        ins += [jnp.swapaxes(h0[0], 0, 1), jnp.swapaxes(h0[1], 0, 1)]
        in_specs += [grp(n_seq, p), grp(n_seq, p)]
    y, hr, hi = pl.pallas_call(
        functools.partial(_ssm_chunk_kernel, n_chunks=n_chunks, n_seq=n_seq, has_h0=h0 is not None, precise=precise),
        grid=(g,), in_specs=in_specs,
        out_specs=[grp(rows, lw), grp(n_seq, p), grp(n_seq, p)],
        out_shape=[jax.ShapeDtypeStruct((g, rows, lw), F32), jax.ShapeDtypeStruct((g, n_seq, p), F32),
                   jax.ShapeDtypeStruct((g, n_seq, p), F32)],
        compiler_params=_params("arbitrary"), name=f"ssm_chunk_{chunk}")(*ins)
    y = y.reshape(g, rows, chunk, j).transpose(1, 2, 0, 3).reshape(n_seq, t, SSM_WIDTH)
    return y, jnp.swapaxes(hr, 0, 1), jnp.swapaxes(hi, 0, 1)


GROUPS_PER_TILE = LANES // SSM_GROUP
GROUPS_PER_TRIP = 4


def _block_transpose(xs):
    n = len(xs)
    blk = lax.broadcasted_iota(jnp.int32, xs[0].shape, 1) // SSM_GROUP
    xs = list(xs)
    d = n // 2
    while d:
        upper = (blk & d) != 0
        for i in range(n):
            if not i & d:
                lo, hi = xs[i], xs[i + d]
                xs[i] = jnp.where(upper, pltpu.roll(hi, SSM_GROUP * d, axis=1), lo)
                xs[i + d] = jnp.where(upper, hi, pltpu.roll(lo, LANES - SSM_GROUP * d, axis=1))
        d //= 2
    return xs


def _ssm_seq_kernel(u_ref, toep_ref, ctl_ref, obs_ref, al_ref, y_ref, h_ref, ug_ref, yg_ref, *, chunk):
    t = u_ref.shape[2]
    rows = t // chunk
    lw = chunk * SSM_GROUP
    n = GROUPS_PER_TILE
    p = STATE_DIM
    ridx = lax.broadcasted_iota(jnp.int32, (rows, 1), 0)

    def shifted(x, d):
        return jnp.where(ridx >= d, pltpu.roll(x, d, axis=0), 0.0)

    def group_tile(v, carry):
        for w in range(lw // LANES):
            xs = [u_ref[0, v, pl.ds(n * w + k, rows, stride=chunk), :] for k in range(n)]
            for k, x in enumerate(_block_transpose(xs)):
                ug_ref[k, :, w * LANES:(w + 1) * LANES] = x.astype(BF16)

        def groups(i, c):
            ks = [i * GROUPS_PER_TRIP + e for e in range(GROUPS_PER_TRIP)]
            gs = [v * n + k for k in ks]
            ugs = [ug_ref[k] for k in ks]
            ys = [_dot_nt(ug, toep_ref[g]) for ug, g in zip(ugs, gs)]
            hs = [_dot_nt(ug, ctl_ref[g]) for ug, g in zip(ugs, gs)]
            ars = [al_ref[g][0:1, :] for g in gs]
            ais = [al_ref[g][1:2, :] for g in gs]
            d = 1
            while d < rows:
                for e in range(GROUPS_PER_TRIP):
                    sh = shifted(hs[e], d)
                    hs[e] = hs[e] + ars[e] * sh + ais[e] * pltpu.roll(sh, p, axis=1)
                    ars[e], ais[e] = ars[e] * ars[e] - ais[e] * ais[e], 2.0 * ars[e] * ais[e]
                d *= 2
            for e, (k, g) in enumerate(zip(ks, gs)):
                h_ref[0, pl.ds(g, 1), :] = hs[e][rows - 1:rows, :]
                yg_ref[k] = ys[e] + _dot_nt(shifted(hs[e], 1).astype(BF16), obs_ref[g])
            return c

        lax.fori_loop(0, n // GROUPS_PER_TRIP, groups, 0)
        for w in range(lw // LANES):
            ys = [yg_ref[k, :, w * LANES:(w + 1) * LANES] for k in range(n)]
            for k, y in enumerate(_block_transpose(ys)):
                y_ref[0, v, pl.ds(n * w + k, rows, stride=chunk), :] = y
        return carry

    lax.fori_loop(0, N_GROUPS // n, group_tile, 0)


def _ssm_seq(u, ops, *, chunk):
    n_seq, _, t, _ = u.shape
    g, p = N_GROUPS, STATE_DIM
    lw = chunk * SSM_GROUP
    assert lw % LANES == 0 and chunk == GROUPS_PER_TILE * (lw // LANES) and SSM_WIDTH == g * SSM_GROUP
    whole = lambda a: pl.BlockSpec(a.shape, lambda b: (0, 0, 0), pipeline_mode=pl.Buffered(1))
    seq = pl.BlockSpec((1, SSM_WIDTH // LANES, t, LANES), lambda b: (b, 0, 0, 0))
    y, h = pl.pallas_call(
        functools.partial(_ssm_seq_kernel, chunk=chunk), grid=(n_seq,),
        in_specs=[seq] + [whole(o) for o in ops],
        out_specs=[seq, pl.BlockSpec((1, g, 2 * p), lambda b: (b, 0, 0))],
        out_shape=[jax.ShapeDtypeStruct(u.shape, F32), jax.ShapeDtypeStruct((n_seq, g, 2 * p), F32)],
        scratch_shapes=[pltpu.VMEM((GROUPS_PER_TILE, t // chunk, lw), BF16),
                        pltpu.VMEM((GROUPS_PER_TILE, t // chunk, lw), F32)],
        compiler_params=_params("arbitrary"), name="ssm_seq")(u, *ops)
    return y, h[:, :, :p], h[:, :, p:]


FF_CHUNK = 1024


def _post_kernel(x_ref, att_ref, ys_ref, u_ref, p_ref, lng_ref, lnb_ref, dsk_ref, wglu_ref, bglu_ref,
                 woa_ref, wos_ref, l1g_ref, l1b_ref, wup_ref, wdn_ref, wpe_ref, wpg_ref, bpg_ref,
                 l2g_ref, l2b_ref, o_ref):
    h = _layer_norm(x_ref[0], lng_ref[...], lnb_ref[...])
    y = _gelu_tanh(_load_lane_tiles(ys_ref) + dsk_ref[...] * _load_lane_tiles(u_ref))
    y = y * _sigmoid(_dot(y.astype(BF16), wglu_ref[...]) + bglu_ref[...])
    mix = _dot(att_ref[0], woa_ref[...]) + _dot(y.astype(BF16), wos_ref[...])
    h1 = _layer_norm(ALPHA * h + mix, l1g_ref[...], l1b_ref[...])
    h1b = h1.astype(BF16)
    e = _sigmoid(_dot(h1b, wpg_ref[...]) + bpg_ref[...]) * _dot(p_ref[0].astype(BF16), wpe_ref[...])
    acc = ALPHA * h1 + e
    for c in range(D_FF // FF_CHUNK):
        cs = slice(c * FF_CHUNK, (c + 1) * FF_CHUNK)
        a = jnp.maximum(_dot(h1b, wup_ref[:, cs]), 0.0)
        acc = acc + _dot((a * a).astype(BF16), wdn_ref[cs, :])
    o_ref[0] = _layer_norm(acc, l2g_ref[...], l2b_ref[...])


def _post_mixer(x, att, ys, u, p, weights, *, tm):
    nb, t, _ = x.shape
    tok = lambda w: pl.BlockSpec((1, tm, w), lambda b, i: (b, i, 0))
    const = lambda a: pl.BlockSpec(a.shape, lambda b, i: (0, 0), pipeline_mode=pl.Buffered(1))
    return pl.pallas_call(
        _post_kernel, grid=(nb, t // tm),
        in_specs=[tok(D_MODEL), tok(ATTN_WIDTH), _lane_tiles_spec(tm, SSM_WIDTH), _lane_tiles_spec(tm, SSM_WIDTH),
                  tok(PLE_DIM)] + [const(w) for w in weights],
        out_specs=tok(D_MODEL), out_shape=jax.ShapeDtypeStruct((nb, t, D_MODEL), F32),
        compiler_params=_params("arbitrary", "arbitrary"), name="post_mixer")(x, att, ys, u, p, *weights)


def kernel(x_prompt, x_sample, cache_k, cache_v, cache_logf, state_re, state_im, page_table, p_prompt, p_sample,
           ln_in_g, ln_in_b, w_in, b_f, lam_re, lam_im, log_dt, b_re, b_im, c_re, c_im, d_skip, w_glu, b_glu,
           w_out, ln1_g, ln1_b, w_up, w_down, w_pe, w_pg, b_pg, ln2_g, ln2_b):
    assert w_in.shape[0] == 1, "one trunk layer"
    nb, t, _ = x_prompt.shape
    db, nq, _ = x_sample.shape
    row = lambda a: a.reshape(1, -1)
    post_w = [row(ln_in_g), row(ln_in_b), row(d_skip[0]), w_glu[0].astype(BF16), row(b_glu[0]),
              w_out[0, :ATTN_WIDTH].astype(BF16), w_out[0, ATTN_WIDTH:].astype(BF16), row(ln1_g[0]), row(ln1_b[0]),
              w_up[0].astype(BF16), w_down[0].astype(BF16), w_pe[0].astype(BF16), w_pg[0].astype(BF16),
              row(b_pg[0]), row(ln2_g[0]), row(ln2_b[0])]
    ssm_par = (lam_re[0], lam_im[0], log_dt[0], b_re[0], b_im[0], c_re[0], c_im[0])

    u, kT, vT, lfT, cT, qTb, kb, vTb, cp = _in_proj(x_prompt, ln_in_g, ln_in_b, w_in[0], b_f[0],
                                                     tm=512, q_scale=QK_SCALE * LOG2E, attn_layouts=True)
    att = _fox_prompt(qTb, kb, cp, vTb, cT, tq=256)
    ssm_ops, ssm_ops_sample = _ssm_prep(*ssm_par, chunk=16, small=nq)
    ys, sr, si = _ssm_seq(u, ssm_ops, chunk=16)
    y_prompt = _post_mixer(x_prompt, att, ys, u, p_prompt[0], post_w, tm=512)
    heads_last = lambda a: jnp.transpose(a.reshape(nb, N_HEADS, HEAD_DIM, t), (0, 3, 1, 2))[None]
    prompt_out = (heads_last(kT), heads_last(vT), jnp.swapaxes(lfT, 1, 2)[None], sr[None], si[None])

    us, qs, ks, vs, logfs = _in_proj(x_sample.reshape(1, db * nq, D_MODEL), ln_in_g, ln_in_b, w_in[0], b_f[0],
                                     tm=db * nq, q_scale=QK_SCALE, attn_layouts=False)
    seq = lambda a: a.reshape(db, nq, a.shape[-1])
    to_tiles = lambda a: jnp.swapaxes(a.reshape(1, db * nq, SSM_WIDTH // LANES, LANES), 1, 2)
    from_tiles = lambda a: jnp.swapaxes(a, 1, 2).reshape(db, nq, SSM_WIDTH)
    att_s = _fox_sample(seq(qs), seq(ks), seq(vs), seq(logfs), cache_k[0], cache_v[0], cache_logf[0], page_table)
    ys_s, sr_s, si_s = _ssm_chunked(from_tiles(us), ssm_ops_sample, (state_re[0], state_im[0]),
                                    chunk=nq, precise=False)
    flat = lambda a: a.reshape(1, db * nq, a.shape[-1])
    y_sample = _post_mixer(flat(x_sample), flat(att_s).astype(BF16), to_tiles(ys_s), us, flat(p_sample[0]),
                           post_w, tm=db * nq).reshape(db, nq, D_MODEL)
    sample_out = (ks.reshape(1, db, nq, N_HEADS, HEAD_DIM), vs.reshape(1, db, nq, N_HEADS, HEAD_DIM),
                  logfs.reshape(1, db, nq, N_HEADS), sr_s[None], si_s[None])
    return (y_prompt, y_sample) + prompt_out + sample_out
```

```python
import functools
import math

import jax
import jax.numpy as jnp
from jax import lax
from jax.experimental import pallas as pl
from jax.experimental.pallas import tpu as pltpu

F32 = jnp.float32
BF16 = jnp.bfloat16

D_MODEL = 1024
ATTN_WIDTH = 512
SSM_WIDTH = 512
HEAD_DIM = 64
N_HEADS = 8
SSM_GROUP = 16
N_GROUPS = 32
STATE_DIM = 64
D_FF = 4096
PLE_DIM = 256
PAGE_SIZE = 128
ALPHA = 2.0 ** 0.25
LN_EPS = 1e-5
NEG_INF = -1e30
QK_SCALE = HEAD_DIM ** -0.5
LOG2E = math.log2(math.e)

VMEM_LIMIT_BYTES = 56 * 1024 * 1024

IN_PROJ_ROWS = 512
ATTN_TILE = 256
SSM_CHUNK = 16
POST_ROWS = 512

_NT = (((1,), (1,)), ((), ()))


def _params(*sem):
    return pltpu.CompilerParams(dimension_semantics=sem, vmem_limit_bytes=VMEM_LIMIT_BYTES)


def _const_spec(shape):
    return pl.BlockSpec(shape, lambda *_: (0,) * len(shape))


def _layer_norm(x, g, b):
    mu = jnp.mean(x, axis=-1, keepdims=True)
    xc = x - mu
    var = jnp.mean(xc * xc, axis=-1, keepdims=True)
    return xc * lax.rsqrt(var + LN_EPS) * g + b


def _log_sigmoid(x):
    return jnp.minimum(x, 0.0) - jnp.log1p(jnp.exp(-jnp.abs(x)))


def _sigmoid(x):
    return 1.0 / (1.0 + jnp.exp(-x))


def _gelu_tanh(x):
    return 0.5 * x * (1.0 + jnp.tanh(math.sqrt(2.0 / math.pi) * (x + 0.044715 * (x * x * x))))


def _dot(a, b):
    return jnp.dot(a, b, preferred_element_type=F32)


def _dot_nt(a, b):
    return lax.dot_general(a, b, _NT, preferred_element_type=F32)


def _split3(x):
    hi = x.astype(BF16)
    r1 = x - hi.astype(F32)
    mid = r1.astype(BF16)
    lo = (r1 - mid.astype(F32)).astype(BF16)
    return hi, mid, lo


LANES = 128


def _lane_tiles_spec(tm, width):
    return pl.BlockSpec((1, width // LANES, tm, LANES), lambda b, i: (b, 0, i, 0))


def _store_lane_tiles(ref, x):
    for v in range(x.shape[-1] // LANES):
        ref[0, v] = x[:, v * LANES:(v + 1) * LANES]


def _load_lane_tiles(ref):
    return jnp.concatenate([ref[0, v] for v in range(ref.shape[1])], axis=-1)


def _dot_exact01(x, m01):
    hi, mid, lo = _split3(x)
    return _dot(hi, m01) + _dot(mid, m01) + _dot(lo, m01)


N_BIAS_ROWS = 128
ONES_ROW = 3 * N_HEADS
SUM_ROWS = 16


def _in_proj_kernel(*refs, tm, q_scale, attn_layouts):
    x_ref, g_ref, b_ref, wu_ref = refs[:4]
    if attn_layouts:
        (wqT_ref, wkT_ref, wvT_ref, wfT_ref, bfc_ref, tri_ref,
         u_ref, kT_ref, vT_ref, lfT_ref, cT_ref, qTb_ref, kb_ref, vTb_ref, cp_ref, carry_ref) = refs[4:]
    else:
        wq_ref, wk_ref, wv_ref, wf_ref, bf_ref, u_ref, q_ref, k_ref, v_ref, lf_ref = refs[4:]
    hb = _layer_norm(x_ref[0], g_ref[...], b_ref[...]).astype(BF16)
    _store_lane_tiles(u_ref, _dot(hb, wu_ref[...]))
    if attn_layouts:
        qTb_ref[0] = (_dot_nt(wqT_ref[...], hb) * q_scale).astype(BF16)
        kT = _dot_nt(wkT_ref[...], hb)
        kT_ref[0] = kT
        kb_ref[0] = kT.T.astype(BF16)
        vT = _dot_nt(wvT_ref[...], hb)
        vT_ref[0] = vT
        vTb_ref[0] = vT.astype(BF16)
        lfT = _log_sigmoid(_dot_nt(wfT_ref[...], hb) + bfc_ref[...])
        lfT_ref[0] = lfT

        @pl.when(pl.program_id(1) == 0)
        def _():
            carry_ref[...] = jnp.zeros_like(carry_ref)

        c = carry_ref[...] + _dot_exact01(lfT, tri_ref[...])
        cT_ref[0] = c
        carry_ref[...] = c[:, tm - 1:tm]
        hi, mid, lo = _split3(c * LOG2E)
        row = lax.broadcasted_iota(jnp.int32, (N_BIAS_ROWS - ONES_ROW, tm), 0)
        ones = jnp.where(row < 3, 1.0, 0.0)
        pieces = jnp.concatenate([hi.astype(F32), mid.astype(F32), lo.astype(F32), ones], axis=0)
        cp_ref[0] = pieces.T.astype(BF16)
    else:
        q_ref[0] = _dot(hb, wq_ref[...]) * q_scale
        k_ref[0] = _dot(hb, wk_ref[...])
        v_ref[0] = _dot(hb, wv_ref[...])
        lf_ref[0] = _log_sigmoid(_dot(hb, wf_ref[...]) + bf_ref[...])


def _in_proj(x3, ln_g, ln_b, w_in, b_f, *, tm, q_scale, attn_layouts):
    nb, t, _ = x3.shape
    wq = w_in[:, :ATTN_WIDTH].astype(BF16)
    wk = w_in[:, ATTN_WIDTH:2 * ATTN_WIDTH].astype(BF16)
    wv = w_in[:, 2 * ATTN_WIDTH:3 * ATTN_WIDTH].astype(BF16)
    wf = w_in[:, 3 * ATTN_WIDTH:3 * ATTN_WIDTH + N_HEADS].astype(BF16)
    wu = w_in[:, 3 * ATTN_WIDTH + N_HEADS:].astype(BF16)
    tok = lambda w: pl.BlockSpec((1, tm, w), lambda b, i: (b, i, 0))
    feat = lambda w: pl.BlockSpec((1, w, tm), lambda b, i: (b, 0, i))
    sds = jax.ShapeDtypeStruct
    ins = [x3, ln_g.reshape(1, -1), ln_b.reshape(1, -1), wu]
    in_specs = [tok(D_MODEL), _const_spec((1, D_MODEL)), _const_spec((1, D_MODEL)), _const_spec((D_MODEL, SSM_WIDTH))]
    out_shape = [sds((nb, SSM_WIDTH // LANES, t, LANES), F32)]
    out_specs = [_lane_tiles_spec(tm, SSM_WIDTH)]
    scratch = []
    if attn_layouts:
        tri = (jnp.arange(tm)[:, None] <= jnp.arange(tm)[None, :]).astype(BF16)
        ins += [wq.T, wk.T, wv.T, wf.T, b_f.reshape(N_HEADS, 1), tri]
        in_specs += [_const_spec((ATTN_WIDTH, D_MODEL))] * 3
        in_specs += [_const_spec((N_HEADS, D_MODEL)), _const_spec((N_HEADS, 1)), _const_spec((tm, tm))]
        out_shape += [sds((nb, ATTN_WIDTH, t), F32), sds((nb, ATTN_WIDTH, t), F32),
                      sds((nb, N_HEADS, t), F32), sds((nb, N_HEADS, t), F32),
                      sds((nb, ATTN_WIDTH, t), BF16), sds((nb, t, ATTN_WIDTH), BF16), sds((nb, ATTN_WIDTH, t), BF16),
                      sds((nb, t, N_BIAS_ROWS), BF16)]
        out_specs += [feat(ATTN_WIDTH), feat(ATTN_WIDTH), feat(N_HEADS), feat(N_HEADS),
                      feat(ATTN_WIDTH), tok(ATTN_WIDTH), feat(ATTN_WIDTH), tok(N_BIAS_ROWS)]
        scratch = [pltpu.VMEM((N_HEADS, 1), F32)]
    else:
        ins += [wq, wk, wv, wf, b_f.reshape(1, N_HEADS)]
        in_specs += [_const_spec((D_MODEL, ATTN_WIDTH))] * 3 + [_const_spec((D_MODEL, N_HEADS)), _const_spec((1, N_HEADS))]
        out_shape += [sds((nb, t, ATTN_WIDTH), F32)] * 3 + [sds((nb, t, N_HEADS), F32)]
        out_specs += [tok(ATTN_WIDTH)] * 3 + [tok(N_HEADS)]
    return pl.pallas_call(
        functools.partial(_in_proj_kernel, tm=tm, q_scale=q_scale, attn_layouts=attn_layouts),
        grid=(nb, t // tm), in_specs=in_specs, out_specs=out_specs, out_shape=out_shape,
        scratch_shapes=scratch, compiler_params=_params("arbitrary", "arbitrary"),
        name="in_proj_prompt" if attn_layouts else "in_proj_sample")(*ins)


def _fox_prompt_kernel(qT_ref, k_ref, cp_ref, vT_ref, cT_ref, o_ref, bq_ref, m_ref, acc_ref, s0_ref, s1_ref, *, tq):
    qi = pl.program_id(1)
    q0 = pl.multiple_of(qi * tq, tq)
    heads = [slice(h * HEAD_DIM, (h + 1) * HEAD_DIM) for h in range(N_HEADS)]
    pairs = [slice(2 * HEAD_DIM * (h // 2), 2 * HEAD_DIM * (h // 2 + 1)) for h in range(N_HEADS)]

    c0 = cT_ref[0, :, pl.ds(q0, tq)][:, 0:1] * LOG2E
    c0_pieces = [p.astype(F32) for p in _split3(c0)]
    r = lax.broadcasted_iota(jnp.int32, (2 * HEAD_DIM, tq), 0)
    for h in range(N_HEADS):
        mine = (r < HEAD_DIM) if h % 2 == 0 else (r >= HEAD_DIM)
        bq_ref[h, 0:2 * HEAD_DIM, :] = jnp.where(mine, qT_ref[0, pairs[h], :], jnp.zeros((), BF16))
        sel = jnp.where((r == h) | (r == N_HEADS + h) | (r == 2 * N_HEADS + h), -1.0, 0.0)
        for j, piece in enumerate(c0_pieces):
            sel = jnp.where(r == ONES_ROW + j, piece[h:h + 1, :], sel)
        bq_ref[h, 2 * HEAD_DIM:, :] = sel.astype(BF16)

    key_row = lax.broadcasted_iota(jnp.int32, (tq, tq), 0)
    qry_col = lax.broadcasted_iota(jnp.int32, (tq, tq), 1)

    m_ref[...] = jnp.full_like(m_ref, NEG_INF)
    acc_ref[...] = jnp.zeros_like(acc_ref)
    ones = jnp.ones((SUM_ROWS, tq), BF16)

    s_refs = (s0_ref, s1_ref)

    def scores(kt, h, buf):
        ks = pl.multiple_of(kt * tq, tq)
        a = jnp.concatenate([k_ref[0, pl.ds(ks, tq), pairs[h]], cp_ref[0, pl.ds(ks, tq), :]], axis=1)
        s_refs[buf][h] = _dot(a, bq_ref[h])

    def accumulate(kt, h, buf, diagonal):
        ks = pl.multiple_of(kt * tq, tq)
        s = s_refs[buf][h]
        if diagonal:
            s = jnp.where(key_row <= qry_col, s, NEG_INF)
        m_new = jnp.maximum(m_ref[h], jnp.max(s, axis=0, keepdims=True))
        alpha = jnp.exp2(m_ref[h] - m_new)
        m_ref[h] = m_new
        p = jnp.exp2((s - m_new).astype(BF16))
        v1 = jnp.concatenate([vT_ref[0, heads[h], pl.ds(ks, tq)], ones], axis=0)
        acc_ref[h] = alpha * acc_ref[h] + _dot(v1, p)

    def stage(kt, cur, *, diagonal=False, issue_next=True):
        for h in range(N_HEADS + 1):
            if issue_next and h < N_HEADS:
                scores(kt + 1, h, 1 - cur)
            if h >= 1:
                accumulate(kt, h - 1, cur, diagonal)

    for h in range(N_HEADS):
        scores(0, h, 0)

    def body(j, carry):
        stage(2 * j, 0)
        stage(2 * j + 1, 1)
        return carry

    lax.fori_loop(0, qi // 2, body, 0)

    @pl.when(qi % 2 == 0)
    def _():
        stage(qi, 0, diagonal=True, issue_next=False)

    @pl.when(qi % 2 == 1)
    def _():
        stage(qi - 1, 0)
        stage(qi, 1, diagonal=True, issue_next=False)

    outs = [acc_ref[h, 0:HEAD_DIM, :] / acc_ref[h, HEAD_DIM:HEAD_DIM + 1, :] for h in range(N_HEADS)]
    o_ref[0] = jnp.concatenate(outs, axis=0).T.astype(o_ref.dtype)


def _fox_prompt(qT, k, cp, vT, cT, *, tq):
    nb, t, _ = k.shape
    assert N_BIAS_ROWS == 2 * HEAD_DIM
    whole_f = lambda w: pl.BlockSpec((1, w, t), lambda b, i: (b, 0, 0))
    whole_t = lambda w: pl.BlockSpec((1, t, w), lambda b, i: (b, 0, 0))
    return pl.pallas_call(
        functools.partial(_fox_prompt_kernel, tq=tq),
        grid=(nb, t // tq),
        in_specs=[pl.BlockSpec((1, ATTN_WIDTH, tq), lambda b, i: (b, 0, i)),
                  whole_t(ATTN_WIDTH), whole_t(N_BIAS_ROWS), whole_f(ATTN_WIDTH), whole_f(N_HEADS)],
        out_specs=pl.BlockSpec((1, tq, ATTN_WIDTH), lambda b, i: (b, i, 0)),
        out_shape=jax.ShapeDtypeStruct((nb, t, ATTN_WIDTH), BF16),
        scratch_shapes=[pltpu.VMEM((N_HEADS, 2 * HEAD_DIM + N_BIAS_ROWS, tq), BF16),
                        pltpu.VMEM((N_HEADS, 1, tq), F32), pltpu.VMEM((N_HEADS, HEAD_DIM + SUM_ROWS, tq), F32),
                        pltpu.VMEM((N_HEADS, tq, tq), F32), pltpu.VMEM((N_HEADS, tq, tq), F32)],
        compiler_params=_params("arbitrary", "arbitrary"), name="fox_prompt")(qT, k, cp, vT, cT)


SAMPLE_SEQS_PER_STEP = 2

def _fox_sample_kernel(pt_ref, q_ref, kn_ref, vn_ref, lfn_ref, sl_ref, *refs, n_pages, nq, n_seq):
    del pt_ref
    o_ref = refs[3 * n_seq * n_pages]
    for e in range(n_seq):
        pages = [refs[(kind * n_seq + e) * n_pages:(kind * n_seq + e + 1) * n_pages] for kind in range(3)]
        _fox_sample_one(e, q_ref, kn_ref, vn_ref, lfn_ref, sl_ref, *pages, o_ref, n_pages=n_pages, nq=nq)


def _fox_sample_one(e, q_ref, kn_ref, vn_ref, lfn_ref, sl_ref, k_refs, v_refs, pf_refs, o_ref, *, n_pages, nq):
    nr = nq * N_HEADS
    head_of_lane = lax.broadcasted_iota(jnp.int32, (N_HEADS, ATTN_WIDTH), 1) // HEAD_DIM
    own = head_of_lane == lax.broadcasted_iota(jnp.int32, (N_HEADS, ATTN_WIDTH), 0)
    q = q_ref[e]
    qbd32 = jnp.concatenate([jnp.where(own, q[i:i + 1, :], 0.0) for i in range(nq)], axis=0)
    qbd = qbd32.astype(BF16)
    rep = lambda e: jnp.concatenate([e] * nq, axis=0)
    rows = lambda j: slice(j * N_HEADS, (j + 1) * N_HEADS)

    lfn = lfn_ref[e]
    e_new = rep(_dot_exact01(lfn, sl_ref[...]))
    kn, vn = kn_ref[e], vn_ref[e]
    qidx = lax.broadcasted_iota(jnp.int32, (nr, 1), 0) // N_HEADS
    s_new = [jnp.where(qidx >= j, jnp.sum(qbd32 * kn[j:j + 1, :], axis=-1, keepdims=True) + e_new[:, j:j + 1], NEG_INF)
             for j in range(nq)]

    pf_all = jnp.concatenate([r[0, 0] for r in pf_refs], axis=0)
    sfx_all = _dot_exact01(pf_all, sl_ref[...])
    tot_all = sfx_all[:, 0:1] + pf_all[:, 0:1]
    tail = jnp.sum(lfn, axis=-1, keepdims=True)
    scores = [None] * n_pages
    for j in reversed(range(n_pages)):
        kt = k_refs[j][0, 0].reshape(ATTN_WIDTH, PAGE_SIZE).astype(BF16)
        scores[j] = _dot(qbd, kt) + rep(sfx_all[rows(j)] + tail)
        tail = tail + tot_all[rows(j)]
    s_all = jnp.concatenate(scores, axis=1)
    m = jnp.max(s_all, axis=-1, keepdims=True)
    for s in s_new:
        m = jnp.maximum(m, s)
    p_all = jnp.exp(s_all - m)
    l = jnp.sum(p_all, axis=-1, keepdims=True)
    acc = jnp.zeros((nr, ATTN_WIDTH), F32)
    for j, s in enumerate(s_new):
        p = jnp.exp(s - m)
        l = l + p
        acc = acc + p * vn[j:j + 1, :]
    p_all = p_all.astype(BF16)
    for j in range(n_pages):
        vt = v_refs[j][0, 0].reshape(ATTN_WIDTH, PAGE_SIZE).astype(BF16)
        acc = acc + _dot_nt(p_all[:, j * PAGE_SIZE:(j + 1) * PAGE_SIZE], vt)
    acc = acc / l
    for i in range(nq):
        o_ref[e, i:i + 1, :] = jnp.sum(jnp.where(own, acc[i * N_HEADS:(i + 1) * N_HEADS], 0.0), axis=0, keepdims=True)


def _fox_sample(q, k_new, v_new, logf_new, cache_k, cache_v, cache_logf, page_table):
    db, nq, _ = q.shape
    n_pages = page_table.shape[1]
    kc = jnp.transpose(cache_k, (0, 2, 3, 1))
    vc = jnp.transpose(cache_v, (0, 2, 3, 1))
    pf = jnp.transpose(cache_logf, (0, 2, 1))
    lfn_t = jnp.swapaxes(jnp.pad(logf_new, ((0, 0), (0, PAGE_SIZE - nq), (0, 0))), 1, 2)
    strict_lower = (jnp.arange(PAGE_SIZE)[:, None] > jnp.arange(PAGE_SIZE)[None, :]).astype(BF16)
    ns = SAMPLE_SEQS_PER_STEP
    seq = lambda r, w: pl.BlockSpec((ns, r, w), lambda b, pt: (b, 0, 0))
    page = lambda e, j: pl.BlockSpec((1, 1, N_HEADS, HEAD_DIM, PAGE_SIZE), lambda b, pt: (0, pt[ns * b + e, j], 0, 0, 0))
    page_f = lambda e, j: pl.BlockSpec((1, 1, N_HEADS, PAGE_SIZE), lambda b, pt: (0, pt[ns * b + e, j], 0, 0))
    every = [(e, j) for e in range(ns) for j in range(n_pages)]
    in_specs = [seq(nq, ATTN_WIDTH), seq(nq, ATTN_WIDTH), seq(nq, ATTN_WIDTH), seq(N_HEADS, PAGE_SIZE),
                pl.BlockSpec((PAGE_SIZE, PAGE_SIZE), lambda b, pt: (0, 0))]
    in_specs += [page(e, j) for e, j in every] * 2 + [page_f(e, j) for e, j in every]
    return pl.pallas_call(
        functools.partial(_fox_sample_kernel, n_pages=n_pages, nq=nq, n_seq=ns),
        grid_spec=pltpu.PrefetchScalarGridSpec(
            num_scalar_prefetch=1, grid=(db // ns,), in_specs=in_specs, out_specs=seq(nq, ATTN_WIDTH)),
        out_shape=jax.ShapeDtypeStruct((db, nq, ATTN_WIDTH), F32),
        compiler_params=_params("arbitrary"), name="fox_sample")(
            page_table, q, k_new, v_new, lfn_t, strict_lower,
            *([kc[None]] * len(every)), *([vc[None]] * len(every)), *([pf[None]] * len(every)))


def _cmul(ar, ai, br, bi):
    return ar * br - ai * bi, ar * bi + ai * br


def _cpow_by_bits(ar, ai, n, nbits):
    pr = jnp.ones(jnp.broadcast_shapes(ar.shape, n.shape), F32)
    pi = jnp.zeros_like(pr)
    for b in range(nbits):
        bit = ((n >> b) & 1) == 1
        fr = jnp.where(bit, ar, 1.0)
        fi = jnp.where(bit, ai, 0.0)
        pr, pi = _cmul(pr, pi, fr, fi)
        ar, ai = _cmul(ar, ai, ar, ai)
    return pr, pi


def _ssm_prep_kernel(ldt_ref, lr_ref, li_ref, bre_ref, bim_ref, cre_ref, cim_ref,
                     toep_ref, ctl_ref, obs_ref, al_ref,
                     toeps_ref, ctlrs_ref, ctlis_ref, obsrs_ref, obsis_ref, asr_ref, asi_ref, *, chunk, small):
    lw, sw = chunk * SSM_GROUP, small * SSM_GROUP
    p = STATE_DIM
    nbits = chunk.bit_length()
    dt = jnp.exp(ldt_ref[0])
    lr_row, li_row = lr_ref[0], li_ref[0]

    def discretise(lr, li):
        mag = jnp.exp(lr * dt)
        return mag * jnp.cos(li * dt), mag * jnp.sin(li * dt)

    def column(row):
        eye = lax.broadcasted_iota(jnp.int32, (p, p), 0) == lax.broadcasted_iota(jnp.int32, (p, p), 1)
        return jnp.sum(jnp.where(eye, row, 0.0), axis=-1, keepdims=True)

    lr, li = column(lr_row), column(li_row)
    ar, ai = discretise(lr, li)
    den = lr * lr + li * li
    nr, ni = ar - 1.0, ai
    cr, ci = (nr * lr + ni * li) / den, (ni * lr - nr * li) / den
    btr = jnp.concatenate([bre_ref[0]] * (2 * lw // LANES), axis=1)
    bti = jnp.concatenate([bim_ref[0]] * (2 * lw // LANES), axis=1)
    bbr, bbi = _cmul(cr, ci, btr, bti)
    sig = lax.broadcasted_iota(jnp.int32, (1, 2 * lw), 1) // SSM_GROUP
    er, ei = _cpow_by_bits(ar, ai, jnp.maximum(chunk - 1 - sig, 0), nbits)
    hr, hi = _cmul(er, ei, bbr, bbi)
    hr = jnp.where(sig < chunk, hr, 0.0)
    hi = jnp.where(sig < chunk, hi, 0.0)
    ctl_ref[0] = jnp.concatenate([hr[:, :lw], hi[:, :lw]], axis=0).astype(BF16)
    ctlrs_ref[0] = hr[:, lw - sw:lw]
    ctlis_ref[0] = hi[:, lw - sw:lw]
    c_re, c_im = cre_ref[0], cim_ref[0]
    hp = lax.Precision.HIGHEST
    k_all = (jnp.dot(c_re, hr, precision=hp, preferred_element_type=F32)
             - jnp.dot(c_im, hi, precision=hp, preferred_element_type=F32))
    for t in range(chunk):
        off = (chunk - 1 - t) * SSM_GROUP
        toep_ref[0, t * SSM_GROUP:(t + 1) * SSM_GROUP, :] = k_all[:, off:off + lw].astype(BF16)
        if t < small:
            toeps_ref[0, t * SSM_GROUP:(t + 1) * SSM_GROUP, :] = k_all[:, off:off + sw]

    arr, air = discretise(lr_row, li_row)
    tp1 = lax.broadcasted_iota(jnp.int32, (lw, 1), 0) // SSM_GROUP + 1
    pr, pi = _cpow_by_bits(arr, air, tp1, nbits)
    c_r = jnp.concatenate([c_re] * chunk, axis=0)
    c_i = jnp.concatenate([c_im] * chunk, axis=0)
    obs_re = c_r * pr - c_i * pi
    obs_im = -(c_r * pi + c_i * pr)
    obs_ref[0] = jnp.concatenate([obs_re, obs_im], axis=1).astype(BF16)
    obsrs_ref[0] = obs_re[:sw]
    obsis_ref[0] = obs_im[:sw]
    n = 1
    while n < chunk:
        arr, air = _cmul(arr, air, arr, air)
        n *= 2
        if n == small:
            asr_ref[0] = arr
            asi_ref[0] = air
    al_ref[0] = jnp.concatenate([jnp.concatenate([arr, arr], axis=1), jnp.concatenate([-air, air], axis=1)], axis=0)


def _ssm_prep(lam_re, lam_im, log_dt, b_re, b_im, c_re, c_im, *, chunk, small):
    assert chunk & (chunk - 1) == 0 and small & (small - 1) == 0 and 1 < small < chunk
    g, p, j = N_GROUPS, STATE_DIM, SSM_GROUP
    lw, sw = chunk * j, small * j
    lane_tile = lambda b: jnp.tile(b, (1, 1, LANES // j))
    ins = [log_dt.reshape(g, 1, 1), lam_re.reshape(g, 1, p), lam_im.reshape(g, 1, p), lane_tile(b_re), lane_tile(b_im),
           c_re, c_im]
    grp = lambda a, b: pl.BlockSpec((1, a, b), lambda i: (i, 0, 0))
    in_specs = [grp(1, 1), grp(1, p), grp(1, p), grp(p, LANES), grp(p, LANES), grp(j, p), grp(j, p)]
    shapes = [((lw, lw), BF16), ((2 * p, lw), BF16), ((lw, 2 * p), BF16), ((2, 2 * p), F32),
              ((sw, sw), F32), ((p, sw), F32), ((p, sw), F32), ((sw, p), F32), ((sw, p), F32), ((1, p), F32), ((1, p), F32)]
    outs = pl.pallas_call(
        functools.partial(_ssm_prep_kernel, chunk=chunk, small=small), grid=(g,), in_specs=in_specs,
        out_specs=[grp(*s) for s, _ in shapes],
        out_shape=[jax.ShapeDtypeStruct((g,) + s, d) for s, d in shapes],
        compiler_params=_params("arbitrary"), name="ssm_prep")(*ins)
    return outs[:4], outs[4:]


def _ssm_step_kernel(u_ref, toep_ref, ctlr_ref, ctli_ref, obsr_ref, obsi_ref, ar_ref, ai_ref, h0r_ref, h0i_ref,
                     y_ref, hr_ref, hi_ref):
    u = u_ref[0]
    h0r, h0i = h0r_ref[0], h0i_ref[0]
    er, ei = _cmul(ar_ref[0], ai_ref[0], h0r, h0i)
    hr_ref[0] = _dot_nt(u, ctlr_ref[0].astype(BF16)) + er
    hi_ref[0] = _dot_nt(u, ctli_ref[0].astype(BF16)) + ei
    y_ref[0] = (_dot_nt(u, toep_ref[0].astype(BF16)) + _dot_nt(h0r.astype(BF16), obsr_ref[0].astype(BF16))
                + _dot_nt(h0i.astype(BF16), obsi_ref[0].astype(BF16)))


def _ssm_step(u, ops, h0):
    n_seq, chunk, _ = u.shape
    g, p, j = N_GROUPS, STATE_DIM, SSM_GROUP
    lw = chunk * j
    ug = u.reshape(n_seq, chunk, g, j).transpose(2, 0, 1, 3).reshape(g, n_seq, lw).astype(BF16)
    grp = lambda a, b: pl.BlockSpec((1, a, b), lambda i: (i, 0, 0))
    y, hr, hi = pl.pallas_call(
        _ssm_step_kernel, grid=(g,),
        in_specs=[grp(n_seq, lw), grp(lw, lw), grp(p, lw), grp(p, lw), grp(lw, p), grp(lw, p), grp(1, p), grp(1, p),
                  grp(n_seq, p), grp(n_seq, p)],
        out_specs=[grp(n_seq, lw), grp(n_seq, p), grp(n_seq, p)],
        out_shape=[jax.ShapeDtypeStruct((g, n_seq, lw), F32), jax.ShapeDtypeStruct((g, n_seq, p), F32),
                   jax.ShapeDtypeStruct((g, n_seq, p), F32)],
        compiler_params=_params("arbitrary"), name="ssm_step")(
            ug, *ops, jnp.swapaxes(h0[0], 0, 1), jnp.swapaxes(h0[1], 0, 1))
    y = y.reshape(g, n_seq, chunk, j).transpose(1, 2, 0, 3).reshape(n_seq, chunk, SSM_WIDTH)
    return y, jnp.swapaxes(hr, 0, 1), jnp.swapaxes(hi, 0, 1)


GROUPS_PER_TILE = LANES // SSM_GROUP
GROUPS_PER_TRIP = 8


def _block_transpose(xs):
    n = len(xs)
    blk = lax.broadcasted_iota(jnp.int32, xs[0].shape, 1) // SSM_GROUP
    xs = list(xs)
    d = n // 2
    while d:
        upper = (blk & d) != 0
        for i in range(n):
            if not i & d:
                lo, hi = xs[i], xs[i + d]
                xs[i] = jnp.where(upper, pltpu.roll(hi, SSM_GROUP * d, axis=1), lo)
                xs[i + d] = jnp.where(upper, hi, pltpu.roll(lo, LANES - SSM_GROUP * d, axis=1))
        d //= 2
    return xs


def _ssm_seq_kernel(u_ref, toep_ref, ctl_ref, obs_ref, al_ref, y_ref, h_ref, ug_ref, yg_ref, *, chunk):
    t = u_ref.shape[2]
    rows = t // chunk
    lw = chunk * SSM_GROUP
    n = GROUPS_PER_TILE
    p = STATE_DIM
    ridx = lax.broadcasted_iota(jnp.int32, (rows, 1), 0)

    def shifted(x, d):
        return jnp.where(ridx >= d, pltpu.roll(x, d, axis=0), 0.0)

    def group_tile(v, carry):
        for w in range(lw // LANES):
            xs = [u_ref[0, v, pl.ds(n * w + k, rows, stride=chunk), :] for k in range(n)]
            for k, x in enumerate(_block_transpose(xs)):
                ug_ref[k, :, w * LANES:(w + 1) * LANES] = x.astype(BF16)

        def groups(i, c):
            ks = [i * GROUPS_PER_TRIP + e for e in range(GROUPS_PER_TRIP)]
            gs = [v * n + k for k in ks]
            ugs = [ug_ref[k] for k in ks]
            ys = [_dot_nt(ug, toep_ref[g]) for ug, g in zip(ugs, gs)]
            hs = [_dot_nt(ug, ctl_ref[g]) for ug, g in zip(ugs, gs)]
            ars = [al_ref[g][0:1, :] for g in gs]
            ais = [al_ref[g][1:2, :] for g in gs]
            d = 1
            while d < rows:
                for e in range(GROUPS_PER_TRIP):
                    sh = shifted(hs[e], d)
                    hs[e] = hs[e] + ars[e] * sh + ais[e] * pltpu.roll(sh, p, axis=1)
                    ars[e], ais[e] = ars[e] * ars[e] - ais[e] * ais[e], 2.0 * ars[e] * ais[e]
                d *= 2
            for e, (k, g) in enumerate(zip(ks, gs)):
                h_ref[0, pl.ds(g, 1), :] = hs[e][rows - 1:rows, :]
                yg_ref[k] = ys[e] + _dot_nt(shifted(hs[e], 1).astype(BF16), obs_ref[g])
            return c

        lax.fori_loop(0, n // GROUPS_PER_TRIP, groups, 0)
        for w in range(lw // LANES):
            ys = [yg_ref[k, :, w * LANES:(w + 1) * LANES] for k in range(n)]
            for k, y in enumerate(_block_transpose(ys)):
                y_ref[0, v, pl.ds(n * w + k, rows, stride=chunk), :] = y
        return carry

    lax.fori_loop(0, N_GROUPS // n, group_tile, 0)


def _ssm_seq(u, ops, *, chunk):
    n_seq, _, t, _ = u.shape
    g, p = N_GROUPS, STATE_DIM
    lw = chunk * SSM_GROUP
    assert lw % LANES == 0 and chunk == GROUPS_PER_TILE * (lw // LANES) and SSM_WIDTH == g * SSM_GROUP
    whole = lambda a: pl.BlockSpec(a.shape, lambda b: (0, 0, 0), pipeline_mode=pl.Buffered(1))
    seq = pl.BlockSpec((1, SSM_WIDTH // LANES, t, LANES), lambda b: (b, 0, 0, 0))
    y, h = pl.pallas_call(
        functools.partial(_ssm_seq_kernel, chunk=chunk), grid=(n_seq,),
        in_specs=[seq] + [whole(o) for o in ops],
        out_specs=[seq, pl.BlockSpec((1, g, 2 * p), lambda b: (b, 0, 0))],
        out_shape=[jax.ShapeDtypeStruct(u.shape, F32), jax.ShapeDtypeStruct((n_seq, g, 2 * p), F32)],
        scratch_shapes=[pltpu.VMEM((GROUPS_PER_TILE, t // chunk, lw), BF16),
                        pltpu.VMEM((GROUPS_PER_TILE, t // chunk, lw), F32)],
        compiler_params=_params("arbitrary"), name="ssm_seq")(u, *ops)
    return y, h[:, :, :p], h[:, :, p:]


FF_CHUNK = 1024


def _post_kernel(x_ref, att_ref, ys_ref, u_ref, p_ref, lng_ref, lnb_ref, dsk_ref, wglu_ref, bglu_ref,
                 woa_ref, wos_ref, l1g_ref, l1b_ref, wup_ref, wdn_ref, wpe_ref, wpg_ref, bpg_ref,
                 l2g_ref, l2b_ref, o_ref):
    h = _layer_norm(x_ref[0], lng_ref[...], lnb_ref[...])
    y = _gelu_tanh(_load_lane_tiles(ys_ref) + dsk_ref[...] * _load_lane_tiles(u_ref))
    y = y * _sigmoid(_dot(y.astype(BF16), wglu_ref[...]) + bglu_ref[...])
    mix = _dot(att_ref[0], woa_ref[...]) + _dot(y.astype(BF16), wos_ref[...])
    h1 = _layer_norm(ALPHA * h + mix, l1g_ref[...], l1b_ref[...])
    h1b = h1.astype(BF16)
    e = _sigmoid(_dot(h1b, wpg_ref[...]) + bpg_ref[...]) * _dot(p_ref[0].astype(BF16), wpe_ref[...])
    acc = ALPHA * h1 + e
    for c in range(D_FF // FF_CHUNK):
        cs = slice(c * FF_CHUNK, (c + 1) * FF_CHUNK)
        a = jnp.maximum(_dot(h1b, wup_ref[:, cs]), 0.0)
        acc = acc + _dot((a * a).astype(BF16), wdn_ref[cs, :])
    o_ref[0] = _layer_norm(acc, l2g_ref[...], l2b_ref[...])


def _post_mixer(x, att, ys, u, p, weights, *, tm):
    nb, t, _ = x.shape
    tok = lambda w: pl.BlockSpec((1, tm, w), lambda b, i: (b, i, 0))
    const = lambda a: pl.BlockSpec(a.shape, lambda b, i: (0, 0), pipeline_mode=pl.Buffered(1))
    return pl.pallas_call(
        _post_kernel, grid=(nb, t // tm),
        in_specs=[tok(D_MODEL), tok(ATTN_WIDTH), _lane_tiles_spec(tm, SSM_WIDTH), _lane_tiles_spec(tm, SSM_WIDTH),
                  tok(PLE_DIM)] + [const(w) for w in weights],
        out_specs=tok(D_MODEL), out_shape=jax.ShapeDtypeStruct((nb, t, D_MODEL), F32),
        compiler_params=_params("arbitrary", "arbitrary"), name="post_mixer")(x, att, ys, u, p, *weights)


def kernel(x_prompt, x_sample, cache_k, cache_v, cache_logf, state_re, state_im, page_table, p_prompt, p_sample,
           ln_in_g, ln_in_b, w_in, b_f, lam_re, lam_im, log_dt, b_re, b_im, c_re, c_im, d_skip, w_glu, b_glu,
           w_out, ln1_g, ln1_b, w_up, w_down, w_pe, w_pg, b_pg, ln2_g, ln2_b):
    assert w_in.shape[0] == 1, "one trunk layer"
    nb, t, _ = x_prompt.shape
    db, nq, _ = x_sample.shape
    row = lambda a: a.reshape(1, -1)
    post_w = [row(ln_in_g), row(ln_in_b), row(d_skip[0]), w_glu[0].astype(BF16), row(b_glu[0]),
              w_out[0, :ATTN_WIDTH].astype(BF16), w_out[0, ATTN_WIDTH:].astype(BF16), row(ln1_g[0]), row(ln1_b[0]),
              w_up[0].astype(BF16), w_down[0].astype(BF16), w_pe[0].astype(BF16), w_pg[0].astype(BF16),
              row(b_pg[0]), row(ln2_g[0]), row(ln2_b[0])]
    ssm_par = (lam_re[0], lam_im[0], log_dt[0], b_re[0], b_im[0], c_re[0], c_im[0])

    u, kT, vT, lfT, cT, qTb, kb, vTb, cp = _in_proj(x_prompt, ln_in_g, ln_in_b, w_in[0], b_f[0],
                                                     tm=IN_PROJ_ROWS, q_scale=QK_SCALE * LOG2E, attn_layouts=True)
    att = _fox_prompt(qTb, kb, cp, vTb, cT, tq=ATTN_TILE)
    ssm_ops, ssm_ops_sample = _ssm_prep(*ssm_par, chunk=SSM_CHUNK, small=nq)
    ys, sr, si = _ssm_seq(u, ssm_ops, chunk=SSM_CHUNK)
    y_prompt = _post_mixer(x_prompt, att, ys, u, p_prompt[0], post_w, tm=POST_ROWS)
    heads_last = lambda a: jnp.transpose(a.reshape(nb, N_HEADS, HEAD_DIM, t), (0, 3, 1, 2))[None]
    prompt_out = (heads_last(kT), heads_last(vT), jnp.swapaxes(lfT, 1, 2)[None], sr[None], si[None])

    us, qs, ks, vs, logfs = _in_proj(x_sample.reshape(1, db * nq, D_MODEL), ln_in_g, ln_in_b, w_in[0], b_f[0],
                                     tm=db * nq, q_scale=QK_SCALE, attn_layouts=False)
    seq = lambda a: a.reshape(db, nq, a.shape[-1])
    to_tiles = lambda a: jnp.swapaxes(a.reshape(1, db * nq, SSM_WIDTH // LANES, LANES), 1, 2)
    from_tiles = lambda a: jnp.swapaxes(a, 1, 2).reshape(db, nq, SSM_WIDTH)
    att_s = _fox_sample(seq(qs), seq(ks), seq(vs), seq(logfs), cache_k[0], cache_v[0], cache_logf[0], page_table)
    ys_s, sr_s, si_s = _ssm_step(from_tiles(us), ssm_ops_sample, (state_re[0], state_im[0]))
    flat = lambda a: a.reshape(1, db * nq, a.shape[-1])
    y_sample = _post_mixer(flat(x_sample), flat(att_s).astype(BF16), to_tiles(ys_s), us, flat(p_sample[0]),
                           post_w, tm=db * nq).reshape(db, nq, D_MODEL)
    sample_out = (ks.reshape(1, db, nq, N_HEADS, HEAD_DIM), vs.reshape(1, db, nq, N_HEADS, HEAD_DIM),
                  logfs.reshape(1, db, nq, N_HEADS), sr_s[None], si_s[None])
    return (y_prompt, y_sample) + prompt_out + sample_out
```

```python
import functools
import math

import jax
import jax.numpy as jnp
from jax import lax
from jax.experimental import pallas as pl
from jax.experimental.pallas import tpu as pltpu

F32 = jnp.float32
BF16 = jnp.bfloat16

D_MODEL = 1024
ATTN_WIDTH = 512
SSM_WIDTH = 512
HEAD_DIM = 64
N_HEADS = 8
SSM_GROUP = 16
N_GROUPS = 32
STATE_DIM = 64
D_FF = 4096
PLE_DIM = 256
PAGE_SIZE = 128
ALPHA = 2.0 ** 0.25
LN_EPS = 1e-5
NEG_INF = -1e30
QK_SCALE = HEAD_DIM ** -0.5
LOG2E = math.log2(math.e)

VMEM_LIMIT_BYTES = 56 * 1024 * 1024

IN_PROJ_ROWS = 512
ATTN_TILE = 256
SSM_CHUNK = 16
POST_ROWS = 512

_NT = (((1,), (1,)), ((), ()))


def _params(*sem):
    return pltpu.CompilerParams(dimension_semantics=sem, vmem_limit_bytes=VMEM_LIMIT_BYTES)


def _const_spec(shape):
    return pl.BlockSpec(shape, lambda *_: (0,) * len(shape))


def _layer_norm(x, g, b):
    mu = jnp.mean(x, axis=-1, keepdims=True)
    xc = x - mu
    var = jnp.mean(xc * xc, axis=-1, keepdims=True)
    return xc * lax.rsqrt(var + LN_EPS) * g + b


def _log_sigmoid(x):
    return jnp.minimum(x, 0.0) - jnp.log1p(jnp.exp(-jnp.abs(x)))


def _sigmoid(x):
    return 1.0 / (1.0 + jnp.exp(-x))


def _gelu_tanh(x):
    return 0.5 * x * (1.0 + jnp.tanh(math.sqrt(2.0 / math.pi) * (x + 0.044715 * (x * x * x))))


def _dot(a, b):
    return jnp.dot(a, b, preferred_element_type=F32)


def _dot_nt(a, b):
    return lax.dot_general(a, b, _NT, preferred_element_type=F32)


def _split3(x):
    hi = x.astype(BF16)
    r1 = x - hi.astype(F32)
    mid = r1.astype(BF16)
    lo = (r1 - mid.astype(F32)).astype(BF16)
    return hi, mid, lo


LANES = 128


def _lane_tiles_spec(tm, width):
    return pl.BlockSpec((1, width // LANES, tm, LANES), lambda b, i: (b, 0, i, 0))


def _store_lane_tiles(ref, x):
    for v in range(x.shape[-1] // LANES):
        ref[0, v] = x[:, v * LANES:(v + 1) * LANES]


def _load_lane_tiles(ref):
    return jnp.concatenate([ref[0, v] for v in range(ref.shape[1])], axis=-1)


def _dot_exact01(x, m01):
    hi, mid, lo = _split3(x)
    return _dot(hi, m01) + _dot(mid, m01) + _dot(lo, m01)


N_BIAS_ROWS = 128
ONES_ROW = 3 * N_HEADS
SUM_ROWS = 16


def _in_proj_kernel(*refs, tm, q_scale, attn_layouts):
    x_ref, g_ref, b_ref, wu_ref = refs[:4]
    if attn_layouts:
        (wqT_ref, wkT_ref, wvT_ref, wfT_ref, bfc_ref, tri_ref,
         u_ref, kT_ref, vT_ref, lfT_ref, cT_ref, qTb_ref, kb_ref, vTb_ref, cp_ref, carry_ref) = refs[4:]
    else:
        wq_ref, wk_ref, wv_ref, wf_ref, bf_ref, u_ref, q_ref, k_ref, v_ref, lf_ref = refs[4:]
    hb = _layer_norm(x_ref[0], g_ref[...], b_ref[...]).astype(BF16)
    _store_lane_tiles(u_ref, _dot(hb, wu_ref[...]))
    if attn_layouts:
        qTb_ref[0] = (_dot_nt(wqT_ref[...], hb) * q_scale).astype(BF16)
        kT = _dot_nt(wkT_ref[...], hb)
        kT_ref[0] = kT
        kb_ref[0] = kT.T.astype(BF16)
        vT = _dot_nt(wvT_ref[...], hb)
        vT_ref[0] = vT
        vTb_ref[0] = vT.astype(BF16)
        lfT = _log_sigmoid(_dot_nt(wfT_ref[...], hb) + bfc_ref[...])
        lfT_ref[0] = lfT

        @pl.when(pl.program_id(1) == 0)
        def _():
            carry_ref[...] = jnp.zeros_like(carry_ref)

        c = carry_ref[...] + _dot_exact01(lfT, tri_ref[...])
        cT_ref[0] = c
        carry_ref[...] = c[:, tm - 1:tm]
        hi, mid, lo = _split3(c * LOG2E)
        row = lax.broadcasted_iota(jnp.int32, (N_BIAS_ROWS - ONES_ROW, tm), 0)
        ones = jnp.where(row < 3, 1.0, 0.0)
        pieces = jnp.concatenate([hi.astype(F32), mid.astype(F32), lo.astype(F32), ones], axis=0)
        cp_ref[0] = pieces.T.astype(BF16)
    else:
        q_ref[0] = _dot(hb, wq_ref[...]) * q_scale
        k_ref[0] = _dot(hb, wk_ref[...])
        v_ref[0] = _dot(hb, wv_ref[...])
        lf_ref[0] = _log_sigmoid(_dot(hb, wf_ref[...]) + bf_ref[...])


def _in_proj(x3, ln_g, ln_b, w_in, b_f, *, tm, q_scale, attn_layouts):
    nb, t, _ = x3.shape
    wq = w_in[:, :ATTN_WIDTH].astype(BF16)
    wk = w_in[:, ATTN_WIDTH:2 * ATTN_WIDTH].astype(BF16)
    wv = w_in[:, 2 * ATTN_WIDTH:3 * ATTN_WIDTH].astype(BF16)
    wf = w_in[:, 3 * ATTN_WIDTH:3 * ATTN_WIDTH + N_HEADS].astype(BF16)
    wu = w_in[:, 3 * ATTN_WIDTH + N_HEADS:].astype(BF16)
    tok = lambda w: pl.BlockSpec((1, tm, w), lambda b, i: (b, i, 0))
    feat = lambda w: pl.BlockSpec((1, w, tm), lambda b, i: (b, 0, i))
    sds = jax.ShapeDtypeStruct
    ins = [x3, ln_g.reshape(1, -1), ln_b.reshape(1, -1), wu]
    in_specs = [tok(D_MODEL), _const_spec((1, D_MODEL)), _const_spec((1, D_MODEL)), _const_spec((D_MODEL, SSM_WIDTH))]
    out_shape = [sds((nb, SSM_WIDTH // LANES, t, LANES), F32)]
    out_specs = [_lane_tiles_spec(tm, SSM_WIDTH)]
    scratch = []
    if attn_layouts:
        tri = (jnp.arange(tm)[:, None] <= jnp.arange(tm)[None, :]).astype(BF16)
        ins += [wq.T, wk.T, wv.T, wf.T, b_f.reshape(N_HEADS, 1), tri]
        in_specs += [_const_spec((ATTN_WIDTH, D_MODEL))] * 3
        in_specs += [_const_spec((N_HEADS, D_MODEL)), _const_spec((N_HEADS, 1)), _const_spec((tm, tm))]
        out_shape += [sds((nb, ATTN_WIDTH, t), F32), sds((nb, ATTN_WIDTH, t), F32),
                      sds((nb, N_HEADS, t), F32), sds((nb, N_HEADS, t), F32),
                      sds((nb, ATTN_WIDTH, t), BF16), sds((nb, t, ATTN_WIDTH), BF16), sds((nb, ATTN_WIDTH, t), BF16),
                      sds((nb, t, N_BIAS_ROWS), BF16)]
        out_specs += [feat(ATTN_WIDTH), feat(ATTN_WIDTH), feat(N_HEADS), feat(N_HEADS),
                      feat(ATTN_WIDTH), tok(ATTN_WIDTH), feat(ATTN_WIDTH), tok(N_BIAS_ROWS)]
        scratch = [pltpu.VMEM((N_HEADS, 1), F32)]
    else:
        ins += [wq, wk, wv, wf, b_f.reshape(1, N_HEADS)]
        in_specs += [_const_spec((D_MODEL, ATTN_WIDTH))] * 3 + [_const_spec((D_MODEL, N_HEADS)), _const_spec((1, N_HEADS))]
        out_shape += [sds((nb, t, ATTN_WIDTH), F32)] * 3 + [sds((nb, t, N_HEADS), F32)]
        out_specs += [tok(ATTN_WIDTH)] * 3 + [tok(N_HEADS)]
    return pl.pallas_call(
        functools.partial(_in_proj_kernel, tm=tm, q_scale=q_scale, attn_layouts=attn_layouts),
        grid=(nb, t // tm), in_specs=in_specs, out_specs=out_specs, out_shape=out_shape,
        scratch_shapes=scratch, compiler_params=_params("arbitrary", "arbitrary"),
        name="in_proj_prompt" if attn_layouts else "in_proj_sample")(*ins)


def _fox_prompt_kernel(qT_ref, k_ref, cp_ref, vT_ref, cT_ref, o_ref, bq_ref, m_ref, acc_ref, s0_ref, s1_ref, *, tq):
    qi = pl.program_id(1)
    q0 = pl.multiple_of(qi * tq, tq)
    heads = [slice(h * HEAD_DIM, (h + 1) * HEAD_DIM) for h in range(N_HEADS)]
    pairs = [slice(2 * HEAD_DIM * (h // 2), 2 * HEAD_DIM * (h // 2 + 1)) for h in range(N_HEADS)]

    c0 = cT_ref[0, :, pl.ds(q0, tq)][:, 0:1] * LOG2E
    c0_pieces = [p.astype(F32) for p in _split3(c0)]
    r = lax.broadcasted_iota(jnp.int32, (2 * HEAD_DIM, tq), 0)
    for h in range(N_HEADS):
        mine = (r < HEAD_DIM) if h % 2 == 0 else (r >= HEAD_DIM)
        bq_ref[h, 0:2 * HEAD_DIM, :] = jnp.where(mine, qT_ref[0, pairs[h], :], jnp.zeros((), BF16))
        sel = jnp.where((r == h) | (r == N_HEADS + h) | (r == 2 * N_HEADS + h), -1.0, 0.0)
        for j, piece in enumerate(c0_pieces):
            sel = jnp.where(r == ONES_ROW + j, piece[h:h + 1, :], sel)
        bq_ref[h, 2 * HEAD_DIM:, :] = sel.astype(BF16)

    key_row = lax.broadcasted_iota(jnp.int32, (tq, tq), 0)
    qry_col = lax.broadcasted_iota(jnp.int32, (tq, tq), 1)

    m_ref[...] = jnp.full_like(m_ref, NEG_INF)
    acc_ref[...] = jnp.zeros_like(acc_ref)
    ones = jnp.ones((SUM_ROWS, tq), BF16)

    s_refs = (s0_ref, s1_ref)

    def scores(kt, h, buf):
        ks = pl.multiple_of(kt * tq, tq)
        a = jnp.concatenate([k_ref[0, pl.ds(ks, tq), pairs[h]], cp_ref[0, pl.ds(ks, tq), :]], axis=1)
        s_refs[buf][h] = _dot(a, bq_ref[h])

    def accumulate(kt, h, buf, diagonal):
        ks = pl.multiple_of(kt * tq, tq)
        s = s_refs[buf][h]
        if diagonal:
            s = jnp.where(key_row <= qry_col, s, NEG_INF)
        m_new = jnp.maximum(m_ref[h], jnp.max(s, axis=0, keepdims=True))
        alpha = jnp.exp2(m_ref[h] - m_new)
        m_ref[h] = m_new
        p = jnp.exp2((s - m_new).astype(BF16))
        v1 = jnp.concatenate([vT_ref[0, heads[h], pl.ds(ks, tq)], ones], axis=0)
        acc_ref[h] = alpha * acc_ref[h] + _dot(v1, p)

    def stage(kt, cur, *, diagonal=False, issue_next=True):
        for h in range(N_HEADS + 1):
            if issue_next and h < N_HEADS:
                scores(kt + 1, h, 1 - cur)
            if h >= 1:
                accumulate(kt, h - 1, cur, diagonal)

    for h in range(N_HEADS):
        scores(0, h, 0)

    def body(j, carry):
        stage(2 * j, 0)
        stage(2 * j + 1, 1)
        return carry

    lax.fori_loop(0, qi // 2, body, 0)

    @pl.when(qi % 2 == 0)
    def _():
        stage(qi, 0, diagonal=True, issue_next=False)

    @pl.when(qi % 2 == 1)
    def _():
        stage(qi - 1, 0)
        stage(qi, 1, diagonal=True, issue_next=False)

    outs = [acc_ref[h, 0:HEAD_DIM, :] / acc_ref[h, HEAD_DIM:HEAD_DIM + 1, :] for h in range(N_HEADS)]
    o_ref[0] = jnp.concatenate(outs, axis=0).T.astype(o_ref.dtype)


def _fox_prompt(qT, k, cp, vT, cT, *, tq):
    nb, t, _ = k.shape
    assert N_BIAS_ROWS == 2 * HEAD_DIM
    whole_f = lambda w: pl.BlockSpec((1, w, t), lambda b, i: (b, 0, 0))
    whole_t = lambda w: pl.BlockSpec((1, t, w), lambda b, i: (b, 0, 0))
    return pl.pallas_call(
        functools.partial(_fox_prompt_kernel, tq=tq),
        grid=(nb, t // tq),
        in_specs=[pl.BlockSpec((1, ATTN_WIDTH, tq), lambda b, i: (b, 0, i)),
                  whole_t(ATTN_WIDTH), whole_t(N_BIAS_ROWS), whole_f(ATTN_WIDTH), whole_f(N_HEADS)],
        out_specs=pl.BlockSpec((1, tq, ATTN_WIDTH), lambda b, i: (b, i, 0)),
        out_shape=jax.ShapeDtypeStruct((nb, t, ATTN_WIDTH), BF16),
        scratch_shapes=[pltpu.VMEM((N_HEADS, 2 * HEAD_DIM + N_BIAS_ROWS, tq), BF16),
                        pltpu.VMEM((N_HEADS, 1, tq), F32), pltpu.VMEM((N_HEADS, HEAD_DIM + SUM_ROWS, tq), F32),
                        pltpu.VMEM((N_HEADS, tq, tq), F32), pltpu.VMEM((N_HEADS, tq, tq), F32)],
        compiler_params=_params("arbitrary", "arbitrary"), name="fox_prompt")(qT, k, cp, vT, cT)


SAMPLE_SEQS_PER_STEP = 2

def _fox_sample_kernel(pt_ref, q_ref, kn_ref, vn_ref, lfn_ref, sl_ref, *refs, n_pages, nq, n_seq):
    del pt_ref
    o_ref = refs[3 * n_seq * n_pages]
    for e in range(n_seq):
        pages = [refs[(kind * n_seq + e) * n_pages:(kind * n_seq + e + 1) * n_pages] for kind in range(3)]
        _fox_sample_one(e, q_ref, kn_ref, vn_ref, lfn_ref, sl_ref, *pages, o_ref, n_pages=n_pages, nq=nq)


def _fox_sample_one(e, q_ref, kn_ref, vn_ref, lfn_ref, sl_ref, k_refs, v_refs, pf_refs, o_ref, *, n_pages, nq):
    nr = nq * N_HEADS
    head_of_lane = lax.broadcasted_iota(jnp.int32, (N_HEADS, ATTN_WIDTH), 1) // HEAD_DIM
    own = head_of_lane == lax.broadcasted_iota(jnp.int32, (N_HEADS, ATTN_WIDTH), 0)
    q = q_ref[e]
    qbd32 = jnp.concatenate([jnp.where(own, q[i:i + 1, :], 0.0) for i in range(nq)], axis=0)
    qbd = qbd32.astype(BF16)
    rep = lambda e: jnp.concatenate([e] * nq, axis=0)
    rows = lambda j: slice(j * N_HEADS, (j + 1) * N_HEADS)

    lfn = lfn_ref[e]
    e_new = rep(_dot_exact01(lfn, sl_ref[...]))
    kn, vn = kn_ref[e], vn_ref[e]
    qidx = lax.broadcasted_iota(jnp.int32, (nr, 1), 0) // N_HEADS
    s_new = [jnp.where(qidx >= j, jnp.sum(qbd32 * kn[j:j + 1, :], axis=-1, keepdims=True) + e_new[:, j:j + 1], NEG_INF)
             for j in range(nq)]

    pf_all = jnp.concatenate([r[0, 0] for r in pf_refs], axis=0)
    sfx_all = _dot_exact01(pf_all, sl_ref[...])
    tot_all = sfx_all[:, 0:1] + pf_all[:, 0:1]
    tail = jnp.sum(lfn, axis=-1, keepdims=True)
    scores = [None] * n_pages
    for j in reversed(range(n_pages)):
        kt = k_refs[j][0, 0].reshape(ATTN_WIDTH, PAGE_SIZE).astype(BF16)
        scores[j] = _dot(qbd, kt) + rep(sfx_all[rows(j)] + tail)
        tail = tail + tot_all[rows(j)]
    s_all = jnp.concatenate(scores, axis=1)
    m = jnp.max(s_all, axis=-1, keepdims=True)
    for s in s_new:
        m = jnp.maximum(m, s)
    p_all = jnp.exp(s_all - m)
    l = jnp.sum(p_all, axis=-1, keepdims=True)
    acc = jnp.zeros((nr, ATTN_WIDTH), F32)
    for j, s in enumerate(s_new):
        p = jnp.exp(s - m)
        l = l + p
        acc = acc + p * vn[j:j + 1, :]
    p_all = p_all.astype(BF16)
    for j in range(n_pages):
        vt = v_refs[j][0, 0].reshape(ATTN_WIDTH, PAGE_SIZE).astype(BF16)
        acc = acc + _dot_nt(p_all[:, j * PAGE_SIZE:(j + 1) * PAGE_SIZE], vt)
    acc = acc / l
    for i in range(nq):
        o_ref[e, i:i + 1, :] = jnp.sum(jnp.where(own, acc[i * N_HEADS:(i + 1) * N_HEADS], 0.0), axis=0, keepdims=True)


def _fox_sample(q, k_new, v_new, logf_new, cache_k, cache_v, cache_logf, page_table):
    db, nq, _ = q.shape
    n_pages = page_table.shape[1]
    kc = jnp.transpose(cache_k, (0, 2, 3, 1))
    vc = jnp.transpose(cache_v, (0, 2, 3, 1))
    pf = jnp.transpose(cache_logf, (0, 2, 1))
    lfn_t = jnp.swapaxes(jnp.pad(logf_new, ((0, 0), (0, PAGE_SIZE - nq), (0, 0))), 1, 2)
    strict_lower = (jnp.arange(PAGE_SIZE)[:, None] > jnp.arange(PAGE_SIZE)[None, :]).astype(BF16)
    ns = SAMPLE_SEQS_PER_STEP
    seq = lambda r, w: pl.BlockSpec((ns, r, w), lambda b, pt: (b, 0, 0))
    page = lambda e, j: pl.BlockSpec((1, 1, N_HEADS, HEAD_DIM, PAGE_SIZE), lambda b, pt: (0, pt[ns * b + e, j], 0, 0, 0))
    page_f = lambda e, j: pl.BlockSpec((1, 1, N_HEADS, PAGE_SIZE), lambda b, pt: (0, pt[ns * b + e, j], 0, 0))
    every = [(e, j) for e in range(ns) for j in range(n_pages)]
    in_specs = [seq(nq, ATTN_WIDTH), seq(nq, ATTN_WIDTH), seq(nq, ATTN_WIDTH), seq(N_HEADS, PAGE_SIZE),
                pl.BlockSpec((PAGE_SIZE, PAGE_SIZE), lambda b, pt: (0, 0))]
    in_specs += [page(e, j) for e, j in every] * 2 + [page_f(e, j) for e, j in every]
    return pl.pallas_call(
        functools.partial(_fox_sample_kernel, n_pages=n_pages, nq=nq, n_seq=ns),
        grid_spec=pltpu.PrefetchScalarGridSpec(
            num_scalar_prefetch=1, grid=(db // ns,), in_specs=in_specs, out_specs=seq(nq, ATTN_WIDTH)),
        out_shape=jax.ShapeDtypeStruct((db, nq, ATTN_WIDTH), F32),
        compiler_params=_params("arbitrary"), name="fox_sample")(
            page_table, q, k_new, v_new, lfn_t, strict_lower,
            *([kc[None]] * len(every)), *([vc[None]] * len(every)), *([pf[None]] * len(every)))


PREP_GROUPS = 2
STEP_GROUPS = 8

def _cmul(ar, ai, br, bi):
    return ar * br - ai * bi, ar * bi + ai * br


def _cpow_by_bits(ar, ai, n, nbits):
    pr = jnp.ones(jnp.broadcast_shapes(ar.shape, n.shape), F32)
    pi = jnp.zeros_like(pr)
    for b in range(nbits):
        bit = ((n >> b) & 1) == 1
        fr = jnp.where(bit, ar, 1.0)
        fi = jnp.where(bit, ai, 0.0)
        pr, pi = _cmul(pr, pi, fr, fi)
        ar, ai = _cmul(ar, ai, ar, ai)
    return pr, pi


def _ssm_prep_kernel(*refs, chunk, small, n_groups):
    for e in range(n_groups):
        _ssm_prep_group(*[r.at[pl.ds(e, 1)] for r in refs], chunk=chunk, small=small)


def _ssm_prep_group(ldt_ref, lr_ref, li_ref, bre_ref, bim_ref, cre_ref, cim_ref,
                    toep_ref, ctl_ref, obs_ref, al_ref,
                    toeps_ref, ctlrs_ref, ctlis_ref, obsrs_ref, obsis_ref, asr_ref, asi_ref, *, chunk, small):
    lw, sw = chunk * SSM_GROUP, small * SSM_GROUP
    p = STATE_DIM
    nbits = chunk.bit_length()
    dt = jnp.exp(ldt_ref[0])
    lr_row, li_row = lr_ref[0], li_ref[0]

    def discretise(lr, li):
        mag = jnp.exp(lr * dt)
        return mag * jnp.cos(li * dt), mag * jnp.sin(li * dt)

    def column(row):
        eye = lax.broadcasted_iota(jnp.int32, (p, p), 0) == lax.broadcasted_iota(jnp.int32, (p, p), 1)
        return jnp.sum(jnp.where(eye, row, 0.0), axis=-1, keepdims=True)

    lr, li = column(lr_row), column(li_row)
    ar, ai = discretise(lr, li)
    den = lr * lr + li * li
    nr, ni = ar - 1.0, ai
    cr, ci = (nr * lr + ni * li) / den, (ni * lr - nr * li) / den
    btr = jnp.concatenate([bre_ref[0]] * (2 * lw // LANES), axis=1)
    bti = jnp.concatenate([bim_ref[0]] * (2 * lw // LANES), axis=1)
    bbr, bbi = _cmul(cr, ci, btr, bti)
    sig = lax.broadcasted_iota(jnp.int32, (1, 2 * lw), 1) // SSM_GROUP
    er, ei = _cpow_by_bits(ar, ai, jnp.maximum(chunk - 1 - sig, 0), nbits)
    hr, hi = _cmul(er, ei, bbr, bbi)
    hr = jnp.where(sig < chunk, hr, 0.0)
    hi = jnp.where(sig < chunk, hi, 0.0)
    ctl_ref[0] = jnp.concatenate([hr[:, :lw], hi[:, :lw]], axis=0).astype(BF16)
    ctlrs_ref[0] = hr[:, lw - sw:lw]
    ctlis_ref[0] = hi[:, lw - sw:lw]
    c_re, c_im = cre_ref[0], cim_ref[0]
    hp = lax.Precision.HIGHEST
    k_all = (jnp.dot(c_re, hr, precision=hp, preferred_element_type=F32)
             - jnp.dot(c_im, hi, precision=hp, preferred_element_type=F32))
    for t in range(chunk):
        off = (chunk - 1 - t) * SSM_GROUP
        toep_ref[0, t * SSM_GROUP:(t + 1) * SSM_GROUP, :] = k_all[:, off:off + lw].astype(BF16)
        if t < small:
            toeps_ref[0, t * SSM_GROUP:(t + 1) * SSM_GROUP, :] = k_all[:, off:off + sw]

    arr, air = discretise(lr_row, li_row)
    tp1 = lax.broadcasted_iota(jnp.int32, (lw, 1), 0) // SSM_GROUP + 1
    pr, pi = _cpow_by_bits(arr, air, tp1, nbits)
    c_r = jnp.concatenate([c_re] * chunk, axis=0)
    c_i = jnp.concatenate([c_im] * chunk, axis=0)
    obs_re = c_r * pr - c_i * pi
    obs_im = -(c_r * pi + c_i * pr)
    obs_ref[0] = jnp.concatenate([obs_re, obs_im], axis=1).astype(BF16)
    obsrs_ref[0] = obs_re[:sw]
    obsis_ref[0] = obs_im[:sw]
    n = 1
    while n < chunk:
        arr, air = _cmul(arr, air, arr, air)
        n *= 2
        if n == small:
            asr_ref[0] = arr
            asi_ref[0] = air
    al_ref[0] = jnp.concatenate([jnp.concatenate([arr, arr], axis=1), jnp.concatenate([-air, air], axis=1)], axis=0)


def _ssm_prep(lam_re, lam_im, log_dt, b_re, b_im, c_re, c_im, *, chunk, small):
    assert chunk & (chunk - 1) == 0 and small & (small - 1) == 0 and 1 < small < chunk
    g, p, j = N_GROUPS, STATE_DIM, SSM_GROUP
    lw, sw = chunk * j, small * j
    lane_tile = lambda b: jnp.tile(b, (1, 1, LANES // j))
    ins = [log_dt.reshape(g, 1, 1), lam_re.reshape(g, 1, p), lam_im.reshape(g, 1, p), lane_tile(b_re), lane_tile(b_im),
           c_re, c_im]
    grp = lambda a, b: pl.BlockSpec((PREP_GROUPS, a, b), lambda i: (i, 0, 0))
    in_specs = [grp(1, 1), grp(1, p), grp(1, p), grp(p, LANES), grp(p, LANES), grp(j, p), grp(j, p)]
    shapes = [((lw, lw), BF16), ((2 * p, lw), BF16), ((lw, 2 * p), BF16), ((2, 2 * p), F32),
              ((sw, sw), F32), ((p, sw), F32), ((p, sw), F32), ((sw, p), F32), ((sw, p), F32), ((1, p), F32), ((1, p), F32)]
    outs = pl.pallas_call(
        functools.partial(_ssm_prep_kernel, chunk=chunk, small=small, n_groups=PREP_GROUPS),
        grid=(g // PREP_GROUPS,), in_specs=in_specs,
        out_specs=[grp(*s) for s, _ in shapes],
        out_shape=[jax.ShapeDtypeStruct((g,) + s, d) for s, d in shapes],
        compiler_params=_params("arbitrary"), name="ssm_prep")(*ins)
    return outs[:4], outs[4:]


def _ssm_step_kernel(*refs, n_groups):
    for e in range(n_groups):
        _ssm_step_group(*[r.at[pl.ds(e, 1)] for r in refs])


def _ssm_step_group(u_ref, toep_ref, ctlr_ref, ctli_ref, obsr_ref, obsi_ref, ar_ref, ai_ref, h0r_ref, h0i_ref,
                    y_ref, hr_ref, hi_ref):
    u = u_ref[0]
    h0r, h0i = h0r_ref[0], h0i_ref[0]
    er, ei = _cmul(ar_ref[0], ai_ref[0], h0r, h0i)
    hr_ref[0] = _dot_nt(u, ctlr_ref[0].astype(BF16)) + er
    hi_ref[0] = _dot_nt(u, ctli_ref[0].astype(BF16)) + ei
    y_ref[0] = (_dot_nt(u, toep_ref[0].astype(BF16)) + _dot_nt(h0r.astype(BF16), obsr_ref[0].astype(BF16))
                + _dot_nt(h0i.astype(BF16), obsi_ref[0].astype(BF16)))


def _ssm_step(u, ops, h0):
    n_seq, chunk, _ = u.shape
    g, p, j = N_GROUPS, STATE_DIM, SSM_GROUP
    lw = chunk * j
    ug = u.reshape(n_seq, chunk, g, j).transpose(2, 0, 1, 3).reshape(g, n_seq, lw).astype(BF16)
    grp = lambda a, b: pl.BlockSpec((STEP_GROUPS, a, b), lambda i: (i, 0, 0))
    y, hr, hi = pl.pallas_call(
        functools.partial(_ssm_step_kernel, n_groups=STEP_GROUPS), grid=(g // STEP_GROUPS,),
        in_specs=[grp(n_seq, lw), grp(lw, lw), grp(p, lw), grp(p, lw), grp(lw, p), grp(lw, p), grp(1, p), grp(1, p),
                  grp(n_seq, p), grp(n_seq, p)],
        out_specs=[grp(n_seq, lw), grp(n_seq, p), grp(n_seq, p)],
        out_shape=[jax.ShapeDtypeStruct((g, n_seq, lw), F32), jax.ShapeDtypeStruct((g, n_seq, p), F32),
                   jax.ShapeDtypeStruct((g, n_seq, p), F32)],
        compiler_params=_params("arbitrary"), name="ssm_step")(
            ug, *ops, jnp.swapaxes(h0[0], 0, 1), jnp.swapaxes(h0[1], 0, 1))
    y = y.reshape(g, n_seq, chunk, j).transpose(1, 2, 0, 3).reshape(n_seq, chunk, SSM_WIDTH)
    return y, jnp.swapaxes(hr, 0, 1), jnp.swapaxes(hi, 0, 1)


GROUPS_PER_TILE = LANES // SSM_GROUP
GROUPS_PER_TRIP = 8


def _block_transpose(xs):
    n = len(xs)
    blk = lax.broadcasted_iota(jnp.int32, xs[0].shape, 1) // SSM_GROUP
    xs = list(xs)
    d = n // 2
    while d:
        upper = (blk & d) != 0
        for i in range(n):
            if not i & d:
                lo, hi = xs[i], xs[i + d]
                xs[i] = jnp.where(upper, pltpu.roll(hi, SSM_GROUP * d, axis=1), lo)
                xs[i + d] = jnp.where(upper, hi, pltpu.roll(lo, LANES - SSM_GROUP * d, axis=1))
        d //= 2
    return xs


def _ssm_seq_kernel(u_ref, toep_ref, ctl_ref, obs_ref, al_ref, y_ref, h_ref, ug_ref, yg_ref, *, chunk):
    t = u_ref.shape[2]
    rows = t // chunk
    lw = chunk * SSM_GROUP
    n = GROUPS_PER_TILE
    p = STATE_DIM
    ridx = lax.broadcasted_iota(jnp.int32, (rows, 1), 0)

    def shifted(x, d):
        return jnp.where(ridx >= d, pltpu.roll(x, d, axis=0), 0.0)

    def group_tile(v, carry):
        for w in range(lw // LANES):
            xs = [u_ref[0, v, pl.ds(n * w + k, rows, stride=chunk), :] for k in range(n)]
            for k, x in enumerate(_block_transpose(xs)):
                ug_ref[k, :, w * LANES:(w + 1) * LANES] = x.astype(BF16)

        def groups(i, c):
            ks = [i * GROUPS_PER_TRIP + e for e in range(GROUPS_PER_TRIP)]
            gs = [v * n + k for k in ks]
            ugs = [ug_ref[k] for k in ks]
            ys = [_dot_nt(ug, toep_ref[g]) for ug, g in zip(ugs, gs)]
            hs = [_dot_nt(ug, ctl_ref[g]) for ug, g in zip(ugs, gs)]
            ars = [al_ref[g][0:1, :] for g in gs]
            ais = [al_ref[g][1:2, :] for g in gs]
            d = 1
            while d < rows:
                for e in range(GROUPS_PER_TRIP):
                    sh = shifted(hs[e], d)
                    hs[e] = hs[e] + ars[e] * sh + ais[e] * pltpu.roll(sh, p, axis=1)
                    ars[e], ais[e] = ars[e] * ars[e] - ais[e] * ais[e], 2.0 * ars[e] * ais[e]
                d *= 2
            for e, (k, g) in enumerate(zip(ks, gs)):
                h_ref[0, pl.ds(g, 1), :] = hs[e][rows - 1:rows, :]
                yg_ref[k] = ys[e] + _dot_nt(shifted(hs[e], 1).astype(BF16), obs_ref[g])
            return c

        lax.fori_loop(0, n // GROUPS_PER_TRIP, groups, 0)
        for w in range(lw // LANES):
            ys = [yg_ref[k, :, w * LANES:(w + 1) * LANES] for k in range(n)]
            for k, y in enumerate(_block_transpose(ys)):
                y_ref[0, v, pl.ds(n * w + k, rows, stride=chunk), :] = y
        return carry

    lax.fori_loop(0, N_GROUPS // n, group_tile, 0)


def _ssm_seq(u, ops, *, chunk):
    n_seq, _, t, _ = u.shape
    g, p = N_GROUPS, STATE_DIM
    lw = chunk * SSM_GROUP
    assert lw % LANES == 0 and chunk == GROUPS_PER_TILE * (lw // LANES) and SSM_WIDTH == g * SSM_GROUP
    whole = lambda a: pl.BlockSpec(a.shape, lambda b: (0, 0, 0), pipeline_mode=pl.Buffered(1))
    seq = pl.BlockSpec((1, SSM_WIDTH // LANES, t, LANES), lambda b: (b, 0, 0, 0))
    y, h = pl.pallas_call(
        functools.partial(_ssm_seq_kernel, chunk=chunk), grid=(n_seq,),
        in_specs=[seq] + [whole(o) for o in ops],
        out_specs=[seq, pl.BlockSpec((1, g, 2 * p), lambda b: (b, 0, 0))],
        out_shape=[jax.ShapeDtypeStruct(u.shape, F32), jax.ShapeDtypeStruct((n_seq, g, 2 * p), F32)],
        scratch_shapes=[pltpu.VMEM((GROUPS_PER_TILE, t // chunk, lw), BF16),
                        pltpu.VMEM((GROUPS_PER_TILE, t // chunk, lw), F32)],
        compiler_params=_params("arbitrary"), name="ssm_seq")(u, *ops)
    return y, h[:, :, :p], h[:, :, p:]


FF_CHUNK = 1024


def _post_kernel(x_ref, att_ref, ys_ref, u_ref, p_ref, lng_ref, lnb_ref, dsk_ref, wglu_ref, bglu_ref,
                 woa_ref, wos_ref, l1g_ref, l1b_ref, wup_ref, wdn_ref, wpe_ref, wpg_ref, bpg_ref,
                 l2g_ref, l2b_ref, o_ref):
    h = _layer_norm(x_ref[0], lng_ref[...], lnb_ref[...])
    y = _gelu_tanh(_load_lane_tiles(ys_ref) + dsk_ref[...] * _load_lane_tiles(u_ref))
    y = y * _sigmoid(_dot(y.astype(BF16), wglu_ref[...]) + bglu_ref[...])
    mix = _dot(att_ref[0], woa_ref[...]) + _dot(y.astype(BF16), wos_ref[...])
    h1 = _layer_norm(ALPHA * h + mix, l1g_ref[...], l1b_ref[...])
    h1b = h1.astype(BF16)
    e = _sigmoid(_dot(h1b, wpg_ref[...]) + bpg_ref[...]) * _dot(p_ref[0].astype(BF16), wpe_ref[...])
    acc = ALPHA * h1 + e
    for c in range(D_FF // FF_CHUNK):
        cs = slice(c * FF_CHUNK, (c + 1) * FF_CHUNK)
        a = jnp.maximum(_dot(h1b, wup_ref[:, cs]), 0.0)
        acc = acc + _dot((a * a).astype(BF16), wdn_ref[cs, :])
    o_ref[0] = _layer_norm(acc, l2g_ref[...], l2b_ref[...])


def _post_mixer(x, att, ys, u, p, weights, *, tm):
    nb, t, _ = x.shape
    tok = lambda w: pl.BlockSpec((1, tm, w), lambda b, i: (b, i, 0))
    const = lambda a: pl.BlockSpec(a.shape, lambda b, i: (0, 0), pipeline_mode=pl.Buffered(1))
    return pl.pallas_call(
        _post_kernel, grid=(nb, t // tm),
        in_specs=[tok(D_MODEL), tok(ATTN_WIDTH), _lane_tiles_spec(tm, SSM_WIDTH), _lane_tiles_spec(tm, SSM_WIDTH),
                  tok(PLE_DIM)] + [const(w) for w in weights],
        out_specs=tok(D_MODEL), out_shape=jax.ShapeDtypeStruct((nb, t, D_MODEL), F32),
        compiler_params=_params("arbitrary", "arbitrary"), name="post_mixer")(x, att, ys, u, p, *weights)


def kernel(x_prompt, x_sample, cache_k, cache_v, cache_logf, state_re, state_im, page_table, p_prompt, p_sample,
           ln_in_g, ln_in_b, w_in, b_f, lam_re, lam_im, log_dt, b_re, b_im, c_re, c_im, d_skip, w_glu, b_glu,
           w_out, ln1_g, ln1_b, w_up, w_down, w_pe, w_pg, b_pg, ln2_g, ln2_b):
    assert w_in.shape[0] == 1, "one trunk layer"
    nb, t, _ = x_prompt.shape
    db, nq, _ = x_sample.shape
    row = lambda a: a.reshape(1, -1)
    post_w = [row(ln_in_g), row(ln_in_b), row(d_skip[0]), w_glu[0].astype(BF16), row(b_glu[0]),
              w_out[0, :ATTN_WIDTH].astype(BF16), w_out[0, ATTN_WIDTH:].astype(BF16), row(ln1_g[0]), row(ln1_b[0]),
              w_up[0].astype(BF16), w_down[0].astype(BF16), w_pe[0].astype(BF16), w_pg[0].astype(BF16),
              row(b_pg[0]), row(ln2_g[0]), row(ln2_b[0])]
    ssm_par = (lam_re[0], lam_im[0], log_dt[0], b_re[0], b_im[0], c_re[0], c_im[0])

    u, kT, vT, lfT, cT, qTb, kb, vTb, cp = _in_proj(x_prompt, ln_in_g, ln_in_b, w_in[0], b_f[0],
                                                     tm=IN_PROJ_ROWS, q_scale=QK_SCALE * LOG2E, attn_layouts=True)
    att = _fox_prompt(qTb, kb, cp, vTb, cT, tq=ATTN_TILE)
    ssm_ops, ssm_ops_sample = _ssm_prep(*ssm_par, chunk=SSM_CHUNK, small=nq)
    ys, sr, si = _ssm_seq(u, ssm_ops, chunk=SSM_CHUNK)
    y_prompt = _post_mixer(x_prompt, att, ys, u, p_prompt[0], post_w, tm=POST_ROWS)
    heads_last = lambda a: jnp.transpose(a.reshape(nb, N_HEADS, HEAD_DIM, t), (0, 3, 1, 2))[None]
    prompt_out = (heads_last(kT), heads_last(vT), jnp.swapaxes(lfT, 1, 2)[None], sr[None], si[None])

    us, qs, ks, vs, logfs = _in_proj(x_sample.reshape(1, db * nq, D_MODEL), ln_in_g, ln_in_b, w_in[0], b_f[0],
                                     tm=db * nq, q_scale=QK_SCALE, attn_layouts=False)
    seq = lambda a: a.reshape(db, nq, a.shape[-1])
    to_tiles = lambda a: jnp.swapaxes(a.reshape(1, db * nq, SSM_WIDTH // LANES, LANES), 1, 2)
    from_tiles = lambda a: jnp.swapaxes(a, 1, 2).reshape(db, nq, SSM_WIDTH)
    att_s = _fox_sample(seq(qs), seq(ks), seq(vs), seq(logfs), cache_k[0], cache_v[0], cache_logf[0], page_table)
    ys_s, sr_s, si_s = _ssm_step(from_tiles(us), ssm_ops_sample, (state_re[0], state_im[0]))
    flat = lambda a: a.reshape(1, db * nq, a.shape[-1])
    y_sample = _post_mixer(flat(x_sample), flat(att_s).astype(BF16), to_tiles(ys_s), us, flat(p_sample[0]),
                           post_w, tm=db * nq).reshape(db, nq, D_MODEL)
    sample_out = (ks.reshape(1, db, nq, N_HEADS, HEAD_DIM), vs.reshape(1, db, nq, N_HEADS, HEAD_DIM),
                  logfs.reshape(1, db, nq, N_HEADS), sr_s[None], si_s[None])
    return (y_prompt, y_sample) + prompt_out + sample_out
```

```python
import functools
import math

import jax
import jax.numpy as jnp
from jax import lax
from jax.experimental import pallas as pl
from jax.experimental.pallas import tpu as pltpu

F32 = jnp.float32
BF16 = jnp.bfloat16

D_MODEL = 1024
ATTN_WIDTH = 512
SSM_WIDTH = 512
HEAD_DIM = 64
N_HEADS = 8
SSM_GROUP = 16
N_GROUPS = 32
STATE_DIM = 64
D_FF = 4096
PLE_DIM = 256
PAGE_SIZE = 128
ALPHA = 2.0 ** 0.25
LN_EPS = 1e-5
NEG_INF = -1e30
QK_SCALE = HEAD_DIM ** -0.5
LOG2E = math.log2(math.e)

VMEM_LIMIT_BYTES = 56 * 1024 * 1024

IN_PROJ_ROWS = 512
ATTN_TILE = 256
SSM_CHUNK = 16
POST_ROWS = 512

_NT = (((1,), (1,)), ((), ()))


def _params(*sem):
    return pltpu.CompilerParams(dimension_semantics=sem, vmem_limit_bytes=VMEM_LIMIT_BYTES)


def _const_spec(shape):
    return pl.BlockSpec(shape, lambda *_: (0,) * len(shape))


def _layer_norm(x, g, b):
    mu = jnp.mean(x, axis=-1, keepdims=True)
    xc = x - mu
    var = jnp.mean(xc * xc, axis=-1, keepdims=True)
    return xc * lax.rsqrt(var + LN_EPS) * g + b


def _log_sigmoid(x):
    return jnp.minimum(x, 0.0) - jnp.log1p(jnp.exp(-jnp.abs(x)))


def _sigmoid(x):
    return 1.0 / (1.0 + jnp.exp(-x))


def _gelu_tanh(x):
    return 0.5 * x * (1.0 + jnp.tanh(math.sqrt(2.0 / math.pi) * (x + 0.044715 * (x * x * x))))


def _dot(a, b):
    return jnp.dot(a, b, preferred_element_type=F32)


def _dot_nt(a, b):
    return lax.dot_general(a, b, _NT, preferred_element_type=F32)


def _split3(x):
    hi = x.astype(BF16)
    r1 = x - hi.astype(F32)
    mid = r1.astype(BF16)
    lo = (r1 - mid.astype(F32)).astype(BF16)
    return hi, mid, lo


LANES = 128


def _lane_tiles_spec(tm, width):
    return pl.BlockSpec((1, width // LANES, tm, LANES), lambda b, i: (b, 0, i, 0))


def _store_lane_tiles(ref, x):
    for v in range(x.shape[-1] // LANES):
        ref[0, v] = x[:, v * LANES:(v + 1) * LANES]


def _load_lane_tiles(ref):
    return jnp.concatenate([ref[0, v] for v in range(ref.shape[1])], axis=-1)


def _dot_exact01(x, m01):
    hi, mid, lo = _split3(x)
    return _dot(hi, m01) + _dot(mid, m01) + _dot(lo, m01)


N_BIAS_ROWS = 128
ONES_ROW = 3 * N_HEADS
SUM_ROWS = 16


def _in_proj_kernel(*refs, tm, q_scale, attn_layouts):
    x_ref, g_ref, b_ref, wu_ref = refs[:4]
    if attn_layouts:
        (wqT_ref, wkT_ref, wvT_ref, wfT_ref, bfc_ref, tri_ref,
         u_ref, kT_ref, vT_ref, lfT_ref, cT_ref, qTb_ref, kb_ref, vTb_ref, cp_ref, carry_ref) = refs[4:]
    else:
        wq_ref, wk_ref, wv_ref, wf_ref, bf_ref, u_ref, q_ref, k_ref, v_ref, lf_ref = refs[4:]
    hb = _layer_norm(x_ref[0], g_ref[...], b_ref[...]).astype(BF16)
    _store_lane_tiles(u_ref, _dot(hb, wu_ref[...]))
    if attn_layouts:
        qTb_ref[0] = (_dot_nt(wqT_ref[...], hb) * q_scale).astype(BF16)
        kT = _dot_nt(wkT_ref[...], hb)
        kT_ref[0] = kT
        kb_ref[0] = kT.T.astype(BF16)
        vT = _dot_nt(wvT_ref[...], hb)
        vT_ref[0] = vT
        vTb_ref[0] = vT.astype(BF16)
        lfT = _log_sigmoid(_dot_nt(wfT_ref[...], hb) + bfc_ref[...])
        lfT_ref[0] = lfT

        @pl.when(pl.program_id(1) == 0)
        def _():
            carry_ref[...] = jnp.zeros_like(carry_ref)

        c = carry_ref[...] + _dot_exact01(lfT, tri_ref[...])
        cT_ref[0] = c
        carry_ref[...] = c[:, tm - 1:tm]
        hi, mid, lo = _split3(c * LOG2E)
        row = lax.broadcasted_iota(jnp.int32, (N_BIAS_ROWS - ONES_ROW, tm), 0)
        ones = jnp.where(row < 3, 1.0, 0.0)
        pieces = jnp.concatenate([hi.astype(F32), mid.astype(F32), lo.astype(F32), ones], axis=0)
        cp_ref[0] = pieces.T.astype(BF16)
    else:
        q_ref[0] = _dot(hb, wq_ref[...]) * q_scale
        k_ref[0] = _dot(hb, wk_ref[...])
        v_ref[0] = _dot(hb, wv_ref[...])
        lf_ref[0] = _log_sigmoid(_dot(hb, wf_ref[...]) + bf_ref[...])


def _in_proj(x3, ln_g, ln_b, w_in, b_f, *, tm, q_scale, attn_layouts):
    nb, t, _ = x3.shape
    wq = w_in[:, :ATTN_WIDTH].astype(BF16)
    wk = w_in[:, ATTN_WIDTH:2 * ATTN_WIDTH].astype(BF16)
    wv = w_in[:, 2 * ATTN_WIDTH:3 * ATTN_WIDTH].astype(BF16)
    wf = w_in[:, 3 * ATTN_WIDTH:3 * ATTN_WIDTH + N_HEADS].astype(BF16)
    wu = w_in[:, 3 * ATTN_WIDTH + N_HEADS:].astype(BF16)
    tok = lambda w: pl.BlockSpec((1, tm, w), lambda b, i: (b, i, 0))
    feat = lambda w: pl.BlockSpec((1, w, tm), lambda b, i: (b, 0, i))
    sds = jax.ShapeDtypeStruct
    ins = [x3, ln_g.reshape(1, -1), ln_b.reshape(1, -1), wu]
    in_specs = [tok(D_MODEL), _const_spec((1, D_MODEL)), _const_spec((1, D_MODEL)), _const_spec((D_MODEL, SSM_WIDTH))]
    out_shape = [sds((nb, SSM_WIDTH // LANES, t, LANES), F32)]
    out_specs = [_lane_tiles_spec(tm, SSM_WIDTH)]
    scratch = []
    if attn_layouts:
        tri = (jnp.arange(tm)[:, None] <= jnp.arange(tm)[None, :]).astype(BF16)
        ins += [wq.T, wk.T, wv.T, wf.T, b_f.reshape(N_HEADS, 1), tri]
        in_specs += [_const_spec((ATTN_WIDTH, D_MODEL))] * 3
        in_specs += [_const_spec((N_HEADS, D_MODEL)), _const_spec((N_HEADS, 1)), _const_spec((tm, tm))]
        out_shape += [sds((nb, ATTN_WIDTH, t), F32), sds((nb, ATTN_WIDTH, t), F32),
                      sds((nb, N_HEADS, t), F32), sds((nb, N_HEADS, t), F32),
                      sds((nb, ATTN_WIDTH, t), BF16), sds((nb, t, ATTN_WIDTH), BF16), sds((nb, ATTN_WIDTH, t), BF16),
                      sds((nb, t, N_BIAS_ROWS), BF16)]
        out_specs += [feat(ATTN_WIDTH), feat(ATTN_WIDTH), feat(N_HEADS), feat(N_HEADS),
                      feat(ATTN_WIDTH), tok(ATTN_WIDTH), feat(ATTN_WIDTH), tok(N_BIAS_ROWS)]
        scratch = [pltpu.VMEM((N_HEADS, 1), F32)]
    else:
        ins += [wq, wk, wv, wf, b_f.reshape(1, N_HEADS)]
        in_specs += [_const_spec((D_MODEL, ATTN_WIDTH))] * 3 + [_const_spec((D_MODEL, N_HEADS)), _const_spec((1, N_HEADS))]
        out_shape += [sds((nb, t, ATTN_WIDTH), F32)] * 3 + [sds((nb, t, N_HEADS), F32)]
        out_specs += [tok(ATTN_WIDTH)] * 3 + [tok(N_HEADS)]
    return pl.pallas_call(
        functools.partial(_in_proj_kernel, tm=tm, q_scale=q_scale, attn_layouts=attn_layouts),
        grid=(nb, t // tm), in_specs=in_specs, out_specs=out_specs, out_shape=out_shape,
        scratch_shapes=scratch, compiler_params=_params("arbitrary", "arbitrary"),
        name="in_proj_prompt" if attn_layouts else "in_proj_sample")(*ins)


def _fox_prompt_kernel(qT_ref, k_ref, cp_ref, vT_ref, cT_ref, o_ref, bq_ref, m_ref, acc_ref, s0_ref, s1_ref, *, tq):
    qi = pl.program_id(1)
    q0 = pl.multiple_of(qi * tq, tq)
    heads = [slice(h * HEAD_DIM, (h + 1) * HEAD_DIM) for h in range(N_HEADS)]
    pairs = [slice(2 * HEAD_DIM * (h // 2), 2 * HEAD_DIM * (h // 2 + 1)) for h in range(N_HEADS)]

    c0 = cT_ref[0, :, pl.ds(q0, tq)][:, 0:1] * LOG2E
    c0_pieces = [p.astype(F32) for p in _split3(c0)]
    r = lax.broadcasted_iota(jnp.int32, (2 * HEAD_DIM, tq), 0)
    for h in range(N_HEADS):
        mine = (r < HEAD_DIM) if h % 2 == 0 else (r >= HEAD_DIM)
        bq_ref[h, 0:2 * HEAD_DIM, :] = jnp.where(mine, qT_ref[0, pairs[h], :], jnp.zeros((), BF16))
        sel = jnp.where((r == h) | (r == N_HEADS + h) | (r == 2 * N_HEADS + h), -1.0, 0.0)
        for j, piece in enumerate(c0_pieces):
            sel = jnp.where(r == ONES_ROW + j, piece[h:h + 1, :], sel)
        bq_ref[h, 2 * HEAD_DIM:, :] = sel.astype(BF16)

    key_row = lax.broadcasted_iota(jnp.int32, (tq, tq), 0)
    qry_col = lax.broadcasted_iota(jnp.int32, (tq, tq), 1)

    m_ref[...] = jnp.full_like(m_ref, NEG_INF)
    acc_ref[...] = jnp.zeros_like(acc_ref)
    ones = jnp.ones((SUM_ROWS, tq), BF16)

    s_refs = (s0_ref, s1_ref)

    def scores(kt, h, buf):
        ks = pl.multiple_of(kt * tq, tq)
        a = jnp.concatenate([k_ref[0, pl.ds(ks, tq), pairs[h]], cp_ref[0, pl.ds(ks, tq), :]], axis=1)
        s_refs[buf][h] = _dot(a, bq_ref[h])

    def accumulate(kt, h, buf, diagonal):
        ks = pl.multiple_of(kt * tq, tq)
        s = s_refs[buf][h]
        if diagonal:
            s = jnp.where(key_row <= qry_col, s, NEG_INF)
        m_new = jnp.maximum(m_ref[h], jnp.max(s, axis=0, keepdims=True))
        alpha = jnp.exp2(m_ref[h] - m_new)
        m_ref[h] = m_new
        p = jnp.exp2((s - m_new).astype(BF16))
        v1 = jnp.concatenate([vT_ref[0, heads[h], pl.ds(ks, tq)], ones], axis=0)
        acc_ref[h] = alpha * acc_ref[h] + _dot(v1, p)

    def stage(kt, cur, *, diagonal=False, issue_next=True):
        for h in range(N_HEADS + 1):
            if issue_next and h < N_HEADS:
                scores(kt + 1, h, 1 - cur)
            if h >= 1:
                accumulate(kt, h - 1, cur, diagonal)

    for h in range(N_HEADS):
        scores(0, h, 0)

    def body(j, carry):
        stage(2 * j, 0)
        stage(2 * j + 1, 1)
        return carry

    lax.fori_loop(0, qi // 2, body, 0)

    @pl.when(qi % 2 == 0)
    def _():
        stage(qi, 0, diagonal=True, issue_next=False)

    @pl.when(qi % 2 == 1)
    def _():
        stage(qi - 1, 0)
        stage(qi, 1, diagonal=True, issue_next=False)

    outs = [acc_ref[h, 0:HEAD_DIM, :] / acc_ref[h, HEAD_DIM:HEAD_DIM + 1, :] for h in range(N_HEADS)]
    o_ref[0] = jnp.concatenate(outs, axis=0).T.astype(o_ref.dtype)


def _fox_prompt(qT, k, cp, vT, cT, *, tq):
    nb, t, _ = k.shape
    assert N_BIAS_ROWS == 2 * HEAD_DIM
    whole_f = lambda w: pl.BlockSpec((1, w, t), lambda b, i: (b, 0, 0))
    whole_t = lambda w: pl.BlockSpec((1, t, w), lambda b, i: (b, 0, 0))
    return pl.pallas_call(
        functools.partial(_fox_prompt_kernel, tq=tq),
        grid=(nb, t // tq),
        in_specs=[pl.BlockSpec((1, ATTN_WIDTH, tq), lambda b, i: (b, 0, i)),
                  whole_t(ATTN_WIDTH), whole_t(N_BIAS_ROWS), whole_f(ATTN_WIDTH), whole_f(N_HEADS)],
        out_specs=pl.BlockSpec((1, tq, ATTN_WIDTH), lambda b, i: (b, i, 0)),
        out_shape=jax.ShapeDtypeStruct((nb, t, ATTN_WIDTH), BF16),
        scratch_shapes=[pltpu.VMEM((N_HEADS, 2 * HEAD_DIM + N_BIAS_ROWS, tq), BF16),
                        pltpu.VMEM((N_HEADS, 1, tq), F32), pltpu.VMEM((N_HEADS, HEAD_DIM + SUM_ROWS, tq), F32),
                        pltpu.VMEM((N_HEADS, tq, tq), F32), pltpu.VMEM((N_HEADS, tq, tq), F32)],
        compiler_params=_params("arbitrary", "arbitrary"), name="fox_prompt")(qT, k, cp, vT, cT)


SAMPLE_SEQS_PER_STEP = 2

def _fox_sample_kernel(pt_ref, q_ref, kn_ref, vn_ref, lfn_ref, sl_ref, *refs, n_pages, nq, n_seq):
    del pt_ref
    o_ref = refs[3 * n_seq * n_pages]
    for e in range(n_seq):
        pages = [refs[(kind * n_seq + e) * n_pages:(kind * n_seq + e + 1) * n_pages] for kind in range(3)]
        _fox_sample_one(e, q_ref, kn_ref, vn_ref, lfn_ref, sl_ref, *pages, o_ref, n_pages=n_pages, nq=nq)


def _fox_sample_one(e, q_ref, kn_ref, vn_ref, lfn_ref, sl_ref, k_refs, v_refs, pf_refs, o_ref, *, n_pages, nq):
    nr = nq * N_HEADS
    head_of_lane = lax.broadcasted_iota(jnp.int32, (N_HEADS, ATTN_WIDTH), 1) // HEAD_DIM
    own = head_of_lane == lax.broadcasted_iota(jnp.int32, (N_HEADS, ATTN_WIDTH), 0)
    q = q_ref[e]
    qbd32 = jnp.concatenate([jnp.where(own, q[i:i + 1, :], 0.0) for i in range(nq)], axis=0)
    qbd = qbd32.astype(BF16)
    rep = lambda e: jnp.concatenate([e] * nq, axis=0)
    rows = lambda j: slice(j * N_HEADS, (j + 1) * N_HEADS)

    lfn = lfn_ref[e]
    e_new = rep(_dot_exact01(lfn, sl_ref[...]))
    kn, vn = kn_ref[e], vn_ref[e]
    qidx = lax.broadcasted_iota(jnp.int32, (nr, 1), 0) // N_HEADS
    s_new = [jnp.where(qidx >= j, jnp.sum(qbd32 * kn[j:j + 1, :], axis=-1, keepdims=True) + e_new[:, j:j + 1], NEG_INF)
             for j in range(nq)]

    pf_all = jnp.concatenate([r[0, 0] for r in pf_refs], axis=0)
    sfx_all = _dot_exact01(pf_all, sl_ref[...])
    tot_all = sfx_all[:, 0:1] + pf_all[:, 0:1]
    tail = jnp.sum(lfn, axis=-1, keepdims=True)
    scores = [None] * n_pages
    for j in reversed(range(n_pages)):
        kt = k_refs[j][0, 0].reshape(ATTN_WIDTH, PAGE_SIZE).astype(BF16)
        scores[j] = _dot(qbd, kt) + rep(sfx_all[rows(j)] + tail)
        tail = tail + tot_all[rows(j)]
    s_all = jnp.concatenate(scores, axis=1)
    m = jnp.max(s_all, axis=-1, keepdims=True)
    for s in s_new:
        m = jnp.maximum(m, s)
    p_all = jnp.exp(s_all - m)
    l = jnp.sum(p_all, axis=-1, keepdims=True)
    acc = jnp.zeros((nr, ATTN_WIDTH), F32)
    for j, s in enumerate(s_new):
        p = jnp.exp(s - m)
        l = l + p
        acc = acc + p * vn[j:j + 1, :]
    p_all = p_all.astype(BF16)
    for j in range(n_pages):
        vt = v_refs[j][0, 0].reshape(ATTN_WIDTH, PAGE_SIZE).astype(BF16)
        acc = acc + _dot_nt(p_all[:, j * PAGE_SIZE:(j + 1) * PAGE_SIZE], vt)
    acc = acc / l
    for i in range(nq):
        o_ref[e, i:i + 1, :] = jnp.sum(jnp.where(own, acc[i * N_HEADS:(i + 1) * N_HEADS], 0.0), axis=0, keepdims=True)


def _fox_sample(q, k_new, v_new, logf_new, cache_k, cache_v, cache_logf, page_table):
    db, nq, _ = q.shape
    n_pages = page_table.shape[1]
    kc = jnp.transpose(cache_k, (0, 2, 3, 1))
    vc = jnp.transpose(cache_v, (0, 2, 3, 1))
    pf = jnp.transpose(cache_logf, (0, 2, 1))
    lfn_t = jnp.swapaxes(jnp.pad(logf_new, ((0, 0), (0, PAGE_SIZE - nq), (0, 0))), 1, 2)
    strict_lower = (jnp.arange(PAGE_SIZE)[:, None] > jnp.arange(PAGE_SIZE)[None, :]).astype(BF16)
    ns = SAMPLE_SEQS_PER_STEP
    seq = lambda r, w: pl.BlockSpec((ns, r, w), lambda b, pt: (b, 0, 0))
    page = lambda e, j: pl.BlockSpec((1, 1, N_HEADS, HEAD_DIM, PAGE_SIZE), lambda b, pt: (0, pt[ns * b + e, j], 0, 0, 0))
    page_f = lambda e, j: pl.BlockSpec((1, 1, N_HEADS, PAGE_SIZE), lambda b, pt: (0, pt[ns * b + e, j], 0, 0))
    every = [(e, j) for e in range(ns) for j in range(n_pages)]
    in_specs = [seq(nq, ATTN_WIDTH), seq(nq, ATTN_WIDTH), seq(nq, ATTN_WIDTH), seq(N_HEADS, PAGE_SIZE),
                pl.BlockSpec((PAGE_SIZE, PAGE_SIZE), lambda b, pt: (0, 0))]
    in_specs += [page(e, j) for e, j in every] * 2 + [page_f(e, j) for e, j in every]
    return pl.pallas_call(
        functools.partial(_fox_sample_kernel, n_pages=n_pages, nq=nq, n_seq=ns),
        grid_spec=pltpu.PrefetchScalarGridSpec(
            num_scalar_prefetch=1, grid=(db // ns,), in_specs=in_specs, out_specs=seq(nq, ATTN_WIDTH)),
        out_shape=jax.ShapeDtypeStruct((db, nq, ATTN_WIDTH), F32),
        compiler_params=_params("arbitrary"), name="fox_sample")(
            page_table, q, k_new, v_new, lfn_t, strict_lower,
            *([kc[None]] * len(every)), *([vc[None]] * len(every)), *([pf[None]] * len(every)))


PREP_GROUPS = 2

def _cmul(ar, ai, br, bi):
    return ar * br - ai * bi, ar * bi + ai * br


def _cpow_by_bits(ar, ai, n, nbits):
    pr = jnp.ones(jnp.broadcast_shapes(ar.shape, n.shape), F32)
    pi = jnp.zeros_like(pr)
    for b in range(nbits):
        bit = ((n >> b) & 1) == 1
        fr = jnp.where(bit, ar, 1.0)
        fi = jnp.where(bit, ai, 0.0)
        pr, pi = _cmul(pr, pi, fr, fi)
        ar, ai = _cmul(ar, ai, ar, ai)
    return pr, pi


def _ssm_prep_kernel(*refs, chunk, small, n_groups):
    for e in range(n_groups):
        _ssm_prep_group(*[r.at[pl.ds(e, 1)] for r in refs], chunk=chunk, small=small)


def _ssm_prep_group(ldt_ref, lr_ref, li_ref, bre_ref, bim_ref, cre_ref, cim_ref,
                    toep_ref, ctl_ref, obs_ref, al_ref,
                    toeps_ref, ctlrs_ref, ctlis_ref, obsrs_ref, obsis_ref, asr_ref, asi_ref, *, chunk, small):
    lw, sw = chunk * SSM_GROUP, small * SSM_GROUP
    p = STATE_DIM
    nbits = chunk.bit_length()
    dt = jnp.exp(ldt_ref[0])
    lr_row, li_row = lr_ref[0], li_ref[0]

    def discretise(lr, li):
        mag = jnp.exp(lr * dt)
        return mag * jnp.cos(li * dt), mag * jnp.sin(li * dt)

    def column(row):
        eye = lax.broadcasted_iota(jnp.int32, (p, p), 0) == lax.broadcasted_iota(jnp.int32, (p, p), 1)
        return jnp.sum(jnp.where(eye, row, 0.0), axis=-1, keepdims=True)

    lr, li = column(lr_row), column(li_row)
    ar, ai = discretise(lr, li)
    den = lr * lr + li * li
    nr, ni = ar - 1.0, ai
    cr, ci = (nr * lr + ni * li) / den, (ni * lr - nr * li) / den
    btr = jnp.concatenate([bre_ref[0]] * (2 * lw // LANES), axis=1)
    bti = jnp.concatenate([bim_ref[0]] * (2 * lw // LANES), axis=1)
    bbr, bbi = _cmul(cr, ci, btr, bti)
    sig = lax.broadcasted_iota(jnp.int32, (1, 2 * lw), 1) // SSM_GROUP
    er, ei = _cpow_by_bits(ar, ai, jnp.maximum(chunk - 1 - sig, 0), nbits)
    hr, hi = _cmul(er, ei, bbr, bbi)
    hr = jnp.where(sig < chunk, hr, 0.0)
    hi = jnp.where(sig < chunk, hi, 0.0)
    ctl_ref[0] = jnp.concatenate([hr[:, :lw], hi[:, :lw]], axis=0).astype(BF16)
    ctlrs_ref[0] = hr[:, lw - sw:lw]
    ctlis_ref[0] = hi[:, lw - sw:lw]
    c_re, c_im = cre_ref[0], cim_ref[0]
    hp = lax.Precision.HIGHEST
    k_all = (jnp.dot(c_re, hr, precision=hp, preferred_element_type=F32)
             - jnp.dot(c_im, hi, precision=hp, preferred_element_type=F32))
    for t in range(chunk):
        off = (chunk - 1 - t) * SSM_GROUP
        toep_ref[0, t * SSM_GROUP:(t + 1) * SSM_GROUP, :] = k_all[:, off:off + lw].astype(BF16)
        if t < small:
            toeps_ref[0, t * SSM_GROUP:(t + 1) * SSM_GROUP, :] = k_all[:, off:off + sw]

    arr, air = discretise(lr_row, li_row)
    tp1 = lax.broadcasted_iota(jnp.int32, (lw, 1), 0) // SSM_GROUP + 1
    pr, pi = _cpow_by_bits(arr, air, tp1, nbits)
    c_r = jnp.concatenate([c_re] * chunk, axis=0)
    c_i = jnp.concatenate([c_im] * chunk, axis=0)
    obs_re = c_r * pr - c_i * pi
    obs_im = -(c_r * pi + c_i * pr)
    obs_ref[0] = jnp.concatenate([obs_re, obs_im], axis=1).astype(BF16)
    obsrs_ref[0] = obs_re[:sw]
    obsis_ref[0] = obs_im[:sw]
    n = 1
    while n < chunk:
        arr, air = _cmul(arr, air, arr, air)
        n *= 2
        if n == small:
            asr_ref[0] = arr
            asi_ref[0] = air
    al_ref[0] = jnp.concatenate([jnp.concatenate([arr, arr], axis=1), jnp.concatenate([-air, air], axis=1)], axis=0)


def _ssm_prep(lam_re, lam_im, log_dt, b_re, b_im, c_re, c_im, *, chunk, small):
    assert chunk & (chunk - 1) == 0 and small & (small - 1) == 0 and 1 < small < chunk
    g, p, j = N_GROUPS, STATE_DIM, SSM_GROUP
    lw, sw = chunk * j, small * j
    lane_tile = lambda b: jnp.tile(b, (1, 1, LANES // j))
    ins = [log_dt.reshape(g, 1, 1), lam_re.reshape(g, 1, p), lam_im.reshape(g, 1, p), lane_tile(b_re), lane_tile(b_im),
           c_re, c_im]
    grp = lambda a, b: pl.BlockSpec((PREP_GROUPS, a, b), lambda i: (i, 0, 0))
    in_specs = [grp(1, 1), grp(1, p), grp(1, p), grp(p, LANES), grp(p, LANES), grp(j, p), grp(j, p)]
    shapes = [((lw, lw), BF16), ((2 * p, lw), BF16), ((lw, 2 * p), BF16), ((2, 2 * p), F32),
              ((sw, sw), F32), ((p, sw), F32), ((p, sw), F32), ((sw, p), F32), ((sw, p), F32), ((1, p), F32), ((1, p), F32)]
    outs = pl.pallas_call(
        functools.partial(_ssm_prep_kernel, chunk=chunk, small=small, n_groups=PREP_GROUPS),
        grid=(g // PREP_GROUPS,), in_specs=in_specs,
        out_specs=[grp(*s) for s, _ in shapes],
        out_shape=[jax.ShapeDtypeStruct((g,) + s, d) for s, d in shapes],
        compiler_params=_params("arbitrary"), name="ssm_prep")(*ins)
    return outs[:4], outs[4:]


def _ssm_step_kernel(u_ref, toep_ref, ctlr_ref, ctli_ref, obsr_ref, obsi_ref, ar_ref, ai_ref, h0r_ref, h0i_ref,
                     y_ref, hr_ref, hi_ref, *, chunk):
    n = GROUPS_PER_TILE
    n_seq = h0r_ref.shape[1]
    lw = chunk * SSM_GROUP
    zero = jnp.zeros((n_seq, LANES), F32)
    xs = [u_ref[0, 0, pl.ds(t, n_seq, stride=chunk), :] if t < chunk else zero for t in range(n)]
    ys = []
    for e, x in enumerate(_block_transpose(xs)):
        u = x[:, :lw].astype(BF16)
        h0r, h0i = h0r_ref[e], h0i_ref[e]
        er, ei = _cmul(ar_ref[e], ai_ref[e], h0r, h0i)
        hr_ref[e] = _dot_nt(u, ctlr_ref[e].astype(BF16)) + er
        hi_ref[e] = _dot_nt(u, ctli_ref[e].astype(BF16)) + ei
        y = (_dot_nt(u, toep_ref[e].astype(BF16)) + _dot_nt(h0r.astype(BF16), obsr_ref[e].astype(BF16))
             + _dot_nt(h0i.astype(BF16), obsi_ref[e].astype(BF16)))
        ys.append(jnp.concatenate([y, jnp.zeros((n_seq, LANES - lw), F32)], axis=1))
    for t, y in enumerate(_block_transpose(ys)[:chunk]):
        y_ref[0, 0, pl.ds(t, n_seq, stride=chunk), :] = y


def _ssm_step(u, ops, h0, *, chunk):
    n_seq = h0[0].shape[0]
    g, p, n = N_GROUPS, STATE_DIM, GROUPS_PER_TILE
    lw = chunk * SSM_GROUP
    assert lw <= LANES and u.shape == (1, g // n, n_seq * chunk, LANES)
    grp = lambda a, b: pl.BlockSpec((n, a, b), lambda v: (v, 0, 0))
    tile = pl.BlockSpec((1, 1, n_seq * chunk, LANES), lambda v: (0, v, 0, 0))
    y, hr, hi = pl.pallas_call(
        functools.partial(_ssm_step_kernel, chunk=chunk), grid=(g // n,),
        in_specs=[tile, grp(lw, lw), grp(p, lw), grp(p, lw), grp(lw, p), grp(lw, p), grp(1, p), grp(1, p),
                  grp(n_seq, p), grp(n_seq, p)],
        out_specs=[tile, grp(n_seq, p), grp(n_seq, p)],
        out_shape=[jax.ShapeDtypeStruct(u.shape, F32), jax.ShapeDtypeStruct((g, n_seq, p), F32),
                   jax.ShapeDtypeStruct((g, n_seq, p), F32)],
        compiler_params=_params("arbitrary"), name="ssm_step")(
            u, *ops, jnp.swapaxes(h0[0], 0, 1), jnp.swapaxes(h0[1], 0, 1))
    return y, jnp.swapaxes(hr, 0, 1), jnp.swapaxes(hi, 0, 1)


GROUPS_PER_TILE = LANES // SSM_GROUP
GROUPS_PER_TRIP = 8


def _block_transpose(xs):
    n = len(xs)
    blk = lax.broadcasted_iota(jnp.int32, xs[0].shape, 1) // SSM_GROUP
    xs = list(xs)
    d = n // 2
    while d:
        upper = (blk & d) != 0
        for i in range(n):
            if not i & d:
                lo, hi = xs[i], xs[i + d]
                xs[i] = jnp.where(upper, pltpu.roll(hi, SSM_GROUP * d, axis=1), lo)
                xs[i + d] = jnp.where(upper, hi, pltpu.roll(lo, LANES - SSM_GROUP * d, axis=1))
        d //= 2
    return xs


def _ssm_seq_kernel(u_ref, toep_ref, ctl_ref, obs_ref, al_ref, y_ref, h_ref, ug_ref, yg_ref, *, chunk):
    t = u_ref.shape[2]
    rows = t // chunk
    lw = chunk * SSM_GROUP
    n = GROUPS_PER_TILE
    p = STATE_DIM
    ridx = lax.broadcasted_iota(jnp.int32, (rows, 1), 0)

    def shifted(x, d):
        return jnp.where(ridx >= d, pltpu.roll(x, d, axis=0), 0.0)

    def group_tile(v, carry):
        for w in range(lw // LANES):
            xs = [u_ref[0, v, pl.ds(n * w + k, rows, stride=chunk), :] for k in range(n)]
            for k, x in enumerate(_block_transpose(xs)):
                ug_ref[k, :, w * LANES:(w + 1) * LANES] = x.astype(BF16)

        def groups(i, c):
            ks = [i * GROUPS_PER_TRIP + e for e in range(GROUPS_PER_TRIP)]
            gs = [v * n + k for k in ks]
            ugs = [ug_ref[k] for k in ks]
            ys = [_dot_nt(ug, toep_ref[g]) for ug, g in zip(ugs, gs)]
            hs = [_dot_nt(ug, ctl_ref[g]) for ug, g in zip(ugs, gs)]
            ars = [al_ref[g][0:1, :] for g in gs]
            ais = [al_ref[g][1:2, :] for g in gs]
            d = 1
            while d < rows:
                for e in range(GROUPS_PER_TRIP):
                    sh = shifted(hs[e], d)
                    hs[e] = hs[e] + ars[e] * sh + ais[e] * pltpu.roll(sh, p, axis=1)
                    ars[e], ais[e] = ars[e] * ars[e] - ais[e] * ais[e], 2.0 * ars[e] * ais[e]
                d *= 2
            for e, (k, g) in enumerate(zip(ks, gs)):
                h_ref[0, pl.ds(g, 1), :] = hs[e][rows - 1:rows, :]
                yg_ref[k] = ys[e] + _dot_nt(shifted(hs[e], 1).astype(BF16), obs_ref[g])
            return c

        lax.fori_loop(0, n // GROUPS_PER_TRIP, groups, 0)
        for w in range(lw // LANES):
            ys = [yg_ref[k, :, w * LANES:(w + 1) * LANES] for k in range(n)]
            for k, y in enumerate(_block_transpose(ys)):
                y_ref[0, v, pl.ds(n * w + k, rows, stride=chunk), :] = y
        return carry

    lax.fori_loop(0, N_GROUPS // n, group_tile, 0)


def _ssm_seq(u, ops, *, chunk):
    n_seq, _, t, _ = u.shape
    g, p = N_GROUPS, STATE_DIM
    lw = chunk * SSM_GROUP
    assert lw % LANES == 0 and chunk == GROUPS_PER_TILE * (lw // LANES) and SSM_WIDTH == g * SSM_GROUP
    whole = lambda a: pl.BlockSpec(a.shape, lambda b: (0, 0, 0), pipeline_mode=pl.Buffered(1))
    seq = pl.BlockSpec((1, SSM_WIDTH // LANES, t, LANES), lambda b: (b, 0, 0, 0))
    y, h = pl.pallas_call(
        functools.partial(_ssm_seq_kernel, chunk=chunk), grid=(n_seq,),
        in_specs=[seq] + [whole(o) for o in ops],
        out_specs=[seq, pl.BlockSpec((1, g, 2 * p), lambda b: (b, 0, 0))],
        out_shape=[jax.ShapeDtypeStruct(u.shape, F32), jax.ShapeDtypeStruct((n_seq, g, 2 * p), F32)],
        scratch_shapes=[pltpu.VMEM((GROUPS_PER_TILE, t // chunk, lw), BF16),
                        pltpu.VMEM((GROUPS_PER_TILE, t // chunk, lw), F32)],
        compiler_params=_params("arbitrary"), name="ssm_seq")(u, *ops)
    return y, h[:, :, :p], h[:, :, p:]


FF_CHUNK = 1024


def _post_kernel(x_ref, att_ref, ys_ref, u_ref, p_ref, lng_ref, lnb_ref, dsk_ref, wglu_ref, bglu_ref,
                 woa_ref, wos_ref, l1g_ref, l1b_ref, wup_ref, wdn_ref, wpe_ref, wpg_ref, bpg_ref,
                 l2g_ref, l2b_ref, o_ref):
    h = _layer_norm(x_ref[0], lng_ref[...], lnb_ref[...])
    y = _gelu_tanh(_load_lane_tiles(ys_ref) + dsk_ref[...] * _load_lane_tiles(u_ref))
    y = y * _sigmoid(_dot(y.astype(BF16), wglu_ref[...]) + bglu_ref[...])
    mix = _dot(att_ref[0], woa_ref[...]) + _dot(y.astype(BF16), wos_ref[...])
    h1 = _layer_norm(ALPHA * h + mix, l1g_ref[...], l1b_ref[...])
    h1b = h1.astype(BF16)
    e = _sigmoid(_dot(h1b, wpg_ref[...]) + bpg_ref[...]) * _dot(p_ref[0].astype(BF16), wpe_ref[...])
    acc = ALPHA * h1 + e
    for c in range(D_FF // FF_CHUNK):
        cs = slice(c * FF_CHUNK, (c + 1) * FF_CHUNK)
        a = jnp.maximum(_dot(h1b, wup_ref[:, cs]), 0.0)
        acc = acc + _dot((a * a).astype(BF16), wdn_ref[cs, :])
    o_ref[0] = _layer_norm(acc, l2g_ref[...], l2b_ref[...])


def _post_mixer(x, att, ys, u, p, weights, *, tm):
    nb, t, _ = x.shape
    tok = lambda w: pl.BlockSpec((1, tm, w), lambda b, i: (b, i, 0))
    const = lambda a: pl.BlockSpec(a.shape, lambda b, i: (0, 0), pipeline_mode=pl.Buffered(1))
    return pl.pallas_call(
        _post_kernel, grid=(nb, t // tm),
        in_specs=[tok(D_MODEL), tok(ATTN_WIDTH), _lane_tiles_spec(tm, SSM_WIDTH), _lane_tiles_spec(tm, SSM_WIDTH),
                  tok(PLE_DIM)] + [const(w) for w in weights],
        out_specs=tok(D_MODEL), out_shape=jax.ShapeDtypeStruct((nb, t, D_MODEL), F32),
        compiler_params=_params("arbitrary", "arbitrary"), name="post_mixer")(x, att, ys, u, p, *weights)


def kernel(x_prompt, x_sample, cache_k, cache_v, cache_logf, state_re, state_im, page_table, p_prompt, p_sample,
           ln_in_g, ln_in_b, w_in, b_f, lam_re, lam_im, log_dt, b_re, b_im, c_re, c_im, d_skip, w_glu, b_glu,
           w_out, ln1_g, ln1_b, w_up, w_down, w_pe, w_pg, b_pg, ln2_g, ln2_b):
    assert w_in.shape[0] == 1, "one trunk layer"
    nb, t, _ = x_prompt.shape
    db, nq, _ = x_sample.shape
    row = lambda a: a.reshape(1, -1)
    post_w = [row(ln_in_g), row(ln_in_b), row(d_skip[0]), w_glu[0].astype(BF16), row(b_glu[0]),
              w_out[0, :ATTN_WIDTH].astype(BF16), w_out[0, ATTN_WIDTH:].astype(BF16), row(ln1_g[0]), row(ln1_b[0]),
              w_up[0].astype(BF16), w_down[0].astype(BF16), w_pe[0].astype(BF16), w_pg[0].astype(BF16),
              row(b_pg[0]), row(ln2_g[0]), row(ln2_b[0])]
    ssm_par = (lam_re[0], lam_im[0], log_dt[0], b_re[0], b_im[0], c_re[0], c_im[0])

    u, kT, vT, lfT, cT, qTb, kb, vTb, cp = _in_proj(x_prompt, ln_in_g, ln_in_b, w_in[0], b_f[0],
                                                     tm=IN_PROJ_ROWS, q_scale=QK_SCALE * LOG2E, attn_layouts=True)
    att = _fox_prompt(qTb, kb, cp, vTb, cT, tq=ATTN_TILE)
    ssm_ops, ssm_ops_sample = _ssm_prep(*ssm_par, chunk=SSM_CHUNK, small=nq)
    ys, sr, si = _ssm_seq(u, ssm_ops, chunk=SSM_CHUNK)
    y_prompt = _post_mixer(x_prompt, att, ys, u, p_prompt[0], post_w, tm=POST_ROWS)
    heads_last = lambda a: jnp.transpose(a.reshape(nb, N_HEADS, HEAD_DIM, t), (0, 3, 1, 2))[None]
    prompt_out = (heads_last(kT), heads_last(vT), jnp.swapaxes(lfT, 1, 2)[None], sr[None], si[None])

    us, qs, ks, vs, logfs = _in_proj(x_sample.reshape(1, db * nq, D_MODEL), ln_in_g, ln_in_b, w_in[0], b_f[0],
                                     tm=db * nq, q_scale=QK_SCALE, attn_layouts=False)
    seq = lambda a: a.reshape(db, nq, a.shape[-1])
    att_s = _fox_sample(seq(qs), seq(ks), seq(vs), seq(logfs), cache_k[0], cache_v[0], cache_logf[0], page_table)
    ys_s, sr_s, si_s = _ssm_step(us, ssm_ops_sample, (state_re[0], state_im[0]), chunk=nq)
    flat = lambda a: a.reshape(1, db * nq, a.shape[-1])
    y_sample = _post_mixer(flat(x_sample), flat(att_s).astype(BF16), ys_s, us, flat(p_sample[0]),
                           post_w, tm=db * nq).reshape(db, nq, D_MODEL)
    sample_out = (ks.reshape(1, db, nq, N_HEADS, HEAD_DIM), vs.reshape(1, db, nq, N_HEADS, HEAD_DIM),
                  logfs.reshape(1, db, nq, N_HEADS), sr_s[None], si_s[None])
    return (y_prompt, y_sample) + prompt_out + sample_out
```

```python
import functools
import math

import jax
import jax.numpy as jnp
from jax import lax
from jax.experimental import pallas as pl
from jax.experimental.pallas import tpu as pltpu

F32 = jnp.float32
BF16 = jnp.bfloat16

D_MODEL = 1024
ATTN_WIDTH = 512
SSM_WIDTH = 512
HEAD_DIM = 64
N_HEADS = 8
SSM_GROUP = 16
N_GROUPS = 32
STATE_DIM = 64
D_FF = 4096
PLE_DIM = 256
PAGE_SIZE = 128
ALPHA = 2.0 ** 0.25
LN_EPS = 1e-5
NEG_INF = -1e30
QK_SCALE = HEAD_DIM ** -0.5
LOG2E = math.log2(math.e)

VMEM_LIMIT_BYTES = 56 * 1024 * 1024

IN_PROJ_ROWS = 512
ATTN_TILE = 256
SSM_CHUNK = 16
POST_ROWS = 512

_NT = (((1,), (1,)), ((), ()))


def _params(*sem):
    return pltpu.CompilerParams(dimension_semantics=sem, vmem_limit_bytes=VMEM_LIMIT_BYTES)


def _const_spec(shape):
    return pl.BlockSpec(shape, lambda *_: (0,) * len(shape))


def _layer_norm(x, g, b):
    mu = jnp.mean(x, axis=-1, keepdims=True)
    xc = x - mu
    var = jnp.mean(xc * xc, axis=-1, keepdims=True)
    return xc * lax.rsqrt(var + LN_EPS) * g + b


def _log_sigmoid(x):
    return jnp.minimum(x, 0.0) - jnp.log1p(jnp.exp(-jnp.abs(x)))


def _sigmoid(x):
    return 1.0 / (1.0 + jnp.exp(-x))


def _gelu_tanh(x):
    return 0.5 * x * (1.0 + jnp.tanh(math.sqrt(2.0 / math.pi) * (x + 0.044715 * (x * x * x))))


def _dot(a, b):
    return jnp.dot(a, b, preferred_element_type=F32)


def _dot_nt(a, b):
    return lax.dot_general(a, b, _NT, preferred_element_type=F32)


def _split3(x):
    hi = x.astype(BF16)
    r1 = x - hi.astype(F32)
    mid = r1.astype(BF16)
    lo = (r1 - mid.astype(F32)).astype(BF16)
    return hi, mid, lo


LANES = 128


def _lane_tiles_spec(tm, width):
    return pl.BlockSpec((1, width // LANES, tm, LANES), lambda b, i: (b, 0, i, 0))


def _store_lane_tiles(ref, x):
    for v in range(x.shape[-1] // LANES):
        ref[0, v] = x[:, v * LANES:(v + 1) * LANES]


def _load_lane_tiles(ref):
    return jnp.concatenate([ref[0, v] for v in range(ref.shape[1])], axis=-1)


def _dot_exact01(x, m01):
    hi, mid, lo = _split3(x)
    return _dot(hi, m01) + _dot(mid, m01) + _dot(lo, m01)


N_BIAS_ROWS = 128
ONES_ROW = 3 * N_HEADS
SUM_ROWS = 16


def _in_proj_kernel(*refs, tm, q_scale, attn_layouts):
    x_ref, g_ref, b_ref, wu_ref = refs[:4]
    if attn_layouts:
        (wqT_ref, wkT_ref, wvT_ref, wfT_ref, bfc_ref, tri_ref,
         u_ref, kT_ref, vT_ref, lfT_ref, cT_ref, qTb_ref, kb_ref, vTb_ref, cp_ref, carry_ref) = refs[4:]
    else:
        wq_ref, wk_ref, wv_ref, wf_ref, bf_ref, u_ref, q_ref, k_ref, v_ref, lf_ref = refs[4:]
    hb = _layer_norm(x_ref[0], g_ref[...], b_ref[...]).astype(BF16)
    _store_lane_tiles(u_ref, _dot(hb, wu_ref[...]))
    if attn_layouts:
        qTb_ref[0] = (_dot_nt(wqT_ref[...], hb) * q_scale).astype(BF16)
        kT = _dot_nt(wkT_ref[...], hb)
        kT_ref[0] = kT
        kb_ref[0] = kT.T.astype(BF16)
        vT = _dot_nt(wvT_ref[...], hb)
        vT_ref[0] = vT
        vTb_ref[0] = vT.astype(BF16)
        lfT = _log_sigmoid(_dot_nt(wfT_ref[...], hb) + bfc_ref[...])
        lfT_ref[0] = lfT

        @pl.when(pl.program_id(1) == 0)
        def _():
            carry_ref[...] = jnp.zeros_like(carry_ref)

        c = carry_ref[...] + _dot_exact01(lfT, tri_ref[...])
        cT_ref[0] = c
        carry_ref[...] = c[:, tm - 1:tm]
        hi, mid, lo = _split3(c * LOG2E)
        row = lax.broadcasted_iota(jnp.int32, (N_BIAS_ROWS - ONES_ROW, tm), 0)
        ones = jnp.where(row < 3, 1.0, 0.0)
        pieces = jnp.concatenate([hi.astype(F32), mid.astype(F32), lo.astype(F32), ones], axis=0)
        cp_ref[0] = pieces.T.astype(BF16)
    else:
        q_ref[0] = _dot(hb, wq_ref[...]) * q_scale
        k_ref[0] = _dot(hb, wk_ref[...])
        v_ref[0] = _dot(hb, wv_ref[...])
        lf_ref[0] = _log_sigmoid(_dot(hb, wf_ref[...]) + bf_ref[...])


def _in_proj(x3, ln_g, ln_b, w_in, b_f, *, tm, q_scale, attn_layouts):
    nb, t, _ = x3.shape
    wq = w_in[:, :ATTN_WIDTH].astype(BF16)
    wk = w_in[:, ATTN_WIDTH:2 * ATTN_WIDTH].astype(BF16)
    wv = w_in[:, 2 * ATTN_WIDTH:3 * ATTN_WIDTH].astype(BF16)
    wf = w_in[:, 3 * ATTN_WIDTH:3 * ATTN_WIDTH + N_HEADS].astype(BF16)
    wu = w_in[:, 3 * ATTN_WIDTH + N_HEADS:].astype(BF16)
    tok = lambda w: pl.BlockSpec((1, tm, w), lambda b, i: (b, i, 0))
    feat = lambda w: pl.BlockSpec((1, w, tm), lambda b, i: (b, 0, i))
    sds = jax.ShapeDtypeStruct
    ins = [x3, ln_g.reshape(1, -1), ln_b.reshape(1, -1), wu]
    in_specs = [tok(D_MODEL), _const_spec((1, D_MODEL)), _const_spec((1, D_MODEL)), _const_spec((D_MODEL, SSM_WIDTH))]
    out_shape = [sds((nb, SSM_WIDTH // LANES, t, LANES), F32)]
    out_specs = [_lane_tiles_spec(tm, SSM_WIDTH)]
    scratch = []
    if attn_layouts:
        tri = (jnp.arange(tm)[:, None] <= jnp.arange(tm)[None, :]).astype(BF16)
        ins += [wq.T, wk.T, wv.T, wf.T, b_f.reshape(N_HEADS, 1), tri]
        in_specs += [_const_spec((ATTN_WIDTH, D_MODEL))] * 3
        in_specs += [_const_spec((N_HEADS, D_MODEL)), _const_spec((N_HEADS, 1)), _const_spec((tm, tm))]
        out_shape += [sds((nb, ATTN_WIDTH, t), F32), sds((nb, ATTN_WIDTH, t), F32),
                      sds((nb, N_HEADS, t), F32), sds((nb, N_HEADS, t), F32),
                      sds((nb, ATTN_WIDTH, t), BF16), sds((nb, t, ATTN_WIDTH), BF16), sds((nb, ATTN_WIDTH, t), BF16),
                      sds((nb, t, N_BIAS_ROWS), BF16)]
        out_specs += [feat(ATTN_WIDTH), feat(ATTN_WIDTH), feat(N_HEADS), feat(N_HEADS),
                      feat(ATTN_WIDTH), tok(ATTN_WIDTH), feat(ATTN_WIDTH), tok(N_BIAS_ROWS)]
        scratch = [pltpu.VMEM((N_HEADS, 1), F32)]
    else:
        ins += [wq, wk, wv, wf, b_f.reshape(1, N_HEADS)]
        in_specs += [_const_spec((D_MODEL, ATTN_WIDTH))] * 3 + [_const_spec((D_MODEL, N_HEADS)), _const_spec((1, N_HEADS))]
        out_shape += [sds((nb, t, ATTN_WIDTH), F32)] * 3 + [sds((nb, t, N_HEADS), F32)]
        out_specs += [tok(ATTN_WIDTH)] * 3 + [tok(N_HEADS)]
    return pl.pallas_call(
        functools.partial(_in_proj_kernel, tm=tm, q_scale=q_scale, attn_layouts=attn_layouts),
        grid=(nb, t // tm), in_specs=in_specs, out_specs=out_specs, out_shape=out_shape,
        scratch_shapes=scratch, compiler_params=_params("arbitrary", "arbitrary"),
        name="in_proj_prompt" if attn_layouts else "in_proj_sample")(*ins)


def _fox_prompt_kernel(qT_ref, k_ref, cp_ref, vT_ref, cT_ref, o_ref, bq_ref, m_ref, acc_ref, s0_ref, s1_ref, *, tq):
    qi = pl.program_id(1)
    q0 = pl.multiple_of(qi * tq, tq)
    heads = [slice(h * HEAD_DIM, (h + 1) * HEAD_DIM) for h in range(N_HEADS)]
    pairs = [slice(2 * HEAD_DIM * (h // 2), 2 * HEAD_DIM * (h // 2 + 1)) for h in range(N_HEADS)]

    c0 = cT_ref[0, :, pl.ds(q0, tq)][:, 0:1] * LOG2E
    c0_pieces = [p.astype(F32) for p in _split3(c0)]
    r = lax.broadcasted_iota(jnp.int32, (2 * HEAD_DIM, tq), 0)
    for h in range(N_HEADS):
        mine = (r < HEAD_DIM) if h % 2 == 0 else (r >= HEAD_DIM)
        bq_ref[h, 0:2 * HEAD_DIM, :] = jnp.where(mine, qT_ref[0, pairs[h], :], jnp.zeros((), BF16))
        sel = jnp.where((r == h) | (r == N_HEADS + h) | (r == 2 * N_HEADS + h), -1.0, 0.0)
        for j, piece in enumerate(c0_pieces):
            sel = jnp.where(r == ONES_ROW + j, piece[h:h + 1, :], sel)
        bq_ref[h, 2 * HEAD_DIM:, :] = sel.astype(BF16)

    key_row = lax.broadcasted_iota(jnp.int32, (tq, tq), 0)
    qry_col = lax.broadcasted_iota(jnp.int32, (tq, tq), 1)

    m_ref[...] = jnp.full_like(m_ref, NEG_INF)
    acc_ref[...] = jnp.zeros_like(acc_ref)
    ones = jnp.ones((SUM_ROWS, tq), BF16)

    s_refs = (s0_ref, s1_ref)

    def scores(kt, h, buf):
        ks = pl.multiple_of(kt * tq, tq)
        a = jnp.concatenate([k_ref[0, pl.ds(ks, tq), pairs[h]], cp_ref[0, pl.ds(ks, tq), :]], axis=1)
        s_refs[buf][h] = _dot(a, bq_ref[h])

    def accumulate(kt, h, buf, diagonal):
        ks = pl.multiple_of(kt * tq, tq)
        s = s_refs[buf][h]
        if diagonal:
            s = jnp.where(key_row <= qry_col, s, NEG_INF)
        m_new = jnp.maximum(m_ref[h], jnp.max(s, axis=0, keepdims=True))
        alpha = jnp.exp2(m_ref[h] - m_new)
        m_ref[h] = m_new
        p = jnp.exp2((s - m_new).astype(BF16))
        v1 = jnp.concatenate([vT_ref[0, heads[h], pl.ds(ks, tq)], ones], axis=0)
        acc_ref[h] = alpha * acc_ref[h] + _dot(v1, p)

    def stage(kt, cur, *, diagonal=False, issue_next=True):
        for h in range(N_HEADS + 1):
            if issue_next and h < N_HEADS:
                scores(kt + 1, h, 1 - cur)
            if h >= 1:
                accumulate(kt, h - 1, cur, diagonal)

    for h in range(N_HEADS):
        scores(0, h, 0)

    def body(j, carry):
        stage(2 * j, 0)
        stage(2 * j + 1, 1)
        return carry

    lax.fori_loop(0, qi // 2, body, 0)

    @pl.when(qi % 2 == 0)
    def _():
        stage(qi, 0, diagonal=True, issue_next=False)

    @pl.when(qi % 2 == 1)
    def _():
        stage(qi - 1, 0)
        stage(qi, 1, diagonal=True, issue_next=False)

    outs = [acc_ref[h, 0:HEAD_DIM, :] / acc_ref[h, HEAD_DIM:HEAD_DIM + 1, :] for h in range(N_HEADS)]
    o_ref[0] = jnp.concatenate(outs, axis=0).T.astype(o_ref.dtype)


def _fox_prompt(qT, k, cp, vT, cT, *, tq):
    nb, t, _ = k.shape
    assert N_BIAS_ROWS == 2 * HEAD_DIM
    whole_f = lambda w: pl.BlockSpec((1, w, t), lambda b, i: (b, 0, 0))
    whole_t = lambda w: pl.BlockSpec((1, t, w), lambda b, i: (b, 0, 0))
    return pl.pallas_call(
        functools.partial(_fox_prompt_kernel, tq=tq),
        grid=(nb, t // tq),
        in_specs=[pl.BlockSpec((1, ATTN_WIDTH, tq), lambda b, i: (b, 0, i)),
                  whole_t(ATTN_WIDTH), whole_t(N_BIAS_ROWS), whole_f(ATTN_WIDTH), whole_f(N_HEADS)],
        out_specs=pl.BlockSpec((1, tq, ATTN_WIDTH), lambda b, i: (b, i, 0)),
        out_shape=jax.ShapeDtypeStruct((nb, t, ATTN_WIDTH), BF16),
        scratch_shapes=[pltpu.VMEM((N_HEADS, 2 * HEAD_DIM + N_BIAS_ROWS, tq), BF16),
                        pltpu.VMEM((N_HEADS, 1, tq), F32), pltpu.VMEM((N_HEADS, HEAD_DIM + SUM_ROWS, tq), F32),
                        pltpu.VMEM((N_HEADS, tq, tq), F32), pltpu.VMEM((N_HEADS, tq, tq), F32)],
        compiler_params=_params("arbitrary", "arbitrary"), name="fox_prompt")(qT, k, cp, vT, cT)


SAMPLE_SEQS_PER_STEP = 2


def _fox_sample_kernel(pt_ref, q_ref, kn_ref, vn_ref, lfn_ref, sl_ref, *refs, n_pages, nq, n_seq):
    del pt_ref
    o_ref = refs[3 * n_seq * n_pages]
    for e in range(n_seq):
        pages = [refs[(kind * n_seq + e) * n_pages:(kind * n_seq + e + 1) * n_pages] for kind in range(3)]
        _fox_sample_one(e, q_ref, kn_ref, vn_ref, lfn_ref, sl_ref, *pages, o_ref, n_pages=n_pages, nq=nq)


def _fox_sample_one(e, q_ref, kn_ref, vn_ref, lfn_ref, sl_ref, k_refs, v_refs, pf_refs, o_ref, *, n_pages, nq):
    nr = nq * N_HEADS
    head_of_lane = lax.broadcasted_iota(jnp.int32, (N_HEADS, ATTN_WIDTH), 1) // HEAD_DIM
    own = head_of_lane == lax.broadcasted_iota(jnp.int32, (N_HEADS, ATTN_WIDTH), 0)
    tok = slice(e * nq, (e + 1) * nq)
    q = q_ref[tok, :]
    qbd32 = jnp.concatenate([jnp.where(own, q[i:i + 1, :], 0.0) for i in range(nq)], axis=0)
    qbd = qbd32.astype(BF16)
    rep = lambda e: jnp.concatenate([e] * nq, axis=0)
    rows = lambda j: slice(j * N_HEADS, (j + 1) * N_HEADS)

    lfn = lfn_ref[e]
    e_new = rep(_dot_exact01(lfn, sl_ref[...]))
    kn, vn = kn_ref[tok, :], vn_ref[tok, :]
    qidx = lax.broadcasted_iota(jnp.int32, (nr, 1), 0) // N_HEADS
    s_new = [jnp.where(qidx >= j, jnp.sum(qbd32 * kn[j:j + 1, :], axis=-1, keepdims=True) + e_new[:, j:j + 1], NEG_INF)
             for j in range(nq)]

    pf_all = jnp.concatenate([r[0, 0] for r in pf_refs], axis=0)
    sfx_all = _dot_exact01(pf_all, sl_ref[...])
    tot_all = sfx_all[:, 0:1] + pf_all[:, 0:1]
    tail = jnp.sum(lfn, axis=-1, keepdims=True)
    scores = [None] * n_pages
    for j in reversed(range(n_pages)):
        kt = k_refs[j][0, 0].reshape(ATTN_WIDTH, PAGE_SIZE).astype(BF16)
        scores[j] = _dot(qbd, kt) + rep(sfx_all[rows(j)] + tail)
        tail = tail + tot_all[rows(j)]
    s_all = jnp.concatenate(scores, axis=1)
    m = jnp.max(s_all, axis=-1, keepdims=True)
    for s in s_new:
        m = jnp.maximum(m, s)
    p_all = jnp.exp(s_all - m)
    l = jnp.sum(p_all, axis=-1, keepdims=True)
    acc = jnp.zeros((nr, ATTN_WIDTH), F32)
    for j, s in enumerate(s_new):
        p = jnp.exp(s - m)
        l = l + p
        acc = acc + p * vn[j:j + 1, :]
    p_all = p_all.astype(BF16)
    for j in range(n_pages):
        vt = v_refs[j][0, 0].reshape(ATTN_WIDTH, PAGE_SIZE).astype(BF16)
        acc = acc + _dot_nt(p_all[:, j * PAGE_SIZE:(j + 1) * PAGE_SIZE], vt)
    acc = acc / l
    for i in range(nq):
        o_ref[e * nq + i:e * nq + i + 1, :] = jnp.sum(jnp.where(own, acc[i * N_HEADS:(i + 1) * N_HEADS], 0.0), axis=0, keepdims=True)


def _fox_sample(q, k_new, v_new, logf_new, cache_k, cache_v, cache_logf, page_table):
    db, nq, _ = logf_new.shape
    n_pages = page_table.shape[1]
    kc = jnp.transpose(cache_k, (0, 2, 3, 1))
    vc = jnp.transpose(cache_v, (0, 2, 3, 1))
    pf = jnp.transpose(cache_logf, (0, 2, 1))
    lfn_t = jnp.swapaxes(jnp.pad(logf_new, ((0, 0), (0, PAGE_SIZE - nq), (0, 0))), 1, 2)
    strict_lower = (jnp.arange(PAGE_SIZE)[:, None] > jnp.arange(PAGE_SIZE)[None, :]).astype(BF16)
    ns = SAMPLE_SEQS_PER_STEP
    seq = lambda r, w: pl.BlockSpec((ns, r, w), lambda b, pt: (b, 0, 0))
    page = lambda e, j: pl.BlockSpec((1, 1, N_HEADS, HEAD_DIM, PAGE_SIZE), lambda b, pt: (0, pt[ns * b + e, j], 0, 0, 0))
    page_f = lambda e, j: pl.BlockSpec((1, 1, N_HEADS, PAGE_SIZE), lambda b, pt: (0, pt[ns * b + e, j], 0, 0))
    every = [(e, j) for e in range(ns) for j in range(n_pages)]
    toks = pl.BlockSpec((ns * nq, ATTN_WIDTH), lambda b, pt: (b, 0))
    in_specs = [toks, toks, toks, seq(N_HEADS, PAGE_SIZE),
                pl.BlockSpec((PAGE_SIZE, PAGE_SIZE), lambda b, pt: (0, 0))]
    in_specs += [page(e, j) for e, j in every] * 2 + [page_f(e, j) for e, j in every]
    return pl.pallas_call(
        functools.partial(_fox_sample_kernel, n_pages=n_pages, nq=nq, n_seq=ns),
        grid_spec=pltpu.PrefetchScalarGridSpec(
            num_scalar_prefetch=1, grid=(db // ns,), in_specs=in_specs, out_specs=toks),
        out_shape=jax.ShapeDtypeStruct((db * nq, ATTN_WIDTH), F32),
        compiler_params=_params("arbitrary"), name="fox_sample")(
            page_table, q, k_new, v_new, lfn_t, strict_lower,
            *([kc[None]] * len(every)), *([vc[None]] * len(every)), *([pf[None]] * len(every)))


PREP_GROUPS = 2

def _cmul(ar, ai, br, bi):
    return ar * br - ai * bi, ar * bi + ai * br


def _cpow_by_bits(ar, ai, n, nbits):
    pr = jnp.ones(jnp.broadcast_shapes(ar.shape, n.shape), F32)
    pi = jnp.zeros_like(pr)
    for b in range(nbits):
        bit = ((n >> b) & 1) == 1
        fr = jnp.where(bit, ar, 1.0)
        fi = jnp.where(bit, ai, 0.0)
        pr, pi = _cmul(pr, pi, fr, fi)
        ar, ai = _cmul(ar, ai, ar, ai)
    return pr, pi


def _ssm_prep_kernel(*refs, chunk, small, n_groups):
    for e in range(n_groups):
        _ssm_prep_group(*[r.at[pl.ds(e, 1)] for r in refs], chunk=chunk, small=small)


def _ssm_prep_group(ldt_ref, lr_ref, li_ref, bre_ref, bim_ref, cre_ref, cim_ref,
                    toep_ref, ctl_ref, obs_ref, al_ref,
                    toeps_ref, ctlrs_ref, ctlis_ref, obsrs_ref, obsis_ref, asr_ref, asi_ref, *, chunk, small):
    lw, sw = chunk * SSM_GROUP, small * SSM_GROUP
    p = STATE_DIM
    nbits = chunk.bit_length()
    dt = jnp.exp(ldt_ref[0])
    lr_row, li_row = lr_ref[0], li_ref[0]

    def discretise(lr, li):
        mag = jnp.exp(lr * dt)
        return mag * jnp.cos(li * dt), mag * jnp.sin(li * dt)

    def column(row):
        eye = lax.broadcasted_iota(jnp.int32, (p, p), 0) == lax.broadcasted_iota(jnp.int32, (p, p), 1)
        return jnp.sum(jnp.where(eye, row, 0.0), axis=-1, keepdims=True)

    lr, li = column(lr_row), column(li_row)
    ar, ai = discretise(lr, li)
    den = lr * lr + li * li
    nr, ni = ar - 1.0, ai
    cr, ci = (nr * lr + ni * li) / den, (ni * lr - nr * li) / den
    btr = jnp.concatenate([bre_ref[0]] * (2 * lw // LANES), axis=1)
    bti = jnp.concatenate([bim_ref[0]] * (2 * lw // LANES), axis=1)
    bbr, bbi = _cmul(cr, ci, btr, bti)
    sig = lax.broadcasted_iota(jnp.int32, (1, 2 * lw), 1) // SSM_GROUP
    er, ei = _cpow_by_bits(ar, ai, jnp.maximum(chunk - 1 - sig, 0), nbits)
    hr, hi = _cmul(er, ei, bbr, bbi)
    hr = jnp.where(sig < chunk, hr, 0.0)
    hi = jnp.where(sig < chunk, hi, 0.0)
    ctl_ref[0] = jnp.concatenate([hr[:, :lw], hi[:, :lw]], axis=0).astype(BF16)
    ctlrs_ref[0] = hr[:, lw - sw:lw]
    ctlis_ref[0] = hi[:, lw - sw:lw]
    c_re, c_im = cre_ref[0], cim_ref[0]
    hp = lax.Precision.HIGHEST
    k_all = (jnp.dot(c_re, hr, precision=hp, preferred_element_type=F32)
             - jnp.dot(c_im, hi, precision=hp, preferred_element_type=F32))
    for t in range(chunk):
        off = (chunk - 1 - t) * SSM_GROUP
        toep_ref[0, t * SSM_GROUP:(t + 1) * SSM_GROUP, :] = k_all[:, off:off + lw].astype(BF16)
        if t < small:
            toeps_ref[0, t * SSM_GROUP:(t + 1) * SSM_GROUP, :] = k_all[:, off:off + sw]

    arr, air = discretise(lr_row, li_row)
    tp1 = lax.broadcasted_iota(jnp.int32, (lw, 1), 0) // SSM_GROUP + 1
    pr, pi = _cpow_by_bits(arr, air, tp1, nbits)
    c_r = jnp.concatenate([c_re] * chunk, axis=0)
    c_i = jnp.concatenate([c_im] * chunk, axis=0)
    obs_re = c_r * pr - c_i * pi
    obs_im = -(c_r * pi + c_i * pr)
    obs_ref[0] = jnp.concatenate([obs_re, obs_im], axis=1).astype(BF16)
    obsrs_ref[0] = obs_re[:sw]
    obsis_ref[0] = obs_im[:sw]
    n = 1
    while n < chunk:
        arr, air = _cmul(arr, air, arr, air)
        n *= 2
        if n == small:
            asr_ref[0] = arr
            asi_ref[0] = air
    al_ref[0] = jnp.concatenate([jnp.concatenate([arr, arr], axis=1), jnp.concatenate([-air, air], axis=1)], axis=0)


def _ssm_prep(lam_re, lam_im, log_dt, b_re, b_im, c_re, c_im, *, chunk, small):
    assert chunk & (chunk - 1) == 0 and small & (small - 1) == 0 and 1 < small < chunk
    g, p, j = N_GROUPS, STATE_DIM, SSM_GROUP
    lw, sw = chunk * j, small * j
    lane_tile = lambda b: jnp.tile(b, (1, 1, LANES // j))
    ins = [log_dt.reshape(g, 1, 1), lam_re.reshape(g, 1, p), lam_im.reshape(g, 1, p), lane_tile(b_re), lane_tile(b_im),
           c_re, c_im]
    grp = lambda a, b: pl.BlockSpec((PREP_GROUPS, a, b), lambda i: (i, 0, 0))
    in_specs = [grp(1, 1), grp(1, p), grp(1, p), grp(p, LANES), grp(p, LANES), grp(j, p), grp(j, p)]
    shapes = [((lw, lw), BF16), ((2 * p, lw), BF16), ((lw, 2 * p), BF16), ((2, 2 * p), F32),
              ((sw, sw), F32), ((p, sw), F32), ((p, sw), F32), ((sw, p), F32), ((sw, p), F32), ((1, p), F32), ((1, p), F32)]
    outs = pl.pallas_call(
        functools.partial(_ssm_prep_kernel, chunk=chunk, small=small, n_groups=PREP_GROUPS),
        grid=(g // PREP_GROUPS,), in_specs=in_specs,
        out_specs=[grp(*s) for s, _ in shapes],
        out_shape=[jax.ShapeDtypeStruct((g,) + s, d) for s, d in shapes],
        compiler_params=_params("arbitrary"), name="ssm_prep")(*ins)
    return outs[:4], outs[4:]


def _ssm_step_kernel(u_ref, toep_ref, ctlr_ref, ctli_ref, obsr_ref, obsi_ref, ar_ref, ai_ref, h0r_ref, h0i_ref,
                     y_ref, hr_ref, hi_ref, *, chunk):
    n = GROUPS_PER_TILE
    n_seq = h0r_ref.shape[1]
    lw = chunk * SSM_GROUP
    zero = jnp.zeros((n_seq, LANES), F32)
    xs = [u_ref[0, 0, pl.ds(t, n_seq, stride=chunk), :] if t < chunk else zero for t in range(n)]
    ys = []
    for e, x in enumerate(_block_transpose(xs)):
        u = x[:, :lw].astype(BF16)
        h0r, h0i = h0r_ref[e], h0i_ref[e]
        er, ei = _cmul(ar_ref[e], ai_ref[e], h0r, h0i)
        hr_ref[e] = _dot_nt(u, ctlr_ref[e].astype(BF16)) + er
        hi_ref[e] = _dot_nt(u, ctli_ref[e].astype(BF16)) + ei
        y = (_dot_nt(u, toep_ref[e].astype(BF16)) + _dot_nt(h0r.astype(BF16), obsr_ref[e].astype(BF16))
             + _dot_nt(h0i.astype(BF16), obsi_ref[e].astype(BF16)))
        ys.append(jnp.concatenate([y, jnp.zeros((n_seq, LANES - lw), F32)], axis=1))
    for t, y in enumerate(_block_transpose(ys)[:chunk]):
        y_ref[0, 0, pl.ds(t, n_seq, stride=chunk), :] = y


def _ssm_step(u, ops, h0, *, chunk):
    n_seq = h0[0].shape[0]
    g, p, n = N_GROUPS, STATE_DIM, GROUPS_PER_TILE
    lw = chunk * SSM_GROUP
    assert lw <= LANES and u.shape == (1, g // n, n_seq * chunk, LANES)
    grp = lambda a, b: pl.BlockSpec((n, a, b), lambda v: (v, 0, 0))
    tile = pl.BlockSpec((1, 1, n_seq * chunk, LANES), lambda v: (0, v, 0, 0))
    y, hr, hi = pl.pallas_call(
        functools.partial(_ssm_step_kernel, chunk=chunk), grid=(g // n,),
        in_specs=[tile, grp(lw, lw), grp(p, lw), grp(p, lw), grp(lw, p), grp(lw, p), grp(1, p), grp(1, p),
                  grp(n_seq, p), grp(n_seq, p)],
        out_specs=[tile, grp(n_seq, p), grp(n_seq, p)],
        out_shape=[jax.ShapeDtypeStruct(u.shape, F32), jax.ShapeDtypeStruct((g, n_seq, p), F32),
                   jax.ShapeDtypeStruct((g, n_seq, p), F32)],
        compiler_params=_params("arbitrary"), name="ssm_step")(
            u, *ops, jnp.swapaxes(h0[0], 0, 1), jnp.swapaxes(h0[1], 0, 1))
    return y, jnp.swapaxes(hr, 0, 1), jnp.swapaxes(hi, 0, 1)


GROUPS_PER_TILE = LANES // SSM_GROUP
GROUPS_PER_TRIP = 8


def _block_transpose(xs):
    n = len(xs)
    blk = lax.broadcasted_iota(jnp.int32, xs[0].shape, 1) // SSM_GROUP
    xs = list(xs)
    d = n // 2
    while d:
        upper = (blk & d) != 0
        for i in range(n):
            if not i & d:
                lo, hi = xs[i], xs[i + d]
                xs[i] = jnp.where(upper, pltpu.roll(hi, SSM_GROUP * d, axis=1), lo)
                xs[i + d] = jnp.where(upper, hi, pltpu.roll(lo, LANES - SSM_GROUP * d, axis=1))
        d //= 2
    return xs


def _ssm_seq_kernel(u_ref, toep_ref, ctl_ref, obs_ref, al_ref, y_ref, h_ref, ug_ref, yg_ref, *, chunk):
    t = u_ref.shape[2]
    rows = t // chunk
    lw = chunk * SSM_GROUP
    n = GROUPS_PER_TILE
    p = STATE_DIM
    ridx = lax.broadcasted_iota(jnp.int32, (rows, 1), 0)

    def shifted(x, d):
        return jnp.where(ridx >= d, pltpu.roll(x, d, axis=0), 0.0)

    def group_tile(v, carry):
        for w in range(lw // LANES):
            xs = [u_ref[0, v, pl.ds(n * w + k, rows, stride=chunk), :] for k in range(n)]
            for k, x in enumerate(_block_transpose(xs)):
                ug_ref[k, :, w * LANES:(w + 1) * LANES] = x.astype(BF16)

        def groups(i, c):
            ks = [i * GROUPS_PER_TRIP + e for e in range(GROUPS_PER_TRIP)]
            gs = [v * n + k for k in ks]
            ugs = [ug_ref[k] for k in ks]
            ys = [_dot_nt(ug, toep_ref[g]) for ug, g in zip(ugs, gs)]
            hs = [_dot_nt(ug, ctl_ref[g]) for ug, g in zip(ugs, gs)]
            ars = [al_ref[g][0:1, :] for g in gs]
            ais = [al_ref[g][1:2, :] for g in gs]
            d = 1
            while d < rows:
                for e in range(GROUPS_PER_TRIP):
                    sh = shifted(hs[e], d)
                    hs[e] = hs[e] + ars[e] * sh + ais[e] * pltpu.roll(sh, p, axis=1)
                    ars[e], ais[e] = ars[e] * ars[e] - ais[e] * ais[e], 2.0 * ars[e] * ais[e]
                d *= 2
            for e, (k, g) in enumerate(zip(ks, gs)):
                h_ref[0, pl.ds(g, 1), :] = hs[e][rows - 1:rows, :]
                yg_ref[k] = ys[e] + _dot_nt(shifted(hs[e], 1).astype(BF16), obs_ref[g])
            return c

        lax.fori_loop(0, n // GROUPS_PER_TRIP, groups, 0)
        for w in range(lw // LANES):
            ys = [yg_ref[k, :, w * LANES:(w + 1) * LANES] for k in range(n)]
            for k, y in enumerate(_block_transpose(ys)):
                y_ref[0, v, pl.ds(n * w + k, rows, stride=chunk), :] = y
        return carry

    lax.fori_loop(0, N_GROUPS // n, group_tile, 0)


def _ssm_seq(u, ops, *, chunk):
    n_seq, _, t, _ = u.shape
    g, p = N_GROUPS, STATE_DIM
    lw = chunk * SSM_GROUP
    assert lw % LANES == 0 and chunk == GROUPS_PER_TILE * (lw // LANES) and SSM_WIDTH == g * SSM_GROUP
    whole = lambda a: pl.BlockSpec(a.shape, lambda b: (0, 0, 0), pipeline_mode=pl.Buffered(1))
    seq = pl.BlockSpec((1, SSM_WIDTH // LANES, t, LANES), lambda b: (b, 0, 0, 0))
    y, h = pl.pallas_call(
        functools.partial(_ssm_seq_kernel, chunk=chunk), grid=(n_seq,),
        in_specs=[seq] + [whole(o) for o in ops],
        out_specs=[seq, pl.BlockSpec((1, g, 2 * p), lambda b: (b, 0, 0))],
        out_shape=[jax.ShapeDtypeStruct(u.shape, F32), jax.ShapeDtypeStruct((n_seq, g, 2 * p), F32)],
        scratch_shapes=[pltpu.VMEM((GROUPS_PER_TILE, t // chunk, lw), BF16),
                        pltpu.VMEM((GROUPS_PER_TILE, t // chunk, lw), F32)],
        compiler_params=_params("arbitrary"), name="ssm_seq")(u, *ops)
    return y, h[:, :, :p], h[:, :, p:]


FF_CHUNK = 1024


def _post_kernel(x_ref, att_ref, ys_ref, u_ref, p_ref, lng_ref, lnb_ref, dsk_ref, wglu_ref, bglu_ref,
                 woa_ref, wos_ref, l1g_ref, l1b_ref, wup_ref, wdn_ref, wpe_ref, wpg_ref, bpg_ref,
                 l2g_ref, l2b_ref, o_ref):
    h = _layer_norm(x_ref[0], lng_ref[...], lnb_ref[...])
    y = _gelu_tanh(_load_lane_tiles(ys_ref) + dsk_ref[...] * _load_lane_tiles(u_ref))
    y = y * _sigmoid(_dot(y.astype(BF16), wglu_ref[...]) + bglu_ref[...])
    mix = _dot(att_ref[0], woa_ref[...]) + _dot(y.astype(BF16), wos_ref[...])
    h1 = _layer_norm(ALPHA * h + mix, l1g_ref[...], l1b_ref[...])
    h1b = h1.astype(BF16)
    e = _sigmoid(_dot(h1b, wpg_ref[...]) + bpg_ref[...]) * _dot(p_ref[0].astype(BF16), wpe_ref[...])
    acc = ALPHA * h1 + e
    for c in range(D_FF // FF_CHUNK):
        cs = slice(c * FF_CHUNK, (c + 1) * FF_CHUNK)
        a = jnp.maximum(_dot(h1b, wup_ref[:, cs]), 0.0)
        acc = acc + _dot((a * a).astype(BF16), wdn_ref[cs, :])
    o_ref[0] = _layer_norm(acc, l2g_ref[...], l2b_ref[...])


def _post_mixer(x, att, ys, u, p, weights, *, tm):
    nb, t, _ = x.shape
    tok = lambda w: pl.BlockSpec((1, tm, w), lambda b, i: (b, i, 0))
    const = lambda a: pl.BlockSpec(a.shape, lambda b, i: (0, 0), pipeline_mode=pl.Buffered(1))
    return pl.pallas_call(
        _post_kernel, grid=(nb, t // tm),
        in_specs=[tok(D_MODEL), tok(ATTN_WIDTH), _lane_tiles_spec(tm, SSM_WIDTH), _lane_tiles_spec(tm, SSM_WIDTH),
                  tok(PLE_DIM)] + [const(w) for w in weights],
        out_specs=tok(D_MODEL), out_shape=jax.ShapeDtypeStruct((nb, t, D_MODEL), F32),
        compiler_params=_params("arbitrary", "arbitrary"), name="post_mixer")(x, att, ys, u, p, *weights)


def kernel(x_prompt, x_sample, cache_k, cache_v, cache_logf, state_re, state_im, page_table, p_prompt, p_sample,
           ln_in_g, ln_in_b, w_in, b_f, lam_re, lam_im, log_dt, b_re, b_im, c_re, c_im, d_skip, w_glu, b_glu,
           w_out, ln1_g, ln1_b, w_up, w_down, w_pe, w_pg, b_pg, ln2_g, ln2_b):
    assert w_in.shape[0] == 1, "one trunk layer"
    nb, t, _ = x_prompt.shape
    db, nq, _ = x_sample.shape
    row = lambda a: a.reshape(1, -1)
    post_w = [row(ln_in_g), row(ln_in_b), row(d_skip[0]), w_glu[0].astype(BF16), row(b_glu[0]),
              w_out[0, :ATTN_WIDTH].astype(BF16), w_out[0, ATTN_WIDTH:].astype(BF16), row(ln1_g[0]), row(ln1_b[0]),
              w_up[0].astype(BF16), w_down[0].astype(BF16), w_pe[0].astype(BF16), w_pg[0].astype(BF16),
              row(b_pg[0]), row(ln2_g[0]), row(ln2_b[0])]
    ssm_par = (lam_re[0], lam_im[0], log_dt[0], b_re[0], b_im[0], c_re[0], c_im[0])

    u, kT, vT, lfT, cT, qTb, kb, vTb, cp = _in_proj(x_prompt, ln_in_g, ln_in_b, w_in[0], b_f[0],
                                                     tm=IN_PROJ_ROWS, q_scale=QK_SCALE * LOG2E, attn_layouts=True)
    att = _fox_prompt(qTb, kb, cp, vTb, cT, tq=ATTN_TILE)
    ssm_ops, ssm_ops_sample = _ssm_prep(*ssm_par, chunk=SSM_CHUNK, small=nq)
    ys, sr, si = _ssm_seq(u, ssm_ops, chunk=SSM_CHUNK)
    y_prompt = _post_mixer(x_prompt, att, ys, u, p_prompt[0], post_w, tm=POST_ROWS)
    heads_last = lambda a: jnp.transpose(a.reshape(nb, N_HEADS, HEAD_DIM, t), (0, 3, 1, 2))[None]
    prompt_out = (heads_last(kT), heads_last(vT), jnp.swapaxes(lfT, 1, 2)[None], sr[None], si[None])

    us, qs, ks, vs, logfs = _in_proj(x_sample.reshape(1, db * nq, D_MODEL), ln_in_g, ln_in_b, w_in[0], b_f[0],
                                     tm=db * nq, q_scale=QK_SCALE, attn_layouts=False)
    seq = lambda a: a.reshape(db, nq, a.shape[-1])
    att_s = _fox_sample(qs[0], ks[0], vs[0], seq(logfs), cache_k[0], cache_v[0], cache_logf[0], page_table)
    ys_s, sr_s, si_s = _ssm_step(us, ssm_ops_sample, (state_re[0], state_im[0]), chunk=nq)
    flat = lambda a: a.reshape(1, db * nq, a.shape[-1])
    y_sample = _post_mixer(flat(x_sample), att_s[None].astype(BF16), ys_s, us, flat(p_sample[0]),
                           post_w, tm=db * nq).reshape(db, nq, D_MODEL)
    sample_out = (ks.reshape(1, db, nq, N_HEADS, HEAD_DIM), vs.reshape(1, db, nq, N_HEADS, HEAD_DIM),
                  logfs.reshape(1, db, nq, N_HEADS), sr_s[None], si_s[None])
    return (y_prompt, y_sample) + prompt_out + sample_out
```

```python
import functools
import math

import jax
import jax.numpy as jnp
from jax import lax
from jax.experimental import pallas as pl
from jax.experimental.pallas import tpu as pltpu

F32 = jnp.float32
BF16 = jnp.bfloat16

D_MODEL = 1024
ATTN_WIDTH = 512
SSM_WIDTH = 512
HEAD_DIM = 64
N_HEADS = 8
SSM_GROUP = 16
N_GROUPS = 32
STATE_DIM = 64
D_FF = 4096
PLE_DIM = 256
PAGE_SIZE = 128
ALPHA = 2.0 ** 0.25
LN_EPS = 1e-5
NEG_INF = -1e30
QK_SCALE = HEAD_DIM ** -0.5
LOG2E = math.log2(math.e)

VMEM_LIMIT_BYTES = 56 * 1024 * 1024

IN_PROJ_ROWS = 512
ATTN_TILE = 256
SSM_CHUNK = 16
POST_ROWS = 512

_NT = (((1,), (1,)), ((), ()))


def _params(*sem):
    return pltpu.CompilerParams(dimension_semantics=sem, vmem_limit_bytes=VMEM_LIMIT_BYTES)


def _const_spec(shape):
    return pl.BlockSpec(shape, lambda *_: (0,) * len(shape))


def _layer_norm(x, g, b):
    mu = jnp.mean(x, axis=-1, keepdims=True)
    xc = x - mu
    var = jnp.mean(xc * xc, axis=-1, keepdims=True)
    return xc * lax.rsqrt(var + LN_EPS) * g + b


def _log_sigmoid(x):
    return jnp.minimum(x, 0.0) - jnp.log1p(jnp.exp(-jnp.abs(x)))


def _sigmoid(x):
    return 1.0 / (1.0 + jnp.exp(-x))


def _gelu_tanh(x):
    return 0.5 * x * (1.0 + jnp.tanh(math.sqrt(2.0 / math.pi) * (x + 0.044715 * (x * x * x))))


def _dot(a, b):
    return jnp.dot(a, b, preferred_element_type=F32)


def _dot_nt(a, b):
    return lax.dot_general(a, b, _NT, preferred_element_type=F32)


def _split3(x):
    hi = x.astype(BF16)
    r1 = x - hi.astype(F32)
    mid = r1.astype(BF16)
    lo = (r1 - mid.astype(F32)).astype(BF16)
    return hi, mid, lo


LANES = 128


def _lane_tiles_spec(tm, width):
    return pl.BlockSpec((1, width // LANES, tm, LANES), lambda b, i: (b, 0, i, 0))


def _store_lane_tiles(ref, x):
    for v in range(x.shape[-1] // LANES):
        ref[0, v] = x[:, v * LANES:(v + 1) * LANES]


def _load_lane_tiles(ref):
    return jnp.concatenate([ref[0, v] for v in range(ref.shape[1])], axis=-1)


def _dot_exact01(x, m01):
    hi, mid, lo = _split3(x)
    return _dot(hi, m01) + _dot(mid, m01) + _dot(lo, m01)


HEAD_SLOT = 2 * HEAD_DIM
N_PIECES = 3
PIECE_ROWS = 128
SUM_ROWS = 16


def _in_proj_kernel(*refs, tm, q_scale, attn_layouts):
    x_ref, g_ref, b_ref, wu_ref = refs[:4]
    if attn_layouts:
        (wqT_ref, wkT_ref, wvT_ref, wfT_ref, bfc_ref, tri_ref, place_ref,
         u_ref, kT_ref, vT_ref, lfT_ref, cT_ref, qTb_ref, k2_ref, vTb_ref, carry_ref) = refs[4:]
    else:
        wq_ref, wk_ref, wv_ref, wf_ref, bf_ref, u_ref, q_ref, k_ref, v_ref, lf_ref = refs[4:]
    hb = _layer_norm(x_ref[0], g_ref[...], b_ref[...]).astype(BF16)
    _store_lane_tiles(u_ref, _dot(hb, wu_ref[...]))
    if attn_layouts:
        qTb_ref[0] = (_dot_nt(wqT_ref[...], hb) * q_scale).astype(BF16)
        kT = _dot_nt(wkT_ref[...], hb)
        kT_ref[0] = kT
        vT = _dot_nt(wvT_ref[...], hb)
        vT_ref[0] = vT
        vTb_ref[0] = vT.astype(BF16)
        lfT = _log_sigmoid(_dot_nt(wfT_ref[...], hb) + bfc_ref[...])
        lfT_ref[0] = lfT

        @pl.when(pl.program_id(1) == 0)
        def _():
            carry_ref[...] = jnp.zeros_like(carry_ref)

        c = carry_ref[...] + _dot_exact01(lfT, tri_ref[...])
        cT_ref[0] = c
        carry_ref[...] = c[:, tm - 1:tm]
        hi, mid, lo = _split3(c * LOG2E)
        row = lax.broadcasted_iota(jnp.int32, (PIECE_ROWS - N_PIECES * N_HEADS, tm), 0)
        ones = jnp.where(row < N_PIECES, 1.0, 0.0)
        pieces = jnp.concatenate([hi.astype(F32), mid.astype(F32), lo.astype(F32), ones], axis=0)
        bias = _dot(pieces.T.astype(BF16), place_ref[...])
        k_tok = kT.T
        gap = jnp.zeros((tm, HEAD_SLOT - HEAD_DIM), F32)
        k_slots = jnp.concatenate([x for h in range(N_HEADS) for x in (k_tok[:, h * HEAD_DIM:(h + 1) * HEAD_DIM], gap)],
                                  axis=1)
        k2_ref[0] = (k_slots + bias).astype(BF16)
    else:
        q_ref[0] = _dot(hb, wq_ref[...]) * q_scale
        k_ref[0] = _dot(hb, wk_ref[...])
        v_ref[0] = _dot(hb, wv_ref[...])
        lf_ref[0] = _log_sigmoid(_dot(hb, wf_ref[...]) + bf_ref[...])


def _in_proj(x3, ln_g, ln_b, w_in, b_f, *, tm, q_scale, attn_layouts):
    nb, t, _ = x3.shape
    wq = w_in[:, :ATTN_WIDTH].astype(BF16)
    wk = w_in[:, ATTN_WIDTH:2 * ATTN_WIDTH].astype(BF16)
    wv = w_in[:, 2 * ATTN_WIDTH:3 * ATTN_WIDTH].astype(BF16)
    wf = w_in[:, 3 * ATTN_WIDTH:3 * ATTN_WIDTH + N_HEADS].astype(BF16)
    wu = w_in[:, 3 * ATTN_WIDTH + N_HEADS:].astype(BF16)
    tok = lambda w: pl.BlockSpec((1, tm, w), lambda b, i: (b, i, 0))
    feat = lambda w: pl.BlockSpec((1, w, tm), lambda b, i: (b, 0, i))
    sds = jax.ShapeDtypeStruct
    ins = [x3, ln_g.reshape(1, -1), ln_b.reshape(1, -1), wu]
    in_specs = [tok(D_MODEL), _const_spec((1, D_MODEL)), _const_spec((1, D_MODEL)), _const_spec((D_MODEL, SSM_WIDTH))]
    out_shape = [sds((nb, SSM_WIDTH // LANES, t, LANES), F32)]
    out_specs = [_lane_tiles_spec(tm, SSM_WIDTH)]
    scratch = []
    if attn_layouts:
        tri = (jnp.arange(tm)[:, None] <= jnp.arange(tm)[None, :]).astype(BF16)
        row = jnp.arange(PIECE_ROWS)[:, None]
        lane = jnp.arange(N_HEADS * HEAD_SLOT)[None, :]
        head, bias_lane = lane // HEAD_SLOT, lane % HEAD_SLOT - HEAD_DIM
        piece_row = (row < N_PIECES * N_HEADS) & (row % N_HEADS == head) & (row // N_HEADS == bias_lane)
        ones_row = (row >= N_PIECES * N_HEADS) & (row - N_PIECES * N_HEADS + N_PIECES == bias_lane) & (bias_lane < 2 * N_PIECES)
        place = (piece_row | ones_row).astype(BF16)
        ins += [wq.T, wk.T, wv.T, wf.T, b_f.reshape(N_HEADS, 1), tri, place]
        in_specs += [_const_spec((ATTN_WIDTH, D_MODEL))] * 3
        in_specs += [_const_spec((N_HEADS, D_MODEL)), _const_spec((N_HEADS, 1)), _const_spec((tm, tm)),
                     _const_spec((PIECE_ROWS, N_HEADS * HEAD_SLOT))]
        out_shape += [sds((nb, ATTN_WIDTH, t), F32), sds((nb, ATTN_WIDTH, t), F32),
                      sds((nb, N_HEADS, t), F32), sds((nb, N_HEADS, t), F32),
                      sds((nb, ATTN_WIDTH, t), BF16), sds((nb, t, N_HEADS * HEAD_SLOT), BF16), sds((nb, ATTN_WIDTH, t), BF16)]
        out_specs += [feat(ATTN_WIDTH), feat(ATTN_WIDTH), feat(N_HEADS), feat(N_HEADS),
                      feat(ATTN_WIDTH), tok(N_HEADS * HEAD_SLOT), feat(ATTN_WIDTH)]
        scratch = [pltpu.VMEM((N_HEADS, 1), F32)]
    else:
        ins += [wq, wk, wv, wf, b_f.reshape(1, N_HEADS)]
        in_specs += [_const_spec((D_MODEL, ATTN_WIDTH))] * 3 + [_const_spec((D_MODEL, N_HEADS)), _const_spec((1, N_HEADS))]
        out_shape += [sds((nb, t, ATTN_WIDTH), F32)] * 3 + [sds((nb, t, N_HEADS), F32)]
        out_specs += [tok(ATTN_WIDTH)] * 3 + [tok(N_HEADS)]
    return pl.pallas_call(
        functools.partial(_in_proj_kernel, tm=tm, q_scale=q_scale, attn_layouts=attn_layouts),
        grid=(nb, t // tm), in_specs=in_specs, out_specs=out_specs, out_shape=out_shape,
        scratch_shapes=scratch, compiler_params=_params("arbitrary", "arbitrary"),
        name="in_proj_prompt" if attn_layouts else "in_proj_sample")(*ins)


def _fox_prompt_kernel(qT_ref, k2_ref, vT_ref, cT_ref, o_ref, bq_ref, m_ref, acc_ref, s0_ref, s1_ref, *, tq):
    qi = pl.program_id(1)
    q0 = pl.multiple_of(qi * tq, tq)
    heads = [slice(h * HEAD_DIM, (h + 1) * HEAD_DIM) for h in range(N_HEADS)]
    slots = [slice(h * HEAD_SLOT, (h + 1) * HEAD_SLOT) for h in range(N_HEADS)]

    c0 = cT_ref[0, :, pl.ds(q0, tq)][:, 0:1] * LOG2E
    c0_pieces = [p.astype(F32) for p in _split3(c0)]
    r = lax.broadcasted_iota(jnp.int32, (HEAD_SLOT - HEAD_DIM, tq), 0)
    for h in range(N_HEADS):
        bq_ref[h, 0:HEAD_DIM, :] = qT_ref[0, heads[h], :]
        sel = jnp.where(r < N_PIECES, -1.0, 0.0)
        for j, piece in enumerate(c0_pieces):
            sel = jnp.where(r == N_PIECES + j, piece[h:h + 1, :], sel)
        bq_ref[h, HEAD_DIM:, :] = sel.astype(BF16)

    key_row = lax.broadcasted_iota(jnp.int32, (tq, tq), 0)
    qry_col = lax.broadcasted_iota(jnp.int32, (tq, tq), 1)

    m_ref[...] = jnp.full_like(m_ref, NEG_INF)
    acc_ref[...] = jnp.zeros_like(acc_ref)
    ones = jnp.ones((SUM_ROWS, tq), BF16)

    s_refs = (s0_ref, s1_ref)

    def scores(kt, h, buf):
        ks = pl.multiple_of(kt * tq, tq)
        s_refs[buf][h] = _dot(k2_ref[0, pl.ds(ks, tq), slots[h]], bq_ref[h])

    def accumulate(kt, h, buf, diagonal):
        ks = pl.multiple_of(kt * tq, tq)
        s = s_refs[buf][h]
        if diagonal:
            s = jnp.where(key_row <= qry_col, s, NEG_INF)
        m_new = jnp.maximum(m_ref[h], jnp.max(s, axis=0, keepdims=True))
        alpha = jnp.exp2(m_ref[h] - m_new)
        m_ref[h] = m_new
        p = jnp.exp2((s - m_new).astype(BF16))
        v1 = jnp.concatenate([vT_ref[0, heads[h], pl.ds(ks, tq)], ones], axis=0)
        acc_ref[h] = alpha * acc_ref[h] + _dot(v1, p)

    def stage(kt, cur, *, diagonal=False, issue_next=True):
        for h in range(N_HEADS + 1):
            if issue_next and h < N_HEADS:
                scores(kt + 1, h, 1 - cur)
            if h >= 1:
                accumulate(kt, h - 1, cur, diagonal)

    for h in range(N_HEADS):
        scores(0, h, 0)

    def body(j, carry):
        stage(2 * j, 0)
        stage(2 * j + 1, 1)
        return carry

    lax.fori_loop(0, qi // 2, body, 0)

    @pl.when(qi % 2 == 0)
    def _():
        stage(qi, 0, diagonal=True, issue_next=False)

    @pl.when(qi % 2 == 1)
    def _():
        stage(qi - 1, 0)
        stage(qi, 1, diagonal=True, issue_next=False)

    outs = [acc_ref[h, 0:HEAD_DIM, :] / acc_ref[h, HEAD_DIM:HEAD_DIM + 1, :] for h in range(N_HEADS)]
    o_ref[0] = jnp.concatenate(outs, axis=0).T.astype(o_ref.dtype)


def _fox_prompt(qT, k2, vT, cT, *, tq):
    nb, t, _ = k2.shape
    whole_f = lambda w: pl.BlockSpec((1, w, t), lambda b, i: (b, 0, 0))
    whole_t = lambda w: pl.BlockSpec((1, t, w), lambda b, i: (b, 0, 0))
    return pl.pallas_call(
        functools.partial(_fox_prompt_kernel, tq=tq),
        grid=(nb, t // tq),
        in_specs=[pl.BlockSpec((1, ATTN_WIDTH, tq), lambda b, i: (b, 0, i)),
                  whole_t(N_HEADS * HEAD_SLOT), whole_f(ATTN_WIDTH), whole_f(N_HEADS)],
        out_specs=pl.BlockSpec((1, tq, ATTN_WIDTH), lambda b, i: (b, i, 0)),
        out_shape=jax.ShapeDtypeStruct((nb, t, ATTN_WIDTH), BF16),
        scratch_shapes=[pltpu.VMEM((N_HEADS, HEAD_SLOT, tq), BF16),
                        pltpu.VMEM((N_HEADS, 1, tq), F32), pltpu.VMEM((N_HEADS, HEAD_DIM + SUM_ROWS, tq), F32),
                        pltpu.VMEM((N_HEADS, tq, tq), F32), pltpu.VMEM((N_HEADS, tq, tq), F32)],
        compiler_params=_params("arbitrary", "arbitrary"), name="fox_prompt")(qT, k2, vT, cT)


SAMPLE_SEQS_PER_STEP = 2

def _fox_sample_kernel(pt_ref, q_ref, kn_ref, vn_ref, lfn_ref, sl_ref, *refs, n_pages, nq, n_seq):
    del pt_ref
    o_ref = refs[3 * n_seq * n_pages]
    for e in range(n_seq):
        pages = [refs[(kind * n_seq + e) * n_pages:(kind * n_seq + e + 1) * n_pages] for kind in range(3)]
        _fox_sample_one(e, q_ref, kn_ref, vn_ref, lfn_ref, sl_ref, *pages, o_ref, n_pages=n_pages, nq=nq)


def _fox_sample_one(e, q_ref, kn_ref, vn_ref, lfn_ref, sl_ref, k_refs, v_refs, pf_refs, o_ref, *, n_pages, nq):
    nr = nq * N_HEADS
    head_of_lane = lax.broadcasted_iota(jnp.int32, (N_HEADS, ATTN_WIDTH), 1) // HEAD_DIM
    own = head_of_lane == lax.broadcasted_iota(jnp.int32, (N_HEADS, ATTN_WIDTH), 0)
    q = q_ref[e]
    qbd32 = jnp.concatenate([jnp.where(own, q[i:i + 1, :], 0.0) for i in range(nq)], axis=0)
    qbd = qbd32.astype(BF16)
    rep = lambda e: jnp.concatenate([e] * nq, axis=0)
    rows = lambda j: slice(j * N_HEADS, (j + 1) * N_HEADS)

    lfn = lfn_ref[e]
    e_new = rep(_dot_exact01(lfn, sl_ref[...]))
    kn, vn = kn_ref[e], vn_ref[e]
    qidx = lax.broadcasted_iota(jnp.int32, (nr, 1), 0) // N_HEADS
    s_new = [jnp.where(qidx >= j, jnp.sum(qbd32 * kn[j:j + 1, :], axis=-1, keepdims=True) + e_new[:, j:j + 1], NEG_INF)
             for j in range(nq)]

    pf_all = jnp.concatenate([r[0, 0] for r in pf_refs], axis=0)
    sfx_all = _dot_exact01(pf_all, sl_ref[...])
    tot_all = sfx_all[:, 0:1] + pf_all[:, 0:1]
    tail = jnp.sum(lfn, axis=-1, keepdims=True)
    scores = [None] * n_pages
    for j in reversed(range(n_pages)):
        kt = k_refs[j][0, 0].reshape(ATTN_WIDTH, PAGE_SIZE).astype(BF16)
        scores[j] = _dot(qbd, kt) + rep(sfx_all[rows(j)] + tail)
        tail = tail + tot_all[rows(j)]
    s_all = jnp.concatenate(scores, axis=1)
    m = jnp.max(s_all, axis=-1, keepdims=True)
    for s in s_new:
        m = jnp.maximum(m, s)
    p_all = jnp.exp(s_all - m)
    l = jnp.sum(p_all, axis=-1, keepdims=True)
    acc = jnp.zeros((nr, ATTN_WIDTH), F32)
    for j, s in enumerate(s_new):
        p = jnp.exp(s - m)
        l = l + p
        acc = acc + p * vn[j:j + 1, :]
    p_all = p_all.astype(BF16)
    for j in range(n_pages):
        vt = v_refs[j][0, 0].reshape(ATTN_WIDTH, PAGE_SIZE).astype(BF16)
        acc = acc + _dot_nt(p_all[:, j * PAGE_SIZE:(j + 1) * PAGE_SIZE], vt)
    acc = acc / l
    for i in range(nq):
        o_ref[e, i:i + 1, :] = jnp.sum(jnp.where(own, acc[i * N_HEADS:(i + 1) * N_HEADS], 0.0), axis=0, keepdims=True)


def _fox_sample(q, k_new, v_new, logf_new, cache_k, cache_v, cache_logf, page_table):
    db, nq, _ = q.shape
    n_pages = page_table.shape[1]
    kc = jnp.transpose(cache_k, (0, 2, 3, 1))
    vc = jnp.transpose(cache_v, (0, 2, 3, 1))
    pf = jnp.transpose(cache_logf, (0, 2, 1))
    lfn_t = jnp.swapaxes(jnp.pad(logf_new, ((0, 0), (0, PAGE_SIZE - nq), (0, 0))), 1, 2)
    strict_lower = (jnp.arange(PAGE_SIZE)[:, None] > jnp.arange(PAGE_SIZE)[None, :]).astype(BF16)
    ns = SAMPLE_SEQS_PER_STEP
    seq = lambda r, w: pl.BlockSpec((ns, r, w), lambda b, pt: (b, 0, 0))
    page = lambda e, j: pl.BlockSpec((1, 1, N_HEADS, HEAD_DIM, PAGE_SIZE), lambda b, pt: (0, pt[ns * b + e, j], 0, 0, 0))
    page_f = lambda e, j: pl.BlockSpec((1, 1, N_HEADS, PAGE_SIZE), lambda b, pt: (0, pt[ns * b + e, j], 0, 0))
    every = [(e, j) for e in range(ns) for j in range(n_pages)]
    in_specs = [seq(nq, ATTN_WIDTH), seq(nq, ATTN_WIDTH), seq(nq, ATTN_WIDTH), seq(N_HEADS, PAGE_SIZE),
                pl.BlockSpec((PAGE_SIZE, PAGE_SIZE), lambda b, pt: (0, 0))]
    in_specs += [page(e, j) for e, j in every] * 2 + [page_f(e, j) for e, j in every]
    return pl.pallas_call(
        functools.partial(_fox_sample_kernel, n_pages=n_pages, nq=nq, n_seq=ns),
        grid_spec=pltpu.PrefetchScalarGridSpec(
            num_scalar_prefetch=1, grid=(db // ns,), in_specs=in_specs, out_specs=seq(nq, ATTN_WIDTH)),
        out_shape=jax.ShapeDtypeStruct((db, nq, ATTN_WIDTH), F32),
        compiler_params=_params("arbitrary"), name="fox_sample")(
            page_table, q, k_new, v_new, lfn_t, strict_lower,
            *([kc[None]] * len(every)), *([vc[None]] * len(every)), *([pf[None]] * len(every)))


PREP_GROUPS = 2

def _cmul(ar, ai, br, bi):
    return ar * br - ai * bi, ar * bi + ai * br


def _cpow_by_bits(ar, ai, n, nbits):
    pr = jnp.ones(jnp.broadcast_shapes(ar.shape, n.shape), F32)
    pi = jnp.zeros_like(pr)
    for b in range(nbits):
        bit = ((n >> b) & 1) == 1
        fr = jnp.where(bit, ar, 1.0)
        fi = jnp.where(bit, ai, 0.0)
        pr, pi = _cmul(pr, pi, fr, fi)
        ar, ai = _cmul(ar, ai, ar, ai)
    return pr, pi


def _ssm_prep_kernel(*refs, chunk, small, n_groups):
    for e in range(n_groups):
        _ssm_prep_group(*[r.at[pl.ds(e, 1)] for r in refs], chunk=chunk, small=small)


def _ssm_prep_group(ldt_ref, lr_ref, li_ref, bre_ref, bim_ref, cre_ref, cim_ref,
                    toep_ref, ctl_ref, obs_ref, al_ref,
                    toeps_ref, ctlrs_ref, ctlis_ref, obsrs_ref, obsis_ref, asr_ref, asi_ref, *, chunk, small):
    lw, sw = chunk * SSM_GROUP, small * SSM_GROUP
    p = STATE_DIM
    nbits = chunk.bit_length()
    dt = jnp.exp(ldt_ref[0])
    lr_row, li_row = lr_ref[0], li_ref[0]

    def discretise(lr, li):
        mag = jnp.exp(lr * dt)
        return mag * jnp.cos(li * dt), mag * jnp.sin(li * dt)

    def column(row):
        eye = lax.broadcasted_iota(jnp.int32, (p, p), 0) == lax.broadcasted_iota(jnp.int32, (p, p), 1)
        return jnp.sum(jnp.where(eye, row, 0.0), axis=-1, keepdims=True)

    lr, li = column(lr_row), column(li_row)
    ar, ai = discretise(lr, li)
    den = lr * lr + li * li
    nr, ni = ar - 1.0, ai
    cr, ci = (nr * lr + ni * li) / den, (ni * lr - nr * li) / den
    btr = jnp.concatenate([bre_ref[0]] * (2 * lw // LANES), axis=1)
    bti = jnp.concatenate([bim_ref[0]] * (2 * lw // LANES), axis=1)
    bbr, bbi = _cmul(cr, ci, btr, bti)
    sig = lax.broadcasted_iota(jnp.int32, (1, 2 * lw), 1) // SSM_GROUP
    er, ei = _cpow_by_bits(ar, ai, jnp.maximum(chunk - 1 - sig, 0), nbits)
    hr, hi = _cmul(er, ei, bbr, bbi)
    hr = jnp.where(sig < chunk, hr, 0.0)
    hi = jnp.where(sig < chunk, hi, 0.0)
    ctl_ref[0] = jnp.concatenate([hr[:, :lw], hi[:, :lw]], axis=0).astype(BF16)
    ctlrs_ref[0] = hr[:, lw - sw:lw]
    ctlis_ref[0] = hi[:, lw - sw:lw]
    c_re, c_im = cre_ref[0], cim_ref[0]
    hp = lax.Precision.HIGHEST
    k_all = (jnp.dot(c_re, hr, precision=hp, preferred_element_type=F32)
             - jnp.dot(c_im, hi, precision=hp, preferred_element_type=F32))
    for t in range(chunk):
        off = (chunk - 1 - t) * SSM_GROUP
        toep_ref[0, t * SSM_GROUP:(t + 1) * SSM_GROUP, :] = k_all[:, off:off + lw].astype(BF16)
        if t < small:
            toeps_ref[0, t * SSM_GROUP:(t + 1) * SSM_GROUP, :] = k_all[:, off:off + sw]

    arr, air = discretise(lr_row, li_row)
    tp1 = lax.broadcasted_iota(jnp.int32, (lw, 1), 0) // SSM_GROUP + 1
    pr, pi = _cpow_by_bits(arr, air, tp1, nbits)
    c_r = jnp.concatenate([c_re] * chunk, axis=0)
    c_i = jnp.concatenate([c_im] * chunk, axis=0)
    obs_re = c_r * pr - c_i * pi
    obs_im = -(c_r * pi + c_i * pr)
    obs_ref[0] = jnp.concatenate([obs_re, obs_im], axis=1).astype(BF16)
    obsrs_ref[0] = obs_re[:sw]
    obsis_ref[0] = obs_im[:sw]
    n = 1
    while n < chunk:
        arr, air = _cmul(arr, air, arr, air)
        n *= 2
        if n == small:
            asr_ref[0] = arr
            asi_ref[0] = air
    al_ref[0] = jnp.concatenate([jnp.concatenate([arr, arr], axis=1), jnp.concatenate([-air, air], axis=1)], axis=0)


def _ssm_prep(lam_re, lam_im, log_dt, b_re, b_im, c_re, c_im, *, chunk, small):
    assert chunk & (chunk - 1) == 0 and small & (small - 1) == 0 and 1 < small < chunk
    g, p, j = N_GROUPS, STATE_DIM, SSM_GROUP
    lw, sw = chunk * j, small * j
    lane_tile = lambda b: jnp.tile(b, (1, 1, LANES // j))
    ins = [log_dt.reshape(g, 1, 1), lam_re.reshape(g, 1, p), lam_im.reshape(g, 1, p), lane_tile(b_re), lane_tile(b_im),
           c_re, c_im]
    grp = lambda a, b: pl.BlockSpec((PREP_GROUPS, a, b), lambda i: (i, 0, 0))
    in_specs = [grp(1, 1), grp(1, p), grp(1, p), grp(p, LANES), grp(p, LANES), grp(j, p), grp(j, p)]
    shapes = [((lw, lw), BF16), ((2 * p, lw), BF16), ((lw, 2 * p), BF16), ((2, 2 * p), F32),
              ((sw, sw), F32), ((p, sw), F32), ((p, sw), F32), ((sw, p), F32), ((sw, p), F32), ((1, p), F32), ((1, p), F32)]
    outs = pl.pallas_call(
        functools.partial(_ssm_prep_kernel, chunk=chunk, small=small, n_groups=PREP_GROUPS),
        grid=(g // PREP_GROUPS,), in_specs=in_specs,
        out_specs=[grp(*s) for s, _ in shapes],
        out_shape=[jax.ShapeDtypeStruct((g,) + s, d) for s, d in shapes],
        compiler_params=_params("arbitrary"), name="ssm_prep")(*ins)
    return outs[:4], outs[4:]


def _ssm_step_kernel(u_ref, toep_ref, ctlr_ref, ctli_ref, obsr_ref, obsi_ref, ar_ref, ai_ref, h0r_ref, h0i_ref,
                     y_ref, hr_ref, hi_ref, *, chunk):
    n = GROUPS_PER_TILE
    n_seq = h0r_ref.shape[1]
    lw = chunk * SSM_GROUP
    zero = jnp.zeros((n_seq, LANES), F32)
    xs = [u_ref[0, 0, pl.ds(t, n_seq, stride=chunk), :] if t < chunk else zero for t in range(n)]
    ys = []
    for e, x in enumerate(_block_transpose(xs)):
        u = x[:, :lw].astype(BF16)
        h0r, h0i = h0r_ref[e], h0i_ref[e]
        er, ei = _cmul(ar_ref[e], ai_ref[e], h0r, h0i)
        hr_ref[e] = _dot_nt(u, ctlr_ref[e].astype(BF16)) + er
        hi_ref[e] = _dot_nt(u, ctli_ref[e].astype(BF16)) + ei
        y = (_dot_nt(u, toep_ref[e].astype(BF16)) + _dot_nt(h0r.astype(BF16), obsr_ref[e].astype(BF16))
             + _dot_nt(h0i.astype(BF16), obsi_ref[e].astype(BF16)))
        ys.append(jnp.concatenate([y, jnp.zeros((n_seq, LANES - lw), F32)], axis=1))
    for t, y in enumerate(_block_transpose(ys)[:chunk]):
        y_ref[0, 0, pl.ds(t, n_seq, stride=chunk), :] = y


def _ssm_step(u, ops, h0, *, chunk):
    n_seq = h0[0].shape[0]
    g, p, n = N_GROUPS, STATE_DIM, GROUPS_PER_TILE
    lw = chunk * SSM_GROUP
    assert lw <= LANES and u.shape == (1, g // n, n_seq * chunk, LANES)
    grp = lambda a, b: pl.BlockSpec((n, a, b), lambda v: (v, 0, 0))
    tile = pl.BlockSpec((1, 1, n_seq * chunk, LANES), lambda v: (0, v, 0, 0))
    y, hr, hi = pl.pallas_call(
        functools.partial(_ssm_step_kernel, chunk=chunk), grid=(g // n,),
        in_specs=[tile, grp(lw, lw), grp(p, lw), grp(p, lw), grp(lw, p), grp(lw, p), grp(1, p), grp(1, p),
                  grp(n_seq, p), grp(n_seq, p)],
        out_specs=[tile, grp(n_seq, p), grp(n_seq, p)],
        out_shape=[jax.ShapeDtypeStruct(u.shape, F32), jax.ShapeDtypeStruct((g, n_seq, p), F32),
                   jax.ShapeDtypeStruct((g, n_seq, p), F32)],
        compiler_params=_params("arbitrary"), name="ssm_step")(
            u, *ops, jnp.swapaxes(h0[0], 0, 1), jnp.swapaxes(h0[1], 0, 1))
    return y, jnp.swapaxes(hr, 0, 1), jnp.swapaxes(hi, 0, 1)


GROUPS_PER_TILE = LANES // SSM_GROUP
GROUPS_PER_TRIP = 8


def _block_transpose(xs):
    n = len(xs)
    blk = lax.broadcasted_iota(jnp.int32, xs[0].shape, 1) // SSM_GROUP
    xs = list(xs)
    d = n // 2
    while d:
        upper = (blk & d) != 0
        for i in range(n):
            if not i & d:
                lo, hi = xs[i], xs[i + d]
                xs[i] = jnp.where(upper, pltpu.roll(hi, SSM_GROUP * d, axis=1), lo)
                xs[i + d] = jnp.where(upper, hi, pltpu.roll(lo, LANES - SSM_GROUP * d, axis=1))
        d //= 2
    return xs


def _ssm_seq_kernel(u_ref, toep_ref, ctl_ref, obs_ref, al_ref, y_ref, h_ref, ug_ref, yg_ref, *, chunk):
    t = u_ref.shape[2]
    rows = t // chunk
    lw = chunk * SSM_GROUP
    n = GROUPS_PER_TILE
    p = STATE_DIM
    ridx = lax.broadcasted_iota(jnp.int32, (rows, 1), 0)

    def shifted(x, d):
        return jnp.where(ridx >= d, pltpu.roll(x, d, axis=0), 0.0)

    def group_tile(v, carry):
        for w in range(lw // LANES):
            xs = [u_ref[0, v, pl.ds(n * w + k, rows, stride=chunk), :] for k in range(n)]
            for k, x in enumerate(_block_transpose(xs)):
                ug_ref[k, :, w * LANES:(w + 1) * LANES] = x.astype(BF16)

        def groups(i, c):
            ks = [i * GROUPS_PER_TRIP + e for e in range(GROUPS_PER_TRIP)]
            gs = [v * n + k for k in ks]
            ugs = [ug_ref[k] for k in ks]
            ys = [_dot_nt(ug, toep_ref[g]) for ug, g in zip(ugs, gs)]
            hs = [_dot_nt(ug, ctl_ref[g]) for ug, g in zip(ugs, gs)]
            ars = [al_ref[g][0:1, :] for g in gs]
            ais = [al_ref[g][1:2, :] for g in gs]
            d = 1
            while d < rows:
                for e in range(GROUPS_PER_TRIP):
                    sh = shifted(hs[e], d)
                    hs[e] = hs[e] + ars[e] * sh + ais[e] * pltpu.roll(sh, p, axis=1)
                    ars[e], ais[e] = ars[e] * ars[e] - ais[e] * ais[e], 2.0 * ars[e] * ais[e]
                d *= 2
            for e, (k, g) in enumerate(zip(ks, gs)):
                h_ref[0, pl.ds(g, 1), :] = hs[e][rows - 1:rows, :]
                yg_ref[k] = ys[e] + _dot_nt(shifted(hs[e], 1).astype(BF16), obs_ref[g])
            return c

        lax.fori_loop(0, n // GROUPS_PER_TRIP, groups, 0)
        for w in range(lw // LANES):
            ys = [yg_ref[k, :, w * LANES:(w + 1) * LANES] for k in range(n)]
            for k, y in enumerate(_block_transpose(ys)):
                y_ref[0, v, pl.ds(n * w + k, rows, stride=chunk), :] = y
        return carry

    lax.fori_loop(0, N_GROUPS // n, group_tile, 0)


def _ssm_seq(u, ops, *, chunk):
    n_seq, _, t, _ = u.shape
    g, p = N_GROUPS, STATE_DIM
    lw = chunk * SSM_GROUP
    assert lw % LANES == 0 and chunk == GROUPS_PER_TILE * (lw // LANES) and SSM_WIDTH == g * SSM_GROUP
    whole = lambda a: pl.BlockSpec(a.shape, lambda b: (0, 0, 0), pipeline_mode=pl.Buffered(1))
    seq = pl.BlockSpec((1, SSM_WIDTH // LANES, t, LANES), lambda b: (b, 0, 0, 0))
    y, h = pl.pallas_call(
        functools.partial(_ssm_seq_kernel, chunk=chunk), grid=(n_seq,),
        in_specs=[seq] + [whole(o) for o in ops],
        out_specs=[seq, pl.BlockSpec((1, g, 2 * p), lambda b: (b, 0, 0))],
        out_shape=[jax.ShapeDtypeStruct(u.shape, F32), jax.ShapeDtypeStruct((n_seq, g, 2 * p), F32)],
        scratch_shapes=[pltpu.VMEM((GROUPS_PER_TILE, t // chunk, lw), BF16),
                        pltpu.VMEM((GROUPS_PER_TILE, t // chunk, lw), F32)],
        compiler_params=_params("arbitrary"), name="ssm_seq")(u, *ops)
    return y, h[:, :, :p], h[:, :, p:]


FF_CHUNK = 1024


def _post_kernel(x_ref, att_ref, ys_ref, u_ref, p_ref, lng_ref, lnb_ref, dsk_ref, wglu_ref, bglu_ref,
                 woa_ref, wos_ref, l1g_ref, l1b_ref, wup_ref, wdn_ref, wpe_ref, wpg_ref, bpg_ref,
                 l2g_ref, l2b_ref, o_ref):
    h = _layer_norm(x_ref[0], lng_ref[...], lnb_ref[...])
    y = _gelu_tanh(_load_lane_tiles(ys_ref) + dsk_ref[...] * _load_lane_tiles(u_ref))
    y = y * _sigmoid(_dot(y.astype(BF16), wglu_ref[...]) + bglu_ref[...])
    mix = _dot(att_ref[0], woa_ref[...]) + _dot(y.astype(BF16), wos_ref[...])
    h1 = _layer_norm(ALPHA * h + mix, l1g_ref[...], l1b_ref[...])
    h1b = h1.astype(BF16)
    e = _sigmoid(_dot(h1b, wpg_ref[...]) + bpg_ref[...]) * _dot(p_ref[0].astype(BF16), wpe_ref[...])
    acc = ALPHA * h1 + e
    for c in range(D_FF // FF_CHUNK):
        cs = slice(c * FF_CHUNK, (c + 1) * FF_CHUNK)
        a = jnp.maximum(_dot(h1b, wup_ref[:, cs]), 0.0)
        acc = acc + _dot((a * a).astype(BF16), wdn_ref[cs, :])
    o_ref[0] = _layer_norm(acc, l2g_ref[...], l2b_ref[...])


def _post_mixer(x, att, ys, u, p, weights, *, tm):
    nb, t, _ = x.shape
    tok = lambda w: pl.BlockSpec((1, tm, w), lambda b, i: (b, i, 0))
    const = lambda a: pl.BlockSpec(a.shape, lambda b, i: (0, 0), pipeline_mode=pl.Buffered(1))
    return pl.pallas_call(
        _post_kernel, grid=(nb, t // tm),
        in_specs=[tok(D_MODEL), tok(ATTN_WIDTH), _lane_tiles_spec(tm, SSM_WIDTH), _lane_tiles_spec(tm, SSM_WIDTH),
                  tok(PLE_DIM)] + [const(w) for w in weights],
        out_specs=tok(D_MODEL), out_shape=jax.ShapeDtypeStruct((nb, t, D_MODEL), F32),
        compiler_params=_params("arbitrary", "arbitrary"), name="post_mixer")(x, att, ys, u, p, *weights)


def kernel(x_prompt, x_sample, cache_k, cache_v, cache_logf, state_re, state_im, page_table, p_prompt, p_sample,
           ln_in_g, ln_in_b, w_in, b_f, lam_re, lam_im, log_dt, b_re, b_im, c_re, c_im, d_skip, w_glu, b_glu,
           w_out, ln1_g, ln1_b, w_up, w_down, w_pe, w_pg, b_pg, ln2_g, ln2_b):
    assert w_in.shape[0] == 1, "one trunk layer"
    nb, t, _ = x_prompt.shape
    db, nq, _ = x_sample.shape
    row = lambda a: a.reshape(1, -1)
    post_w = [row(ln_in_g), row(ln_in_b), row(d_skip[0]), w_glu[0].astype(BF16), row(b_glu[0]),
              w_out[0, :ATTN_WIDTH].astype(BF16), w_out[0, ATTN_WIDTH:].astype(BF16), row(ln1_g[0]), row(ln1_b[0]),
              w_up[0].astype(BF16), w_down[0].astype(BF16), w_pe[0].astype(BF16), w_pg[0].astype(BF16),
              row(b_pg[0]), row(ln2_g[0]), row(ln2_b[0])]
    ssm_par = (lam_re[0], lam_im[0], log_dt[0], b_re[0], b_im[0], c_re[0], c_im[0])

    u, kT, vT, lfT, cT, qTb, k2, vTb = _in_proj(x_prompt, ln_in_g, ln_in_b, w_in[0], b_f[0],
                                                     tm=IN_PROJ_ROWS, q_scale=QK_SCALE * LOG2E, attn_layouts=True)
    att = _fox_prompt(qTb, k2, vTb, cT, tq=ATTN_TILE)
    ssm_ops, ssm_ops_sample = _ssm_prep(*ssm_par, chunk=SSM_CHUNK, small=nq)
    ys, sr, si = _ssm_seq(u, ssm_ops, chunk=SSM_CHUNK)
    y_prompt = _post_mixer(x_prompt, att, ys, u, p_prompt[0], post_w, tm=POST_ROWS)
    heads_last = lambda a: jnp.transpose(a.reshape(nb, N_HEADS, HEAD_DIM, t), (0, 3, 1, 2))[None]
    prompt_out = (heads_last(kT), heads_last(vT), jnp.swapaxes(lfT, 1, 2)[None], sr[None], si[None])

    us, qs, ks, vs, logfs = _in_proj(x_sample.reshape(1, db * nq, D_MODEL), ln_in_g, ln_in_b, w_in[0], b_f[0],
                                     tm=db * nq, q_scale=QK_SCALE, attn_layouts=False)
    seq = lambda a: a.reshape(db, nq, a.shape[-1])
    att_s = _fox_sample(seq(qs), seq(ks), seq(vs), seq(logfs), cache_k[0], cache_v[0], cache_logf[0], page_table)
    ys_s, sr_s, si_s = _ssm_step(us, ssm_ops_sample, (state_re[0], state_im[0]), chunk=nq)
    flat = lambda a: a.reshape(1, db * nq, a.shape[-1])
    y_sample = _post_mixer(flat(x_sample), flat(att_s).astype(BF16), ys_s, us, flat(p_sample[0]),
                           post_w, tm=db * nq).reshape(db, nq, D_MODEL)
    sample_out = (ks.reshape(1, db, nq, N_HEADS, HEAD_DIM), vs.reshape(1, db, nq, N_HEADS, HEAD_DIM),
                  logfs.reshape(1, db, nq, N_HEADS), sr_s[None], si_s[None])
    return (y_prompt, y_sample) + prompt_out + sample_out
```

```python
import functools
import math

import jax
import jax.numpy as jnp
from jax import lax
from jax.experimental import pallas as pl
from jax.experimental.pallas import tpu as pltpu

F32 = jnp.float32
BF16 = jnp.bfloat16

D_MODEL = 1024
ATTN_WIDTH = 512
SSM_WIDTH = 512
HEAD_DIM = 64
N_HEADS = 8
SSM_GROUP = 16
N_GROUPS = 32
STATE_DIM = 64
D_FF = 4096
PLE_DIM = 256
PAGE_SIZE = 128
ALPHA = 2.0 ** 0.25
LN_EPS = 1e-5
NEG_INF = -1e30
QK_SCALE = HEAD_DIM ** -0.5
LOG2E = math.log2(math.e)

VMEM_LIMIT_BYTES = 56 * 1024 * 1024

IN_PROJ_ROWS = 1024
ATTN_TILE = 256
SSM_CHUNK = 16
POST_ROWS = 512

_NT = (((1,), (1,)), ((), ()))


def _params(*sem):
    return pltpu.CompilerParams(dimension_semantics=sem, vmem_limit_bytes=VMEM_LIMIT_BYTES)


def _const_spec(shape):
    return pl.BlockSpec(shape, lambda *_: (0,) * len(shape))


def _layer_norm(x, g, b):
    mu = jnp.mean(x, axis=-1, keepdims=True)
    xc = x - mu
    var = jnp.mean(xc * xc, axis=-1, keepdims=True)
    return xc * lax.rsqrt(var + LN_EPS) * g + b


def _log_sigmoid(x):
    return jnp.minimum(x, 0.0) - jnp.log1p(jnp.exp(-jnp.abs(x)))


def _sigmoid(x):
    return 1.0 / (1.0 + jnp.exp(-x))


def _gelu_tanh(x):
    return 0.5 * x * (1.0 + jnp.tanh(math.sqrt(2.0 / math.pi) * (x + 0.044715 * (x * x * x))))


def _dot(a, b):
    return jnp.dot(a, b, preferred_element_type=F32)


def _dot_nt(a, b):
    return lax.dot_general(a, b, _NT, preferred_element_type=F32)


def _split3(x):
    hi = x.astype(BF16)
    r1 = x - hi.astype(F32)
    mid = r1.astype(BF16)
    lo = (r1 - mid.astype(F32)).astype(BF16)
    return hi, mid, lo


LANES = 128


def _lane_tiles_spec(tm, width):
    return pl.BlockSpec((1, width // LANES, tm, LANES), lambda b, i: (b, 0, i, 0))


def _store_lane_tiles(ref, x):
    for v in range(x.shape[-1] // LANES):
        ref[0, v] = x[:, v * LANES:(v + 1) * LANES]


def _load_lane_tiles(ref):
    return jnp.concatenate([ref[0, v] for v in range(ref.shape[1])], axis=-1)


def _dot_exact01(x, m01):
    hi, mid, lo = _split3(x)
    return _dot(hi, m01) + _dot(mid, m01) + _dot(lo, m01)


HEAD_SLOT = 2 * HEAD_DIM
N_PIECES = 3
PIECE_ROWS = 128
SUM_ROWS = 16


def _in_proj_kernel(*refs, tm, q_scale, attn_layouts):
    x_ref, g_ref, b_ref, wu_ref = refs[:4]
    if attn_layouts:
        (wqT_ref, wkT_ref, wvT_ref, wfT_ref, bfc_ref, tri_ref, place_ref,
         u_ref, kT_ref, vT_ref, lfT_ref, cT_ref, qTb_ref, k2_ref, vTb_ref, carry_ref) = refs[4:]
    else:
        wq_ref, wk_ref, wv_ref, wf_ref, bf_ref, u_ref, q_ref, k_ref, v_ref, lf_ref = refs[4:]
    hb = _layer_norm(x_ref[0], g_ref[...], b_ref[...]).astype(BF16)
    _store_lane_tiles(u_ref, _dot(hb, wu_ref[...]))
    if attn_layouts:
        qTb_ref[0] = (_dot_nt(wqT_ref[...], hb) * q_scale).astype(BF16)
        kT = _dot_nt(wkT_ref[...], hb)
        kT_ref[0] = kT
        vT = _dot_nt(wvT_ref[...], hb)
        vT_ref[0] = vT
        vTb_ref[0] = vT.astype(BF16)
        lfT = _log_sigmoid(_dot_nt(wfT_ref[...], hb) + bfc_ref[...])
        lfT_ref[0] = lfT

        @pl.when(pl.program_id(1) == 0)
        def _():
            carry_ref[...] = jnp.zeros_like(carry_ref)

        c = carry_ref[...] + _dot_exact01(lfT, tri_ref[...])
        cT_ref[0] = c
        carry_ref[...] = c[:, tm - 1:tm]
        hi, mid, lo = _split3(c * LOG2E)
        row = lax.broadcasted_iota(jnp.int32, (PIECE_ROWS - N_PIECES * N_HEADS, tm), 0)
        ones = jnp.where(row < N_PIECES, 1.0, 0.0)
        pieces = jnp.concatenate([hi.astype(F32), mid.astype(F32), lo.astype(F32), ones], axis=0)
        bias = _dot(pieces.T.astype(BF16), place_ref[...])
        k_tok = kT.T
        gap = jnp.zeros((tm, HEAD_SLOT - HEAD_DIM), F32)
        k_slots = jnp.concatenate([x for h in range(N_HEADS) for x in (k_tok[:, h * HEAD_DIM:(h + 1) * HEAD_DIM], gap)],
                                  axis=1)
        k2_ref[0] = (k_slots + bias).astype(BF16)
    else:
        q_ref[0] = _dot(hb, wq_ref[...]) * q_scale
        k_ref[0] = _dot(hb, wk_ref[...])
        v_ref[0] = _dot(hb, wv_ref[...])
        lf_ref[0] = _log_sigmoid(_dot(hb, wf_ref[...]) + bf_ref[...])


def _in_proj(x3, ln_g, ln_b, w_in, b_f, *, tm, q_scale, attn_layouts):
    nb, t, _ = x3.shape
    wq = w_in[:, :ATTN_WIDTH].astype(BF16)
    wk = w_in[:, ATTN_WIDTH:2 * ATTN_WIDTH].astype(BF16)
    wv = w_in[:, 2 * ATTN_WIDTH:3 * ATTN_WIDTH].astype(BF16)
    wf = w_in[:, 3 * ATTN_WIDTH:3 * ATTN_WIDTH + N_HEADS].astype(BF16)
    wu = w_in[:, 3 * ATTN_WIDTH + N_HEADS:].astype(BF16)
    tok = lambda w: pl.BlockSpec((1, tm, w), lambda b, i: (b, i, 0))
    feat = lambda w: pl.BlockSpec((1, w, tm), lambda b, i: (b, 0, i))
    sds = jax.ShapeDtypeStruct
    ins = [x3, ln_g.reshape(1, -1), ln_b.reshape(1, -1), wu]
    in_specs = [tok(D_MODEL), _const_spec((1, D_MODEL)), _const_spec((1, D_MODEL)), _const_spec((D_MODEL, SSM_WIDTH))]
    out_shape = [sds((nb, SSM_WIDTH // LANES, t, LANES), F32)]
    out_specs = [_lane_tiles_spec(tm, SSM_WIDTH)]
    scratch = []
    if attn_layouts:
        tri = (jnp.arange(tm)[:, None] <= jnp.arange(tm)[None, :]).astype(BF16)
        row = jnp.arange(PIECE_ROWS)[:, None]
        lane = jnp.arange(N_HEADS * HEAD_SLOT)[None, :]
        head, bias_lane = lane // HEAD_SLOT, lane % HEAD_SLOT - HEAD_DIM
        piece_row = (row < N_PIECES * N_HEADS) & (row % N_HEADS == head) & (row // N_HEADS == bias_lane)
        ones_row = (row >= N_PIECES * N_HEADS) & (row - N_PIECES * N_HEADS + N_PIECES == bias_lane) & (bias_lane < 2 * N_PIECES)
        place = (piece_row | ones_row).astype(BF16)
        ins += [wq.T, wk.T, wv.T, wf.T, b_f.reshape(N_HEADS, 1), tri, place]
        in_specs += [_const_spec((ATTN_WIDTH, D_MODEL))] * 3
        in_specs += [_const_spec((N_HEADS, D_MODEL)), _const_spec((N_HEADS, 1)), _const_spec((tm, tm)),
                     _const_spec((PIECE_ROWS, N_HEADS * HEAD_SLOT))]
        out_shape += [sds((nb, ATTN_WIDTH, t), F32), sds((nb, ATTN_WIDTH, t), F32),
                      sds((nb, N_HEADS, t), F32), sds((nb, N_HEADS, t), F32),
                      sds((nb, ATTN_WIDTH, t), BF16), sds((nb, t, N_HEADS * HEAD_SLOT), BF16), sds((nb, ATTN_WIDTH, t), BF16)]
        out_specs += [feat(ATTN_WIDTH), feat(ATTN_WIDTH), feat(N_HEADS), feat(N_HEADS),
                      feat(ATTN_WIDTH), tok(N_HEADS * HEAD_SLOT), feat(ATTN_WIDTH)]
        scratch = [pltpu.VMEM((N_HEADS, 1), F32)]
    else:
        ins += [wq, wk, wv, wf, b_f.reshape(1, N_HEADS)]
        in_specs += [_const_spec((D_MODEL, ATTN_WIDTH))] * 3 + [_const_spec((D_MODEL, N_HEADS)), _const_spec((1, N_HEADS))]
        out_shape += [sds((nb, t, ATTN_WIDTH), F32)] * 3 + [sds((nb, t, N_HEADS), F32)]
        out_specs += [tok(ATTN_WIDTH)] * 3 + [tok(N_HEADS)]
    return pl.pallas_call(
        functools.partial(_in_proj_kernel, tm=tm, q_scale=q_scale, attn_layouts=attn_layouts),
        grid=(nb, t // tm), in_specs=in_specs, out_specs=out_specs, out_shape=out_shape,
        scratch_shapes=scratch, compiler_params=_params("arbitrary", "arbitrary"),
        name="in_proj_prompt" if attn_layouts else "in_proj_sample")(*ins)


def _fox_prompt_kernel(qT_ref, k2_ref, vT_ref, cT_ref, o_ref, bq_ref, m_ref, acc_ref, s0_ref, s1_ref, *, tq):
    qi = pl.program_id(1)
    q0 = pl.multiple_of(qi * tq, tq)
    heads = [slice(h * HEAD_DIM, (h + 1) * HEAD_DIM) for h in range(N_HEADS)]
    slots = [slice(h * HEAD_SLOT, (h + 1) * HEAD_SLOT) for h in range(N_HEADS)]

    c0 = cT_ref[0, :, pl.ds(q0, tq)][:, 0:1] * LOG2E
    c0_pieces = [p.astype(F32) for p in _split3(c0)]
    r = lax.broadcasted_iota(jnp.int32, (HEAD_SLOT - HEAD_DIM, tq), 0)
    for h in range(N_HEADS):
        bq_ref[h, 0:HEAD_DIM, :] = qT_ref[0, heads[h], :]
        sel = jnp.where(r < N_PIECES, -1.0, 0.0)
        for j, piece in enumerate(c0_pieces):
            sel = jnp.where(r == N_PIECES + j, piece[h:h + 1, :], sel)
        bq_ref[h, HEAD_DIM:, :] = sel.astype(BF16)

    key_row = lax.broadcasted_iota(jnp.int32, (tq, tq), 0)
    qry_col = lax.broadcasted_iota(jnp.int32, (tq, tq), 1)

    m_ref[...] = jnp.full_like(m_ref, NEG_INF)
    acc_ref[...] = jnp.zeros_like(acc_ref)
    ones = jnp.ones((SUM_ROWS, tq), BF16)

    s_refs = (s0_ref, s1_ref)

    def scores(kt, h, buf):
        ks = pl.multiple_of(kt * tq, tq)
        s_refs[buf][h] = _dot(k2_ref[0, pl.ds(ks, tq), slots[h]], bq_ref[h])

    def accumulate(kt, h, buf, diagonal):
        ks = pl.multiple_of(kt * tq, tq)
        s = s_refs[buf][h]
        if diagonal:
            s = jnp.where(key_row <= qry_col, s, NEG_INF)
        m_new = jnp.maximum(m_ref[h], jnp.max(s, axis=0, keepdims=True))
        alpha = jnp.exp2(m_ref[h] - m_new)
        m_ref[h] = m_new
        p = jnp.exp2((s - m_new).astype(BF16))
        v1 = jnp.concatenate([vT_ref[0, heads[h], pl.ds(ks, tq)], ones], axis=0)
        acc_ref[h] = alpha * acc_ref[h] + _dot(v1, p)

    def stage(kt, cur, *, diagonal=False, issue_next=True):
        for h in range(N_HEADS + 1):
            if issue_next and h < N_HEADS:
                scores(kt + 1, h, 1 - cur)
            if h >= 1:
                accumulate(kt, h - 1, cur, diagonal)

    for h in range(N_HEADS):
        scores(0, h, 0)

    def body(j, carry):
        stage(2 * j, 0)
        stage(2 * j + 1, 1)
        return carry

    lax.fori_loop(0, qi // 2, body, 0)

    @pl.when(qi % 2 == 0)
    def _():
        stage(qi, 0, diagonal=True, issue_next=False)

    @pl.when(qi % 2 == 1)
    def _():
        stage(qi - 1, 0)
        stage(qi, 1, diagonal=True, issue_next=False)

    outs = [acc_ref[h, 0:HEAD_DIM, :] / acc_ref[h, HEAD_DIM:HEAD_DIM + 1, :] for h in range(N_HEADS)]
    o_ref[0] = jnp.concatenate(outs, axis=0).T.astype(o_ref.dtype)


def _fox_prompt(qT, k2, vT, cT, *, tq):
    nb, t, _ = k2.shape
    whole_f = lambda w: pl.BlockSpec((1, w, t), lambda b, i: (b, 0, 0))
    whole_t = lambda w: pl.BlockSpec((1, t, w), lambda b, i: (b, 0, 0))
    return pl.pallas_call(
        functools.partial(_fox_prompt_kernel, tq=tq),
        grid=(nb, t // tq),
        in_specs=[pl.BlockSpec((1, ATTN_WIDTH, tq), lambda b, i: (b, 0, i)),
                  whole_t(N_HEADS * HEAD_SLOT), whole_f(ATTN_WIDTH), whole_f(N_HEADS)],
        out_specs=pl.BlockSpec((1, tq, ATTN_WIDTH), lambda b, i: (b, i, 0)),
        out_shape=jax.ShapeDtypeStruct((nb, t, ATTN_WIDTH), BF16),
        scratch_shapes=[pltpu.VMEM((N_HEADS, HEAD_SLOT, tq), BF16),
                        pltpu.VMEM((N_HEADS, 1, tq), F32), pltpu.VMEM((N_HEADS, HEAD_DIM + SUM_ROWS, tq), F32),
                        pltpu.VMEM((N_HEADS, tq, tq), F32), pltpu.VMEM((N_HEADS, tq, tq), F32)],
        compiler_params=_params("arbitrary", "arbitrary"), name="fox_prompt")(qT, k2, vT, cT)


SAMPLE_SEQS_PER_STEP = 2

def _fox_sample_kernel(pt_ref, q_ref, kn_ref, vn_ref, lfn_ref, sl_ref, *refs, n_pages, nq, n_seq):
    del pt_ref
    o_ref = refs[3 * n_seq * n_pages]
    for e in range(n_seq):
        pages = [refs[(kind * n_seq + e) * n_pages:(kind * n_seq + e + 1) * n_pages] for kind in range(3)]
        _fox_sample_one(e, q_ref, kn_ref, vn_ref, lfn_ref, sl_ref, *pages, o_ref, n_pages=n_pages, nq=nq)


def _fox_sample_one(e, q_ref, kn_ref, vn_ref, lfn_ref, sl_ref, k_refs, v_refs, pf_refs, o_ref, *, n_pages, nq):
    nr = nq * N_HEADS
    head_of_lane = lax.broadcasted_iota(jnp.int32, (N_HEADS, ATTN_WIDTH), 1) // HEAD_DIM
    own = head_of_lane == lax.broadcasted_iota(jnp.int32, (N_HEADS, ATTN_WIDTH), 0)
    q = q_ref[e]
    qbd32 = jnp.concatenate([jnp.where(own, q[i:i + 1, :], 0.0) for i in range(nq)], axis=0)
    qbd = qbd32.astype(BF16)
    rep = lambda e: jnp.concatenate([e] * nq, axis=0)
    rows = lambda j: slice(j * N_HEADS, (j + 1) * N_HEADS)

    lfn = lfn_ref[e]
    e_new = rep(_dot_exact01(lfn, sl_ref[...]))
    kn, vn = kn_ref[e], vn_ref[e]
    qidx = lax.broadcasted_iota(jnp.int32, (nr, 1), 0) // N_HEADS
    s_new = [jnp.where(qidx >= j, jnp.sum(qbd32 * kn[j:j + 1, :], axis=-1, keepdims=True) + e_new[:, j:j + 1], NEG_INF)
             for j in range(nq)]

    pf_all = jnp.concatenate([r[0, 0] for r in pf_refs], axis=0)
    sfx_all = _dot_exact01(pf_all, sl_ref[...])
    tot_all = sfx_all[:, 0:1] + pf_all[:, 0:1]
    tail = jnp.sum(lfn, axis=-1, keepdims=True)
    scores = [None] * n_pages
    for j in reversed(range(n_pages)):
        kt = k_refs[j][0, 0].reshape(ATTN_WIDTH, PAGE_SIZE).astype(BF16)
        scores[j] = _dot(qbd, kt) + rep(sfx_all[rows(j)] + tail)
        tail = tail + tot_all[rows(j)]
    s_all = jnp.concatenate(scores, axis=1)
    m = jnp.max(s_all, axis=-1, keepdims=True)
    for s in s_new:
        m = jnp.maximum(m, s)
    p_all = jnp.exp(s_all - m)
    l = jnp.sum(p_all, axis=-1, keepdims=True)
    acc = jnp.zeros((nr, ATTN_WIDTH), F32)
    for j, s in enumerate(s_new):
        p = jnp.exp(s - m)
        l = l + p
        acc = acc + p * vn[j:j + 1, :]
    p_all = p_all.astype(BF16)
    for j in range(n_pages):
        vt = v_refs[j][0, 0].reshape(ATTN_WIDTH, PAGE_SIZE).astype(BF16)
        acc = acc + _dot_nt(p_all[:, j * PAGE_SIZE:(j + 1) * PAGE_SIZE], vt)
    acc = acc / l
    for i in range(nq):
        o_ref[e, i:i + 1, :] = jnp.sum(jnp.where(own, acc[i * N_HEADS:(i + 1) * N_HEADS], 0.0), axis=0, keepdims=True)


def _fox_sample(q, k_new, v_new, logf_new, cache_k, cache_v, cache_logf, page_table):
    db, nq, _ = q.shape
    n_pages = page_table.shape[1]
    kc = jnp.transpose(cache_k, (0, 2, 3, 1))
    vc = jnp.transpose(cache_v, (0, 2, 3, 1))
    pf = jnp.transpose(cache_logf, (0, 2, 1))
    lfn_t = jnp.swapaxes(jnp.pad(logf_new, ((0, 0), (0, PAGE_SIZE - nq), (0, 0))), 1, 2)
    strict_lower = (jnp.arange(PAGE_SIZE)[:, None] > jnp.arange(PAGE_SIZE)[None, :]).astype(BF16)
    ns = SAMPLE_SEQS_PER_STEP
    seq = lambda r, w: pl.BlockSpec((ns, r, w), lambda b, pt: (b, 0, 0))
    page = lambda e, j: pl.BlockSpec((1, 1, N_HEADS, HEAD_DIM, PAGE_SIZE), lambda b, pt: (0, pt[ns * b + e, j], 0, 0, 0))
    page_f = lambda e, j: pl.BlockSpec((1, 1, N_HEADS, PAGE_SIZE), lambda b, pt: (0, pt[ns * b + e, j], 0, 0))
    every = [(e, j) for e in range(ns) for j in range(n_pages)]
    in_specs = [seq(nq, ATTN_WIDTH), seq(nq, ATTN_WIDTH), seq(nq, ATTN_WIDTH), seq(N_HEADS, PAGE_SIZE),
                pl.BlockSpec((PAGE_SIZE, PAGE_SIZE), lambda b, pt: (0, 0))]
    in_specs += [page(e, j) for e, j in every] * 2 + [page_f(e, j) for e, j in every]
    return pl.pallas_call(
        functools.partial(_fox_sample_kernel, n_pages=n_pages, nq=nq, n_seq=ns),
        grid_spec=pltpu.PrefetchScalarGridSpec(
            num_scalar_prefetch=1, grid=(db // ns,), in_specs=in_specs, out_specs=seq(nq, ATTN_WIDTH)),
        out_shape=jax.ShapeDtypeStruct((db, nq, ATTN_WIDTH), F32),
        compiler_params=_params("arbitrary"), name="fox_sample")(
            page_table, q, k_new, v_new, lfn_t, strict_lower,
            *([kc[None]] * len(every)), *([vc[None]] * len(every)), *([pf[None]] * len(every)))


PREP_GROUPS = 2

def _cmul(ar, ai, br, bi):
    return ar * br - ai * bi, ar * bi + ai * br


def _cpow_by_bits(ar, ai, n, nbits):
    pr = jnp.ones(jnp.broadcast_shapes(ar.shape, n.shape), F32)
    pi = jnp.zeros_like(pr)
    for b in range(nbits):
        bit = ((n >> b) & 1) == 1
        fr = jnp.where(bit, ar, 1.0)
        fi = jnp.where(bit, ai, 0.0)
        pr, pi = _cmul(pr, pi, fr, fi)
        ar, ai = _cmul(ar, ai, ar, ai)
    return pr, pi


def _ssm_prep_kernel(*refs, chunk, small, n_groups):
    for e in range(n_groups):
        _ssm_prep_group(*[r.at[pl.ds(e, 1)] for r in refs], chunk=chunk, small=small)


def _ssm_prep_group(ldt_ref, lr_ref, li_ref, bre_ref, bim_ref, cre_ref, cim_ref,
                    toep_ref, ctl_ref, obs_ref, al_ref,
                    toeps_ref, ctlrs_ref, ctlis_ref, obsrs_ref, obsis_ref, asr_ref, asi_ref, *, chunk, small):
    lw, sw = chunk * SSM_GROUP, small * SSM_GROUP
    p = STATE_DIM
    nbits = chunk.bit_length()
    dt = jnp.exp(ldt_ref[0])
    lr_row, li_row = lr_ref[0], li_ref[0]

    def discretise(lr, li):
        mag = jnp.exp(lr * dt)
        return mag * jnp.cos(li * dt), mag * jnp.sin(li * dt)

    def column(row):
        eye = lax.broadcasted_iota(jnp.int32, (p, p), 0) == lax.broadcasted_iota(jnp.int32, (p, p), 1)
        return jnp.sum(jnp.where(eye, row, 0.0), axis=-1, keepdims=True)

    lr, li = column(lr_row), column(li_row)
    ar, ai = discretise(lr, li)
    den = lr * lr + li * li
    nr, ni = ar - 1.0, ai
    cr, ci = (nr * lr + ni * li) / den, (ni * lr - nr * li) / den
    btr = jnp.concatenate([bre_ref[0]] * (2 * lw // LANES), axis=1)
    bti = jnp.concatenate([bim_ref[0]] * (2 * lw // LANES), axis=1)
    bbr, bbi = _cmul(cr, ci, btr, bti)
    sig = lax.broadcasted_iota(jnp.int32, (1, 2 * lw), 1) // SSM_GROUP
    er, ei = _cpow_by_bits(ar, ai, jnp.maximum(chunk - 1 - sig, 0), nbits)
    hr, hi = _cmul(er, ei, bbr, bbi)
    hr = jnp.where(sig < chunk, hr, 0.0)
    hi = jnp.where(sig < chunk, hi, 0.0)
    ctl_ref[0] = jnp.concatenate([hr[:, :lw], hi[:, :lw]], axis=0).astype(BF16)
    ctlrs_ref[0] = hr[:, lw - sw:lw]
    ctlis_ref[0] = hi[:, lw - sw:lw]
    c_re, c_im = cre_ref[0], cim_ref[0]
    hp = lax.Precision.HIGHEST
    k_all = (jnp.dot(c_re, hr, precision=hp, preferred_element_type=F32)
             - jnp.dot(c_im, hi, precision=hp, preferred_element_type=F32))
    for t in range(chunk):
        off = (chunk - 1 - t) * SSM_GROUP
        toep_ref[0, t * SSM_GROUP:(t + 1) * SSM_GROUP, :] = k_all[:, off:off + lw].astype(BF16)
        if t < small:
            toeps_ref[0, t * SSM_GROUP:(t + 1) * SSM_GROUP, :] = k_all[:, off:off + sw]

    arr, air = discretise(lr_row, li_row)
    tp1 = lax.broadcasted_iota(jnp.int32, (lw, 1), 0) // SSM_GROUP + 1
    pr, pi = _cpow_by_bits(arr, air, tp1, nbits)
    c_r = jnp.concatenate([c_re] * chunk, axis=0)
    c_i = jnp.concatenate([c_im] * chunk, axis=0)
    obs_re = c_r * pr - c_i * pi
    obs_im = -(c_r * pi + c_i * pr)
    obs_ref[0] = jnp.concatenate([obs_re, obs_im], axis=1).astype(BF16)
    obsrs_ref[0] = obs_re[:sw]
    obsis_ref[0] = obs_im[:sw]
    n = 1
    while n < chunk:
        arr, air = _cmul(arr, air, arr, air)
        n *= 2
        if n == small:
            asr_ref[0] = arr
            asi_ref[0] = air
    al_ref[0] = jnp.concatenate([jnp.concatenate([arr, arr], axis=1), jnp.concatenate([-air, air], axis=1)], axis=0)


def _ssm_prep(lam_re, lam_im, log_dt, b_re, b_im, c_re, c_im, *, chunk, small):
    assert chunk & (chunk - 1) == 0 and small & (small - 1) == 0 and 1 < small < chunk
    g, p, j = N_GROUPS, STATE_DIM, SSM_GROUP
    lw, sw = chunk * j, small * j
    lane_tile = lambda b: jnp.tile(b, (1, 1, LANES // j))
    ins = [log_dt.reshape(g, 1, 1), lam_re.reshape(g, 1, p), lam_im.reshape(g, 1, p), lane_tile(b_re), lane_tile(b_im),
           c_re, c_im]
    grp = lambda a, b: pl.BlockSpec((PREP_GROUPS, a, b), lambda i: (i, 0, 0))
    in_specs = [grp(1, 1), grp(1, p), grp(1, p), grp(p, LANES), grp(p, LANES), grp(j, p), grp(j, p)]
    shapes = [((lw, lw), BF16), ((2 * p, lw), BF16), ((lw, 2 * p), BF16), ((2, 2 * p), F32),
              ((sw, sw), F32), ((p, sw), F32), ((p, sw), F32), ((sw, p), F32), ((sw, p), F32), ((1, p), F32), ((1, p), F32)]
    outs = pl.pallas_call(
        functools.partial(_ssm_prep_kernel, chunk=chunk, small=small, n_groups=PREP_GROUPS),
        grid=(g // PREP_GROUPS,), in_specs=in_specs,
        out_specs=[grp(*s) for s, _ in shapes],
        out_shape=[jax.ShapeDtypeStruct((g,) + s, d) for s, d in shapes],
        compiler_params=_params("arbitrary"), name="ssm_prep")(*ins)
    return outs[:4], outs[4:]


def _ssm_step_kernel(u_ref, toep_ref, ctlr_ref, ctli_ref, obsr_ref, obsi_ref, ar_ref, ai_ref, h0r_ref, h0i_ref,
                     y_ref, hr_ref, hi_ref, *, chunk):
    n = GROUPS_PER_TILE
    n_seq = h0r_ref.shape[1]
    lw = chunk * SSM_GROUP
    zero = jnp.zeros((n_seq, LANES), F32)
    xs = [u_ref[0, 0, pl.ds(t, n_seq, stride=chunk), :] if t < chunk else zero for t in range(n)]
    ys = []
    for e, x in enumerate(_block_transpose(xs)):
        u = x[:, :lw].astype(BF16)
        h0r, h0i = h0r_ref[e], h0i_ref[e]
        er, ei = _cmul(ar_ref[e], ai_ref[e], h0r, h0i)
        hr_ref[e] = _dot_nt(u, ctlr_ref[e].astype(BF16)) + er
        hi_ref[e] = _dot_nt(u, ctli_ref[e].astype(BF16)) + ei
        y = (_dot_nt(u, toep_ref[e].astype(BF16)) + _dot_nt(h0r.astype(BF16), obsr_ref[e].astype(BF16))
             + _dot_nt(h0i.astype(BF16), obsi_ref[e].astype(BF16)))
        ys.append(jnp.concatenate([y, jnp.zeros((n_seq, LANES - lw), F32)], axis=1))
    for t, y in enumerate(_block_transpose(ys)[:chunk]):
        y_ref[0, 0, pl.ds(t, n_seq, stride=chunk), :] = y


def _ssm_step(u, ops, h0, *, chunk):
    n_seq = h0[0].shape[0]
    g, p, n = N_GROUPS, STATE_DIM, GROUPS_PER_TILE
    lw = chunk * SSM_GROUP
    assert lw <= LANES and u.shape == (1, g // n, n_seq * chunk, LANES)
    grp = lambda a, b: pl.BlockSpec((n, a, b), lambda v: (v, 0, 0))
    tile = pl.BlockSpec((1, 1, n_seq * chunk, LANES), lambda v: (0, v, 0, 0))
    y, hr, hi = pl.pallas_call(
        functools.partial(_ssm_step_kernel, chunk=chunk), grid=(g // n,),
        in_specs=[tile, grp(lw, lw), grp(p, lw), grp(p, lw), grp(lw, p), grp(lw, p), grp(1, p), grp(1, p),
                  grp(n_seq, p), grp(n_seq, p)],
        out_specs=[tile, grp(n_seq, p), grp(n_seq, p)],
        out_shape=[jax.ShapeDtypeStruct(u.shape, F32), jax.ShapeDtypeStruct((g, n_seq, p), F32),
                   jax.ShapeDtypeStruct((g, n_seq, p), F32)],
        compiler_params=_params("arbitrary"), name="ssm_step")(
            u, *ops, jnp.swapaxes(h0[0], 0, 1), jnp.swapaxes(h0[1], 0, 1))
    return y, jnp.swapaxes(hr, 0, 1), jnp.swapaxes(hi, 0, 1)


GROUPS_PER_TILE = LANES // SSM_GROUP
GROUPS_PER_TRIP = 8


def _block_transpose(xs):
    n = len(xs)
    blk = lax.broadcasted_iota(jnp.int32, xs[0].shape, 1) // SSM_GROUP
    xs = list(xs)
    d = n // 2
    while d:
        upper = (blk & d) != 0
        for i in range(n):
            if not i & d:
                lo, hi = xs[i], xs[i + d]
                xs[i] = jnp.where(upper, pltpu.roll(hi, SSM_GROUP * d, axis=1), lo)
                xs[i + d] = jnp.where(upper, hi, pltpu.roll(lo, LANES - SSM_GROUP * d, axis=1))
        d //= 2
    return xs


def _ssm_seq_kernel(u_ref, toep_ref, ctl_ref, obs_ref, al_ref, y_ref, h_ref, ug_ref, yg_ref, *, chunk):
    t = u_ref.shape[2]
    rows = t // chunk
    lw = chunk * SSM_GROUP
    n = GROUPS_PER_TILE
    p = STATE_DIM
    ridx = lax.broadcasted_iota(jnp.int32, (rows, 1), 0)

    def shifted(x, d):
        return jnp.where(ridx >= d, pltpu.roll(x, d, axis=0), 0.0)

    def group_tile(v, carry):
        for w in range(lw // LANES):
            xs = [u_ref[0, v, pl.ds(n * w + k, rows, stride=chunk), :] for k in range(n)]
            for k, x in enumerate(_block_transpose(xs)):
                ug_ref[k, :, w * LANES:(w + 1) * LANES] = x.astype(BF16)

        def groups(i, c):
            ks = [i * GROUPS_PER_TRIP + e for e in range(GROUPS_PER_TRIP)]
            gs = [v * n + k for k in ks]
            ugs = [ug_ref[k] for k in ks]
            ys = [_dot_nt(ug, toep_ref[g]) for ug, g in zip(ugs, gs)]
            hs = [_dot_nt(ug, ctl_ref[g]) for ug, g in zip(ugs, gs)]
            ars = [al_ref[g][0:1, :] for g in gs]
            ais = [al_ref[g][1:2, :] for g in gs]
            d = 1
            while d < rows:
                for e in range(GROUPS_PER_TRIP):
                    sh = shifted(hs[e], d)
                    hs[e] = hs[e] + ars[e] * sh + ais[e] * pltpu.roll(sh, p, axis=1)
                    ars[e], ais[e] = ars[e] * ars[e] - ais[e] * ais[e], 2.0 * ars[e] * ais[e]
                d *= 2
            for e, (k, g) in enumerate(zip(ks, gs)):
                h_ref[0, pl.ds(g, 1), :] = hs[e][rows - 1:rows, :]
                yg_ref[k] = ys[e] + _dot_nt(shifted(hs[e], 1).astype(BF16), obs_ref[g])
            return c

        lax.fori_loop(0, n // GROUPS_PER_TRIP, groups, 0)
        for w in range(lw // LANES):
            ys = [yg_ref[k, :, w * LANES:(w + 1) * LANES] for k in range(n)]
            for k, y in enumerate(_block_transpose(ys)):
                y_ref[0, v, pl.ds(n * w + k, rows, stride=chunk), :] = y
        return carry

    lax.fori_loop(0, N_GROUPS // n, group_tile, 0)


def _ssm_seq(u, ops, *, chunk):
    n_seq, _, t, _ = u.shape
    g, p = N_GROUPS, STATE_DIM
    lw = chunk * SSM_GROUP
    assert lw % LANES == 0 and chunk == GROUPS_PER_TILE * (lw // LANES) and SSM_WIDTH == g * SSM_GROUP
    whole = lambda a: pl.BlockSpec(a.shape, lambda b: (0, 0, 0), pipeline_mode=pl.Buffered(1))
    seq = pl.BlockSpec((1, SSM_WIDTH // LANES, t, LANES), lambda b: (b, 0, 0, 0))
    y, h = pl.pallas_call(
        functools.partial(_ssm_seq_kernel, chunk=chunk), grid=(n_seq,),
        in_specs=[seq] + [whole(o) for o in ops],
        out_specs=[seq, pl.BlockSpec((1, g, 2 * p), lambda b: (b, 0, 0))],
        out_shape=[jax.ShapeDtypeStruct(u.shape, F32), jax.ShapeDtypeStruct((n_seq, g, 2 * p), F32)],
        scratch_shapes=[pltpu.VMEM((GROUPS_PER_TILE, t // chunk, lw), BF16),
                        pltpu.VMEM((GROUPS_PER_TILE, t // chunk, lw), F32)],
        compiler_params=_params("arbitrary"), name="ssm_seq")(u, *ops)
    return y, h[:, :, :p], h[:, :, p:]


FF_CHUNK = 1024


def _post_kernel(x_ref, att_ref, ys_ref, u_ref, p_ref, lng_ref, lnb_ref, dsk_ref, wglu_ref, bglu_ref,
                 woa_ref, wos_ref, l1g_ref, l1b_ref, wup_ref, wdn_ref, wpe_ref, wpg_ref, bpg_ref,
                 l2g_ref, l2b_ref, o_ref):
    h = _layer_norm(x_ref[0], lng_ref[...], lnb_ref[...])
    y = _gelu_tanh(_load_lane_tiles(ys_ref) + dsk_ref[...] * _load_lane_tiles(u_ref))
    y = y * _sigmoid(_dot(y.astype(BF16), wglu_ref[...]) + bglu_ref[...])
    mix = _dot(att_ref[0], woa_ref[...]) + _dot(y.astype(BF16), wos_ref[...])
    h1 = _layer_norm(ALPHA * h + mix, l1g_ref[...], l1b_ref[...])
    h1b = h1.astype(BF16)
    e = _sigmoid(_dot(h1b, wpg_ref[...]) + bpg_ref[...]) * _dot(p_ref[0].astype(BF16), wpe_ref[...])
    acc = ALPHA * h1 + e
    for c in range(D_FF // FF_CHUNK):
        cs = slice(c * FF_CHUNK, (c + 1) * FF_CHUNK)
        a = jnp.maximum(_dot(h1b, wup_ref[:, cs]), 0.0)
        acc = acc + _dot((a * a).astype(BF16), wdn_ref[cs, :])
    o_ref[0] = _layer_norm(acc, l2g_ref[...], l2b_ref[...])


def _post_mixer(x, att, ys, u, p, weights, *, tm):
    nb, t, _ = x.shape
    tok = lambda w: pl.BlockSpec((1, tm, w), lambda b, i: (b, i, 0))
    const = lambda a: pl.BlockSpec(a.shape, lambda b, i: (0, 0), pipeline_mode=pl.Buffered(1))
    return pl.pallas_call(
        _post_kernel, grid=(nb, t // tm),
        in_specs=[tok(D_MODEL), tok(ATTN_WIDTH), _lane_tiles_spec(tm, SSM_WIDTH), _lane_tiles_spec(tm, SSM_WIDTH),
                  tok(PLE_DIM)] + [const(w) for w in weights],
        out_specs=tok(D_MODEL), out_shape=jax.ShapeDtypeStruct((nb, t, D_MODEL), F32),
        compiler_params=_params("arbitrary", "arbitrary"), name="post_mixer")(x, att, ys, u, p, *weights)


def kernel(x_prompt, x_sample, cache_k, cache_v, cache_logf, state_re, state_im, page_table, p_prompt, p_sample,
           ln_in_g, ln_in_b, w_in, b_f, lam_re, lam_im, log_dt, b_re, b_im, c_re, c_im, d_skip, w_glu, b_glu,
           w_out, ln1_g, ln1_b, w_up, w_down, w_pe, w_pg, b_pg, ln2_g, ln2_b):
    assert w_in.shape[0] == 1, "one trunk layer"
    nb, t, _ = x_prompt.shape
    db, nq, _ = x_sample.shape
    row = lambda a: a.reshape(1, -1)
    post_w = [row(ln_in_g), row(ln_in_b), row(d_skip[0]), w_glu[0].astype(BF16), row(b_glu[0]),
              w_out[0, :ATTN_WIDTH].astype(BF16), w_out[0, ATTN_WIDTH:].astype(BF16), row(ln1_g[0]), row(ln1_b[0]),
              w_up[0].astype(BF16), w_down[0].astype(BF16), w_pe[0].astype(BF16), w_pg[0].astype(BF16),
              row(b_pg[0]), row(ln2_g[0]), row(ln2_b[0])]
    ssm_par = (lam_re[0], lam_im[0], log_dt[0], b_re[0], b_im[0], c_re[0], c_im[0])

    u, kT, vT, lfT, cT, qTb, k2, vTb = _in_proj(x_prompt, ln_in_g, ln_in_b, w_in[0], b_f[0],
                                                     tm=IN_PROJ_ROWS, q_scale=QK_SCALE * LOG2E, attn_layouts=True)
    att = _fox_prompt(qTb, k2, vTb, cT, tq=ATTN_TILE)
    ssm_ops, ssm_ops_sample = _ssm_prep(*ssm_par, chunk=SSM_CHUNK, small=nq)
    ys, sr, si = _ssm_seq(u, ssm_ops, chunk=SSM_CHUNK)
    y_prompt = _post_mixer(x_prompt, att, ys, u, p_prompt[0], post_w, tm=POST_ROWS)
    heads_last = lambda a: jnp.transpose(a.reshape(nb, N_HEADS, HEAD_DIM, t), (0, 3, 1, 2))[None]
    prompt_out = (heads_last(kT), heads_last(vT), jnp.swapaxes(lfT, 1, 2)[None], sr[None], si[None])

    us, qs, ks, vs, logfs = _in_proj(x_sample.reshape(1, db * nq, D_MODEL), ln_in_g, ln_in_b, w_in[0], b_f[0],
                                     tm=db * nq, q_scale=QK_SCALE, attn_layouts=False)
    seq = lambda a: a.reshape(db, nq, a.shape[-1])
    att_s = _fox_sample(seq(qs), seq(ks), seq(vs), seq(logfs), cache_k[0], cache_v[0], cache_logf[0], page_table)
    ys_s, sr_s, si_s = _ssm_step(us, ssm_ops_sample, (state_re[0], state_im[0]), chunk=nq)
    flat = lambda a: a.reshape(1, db * nq, a.shape[-1])
    y_sample = _post_mixer(flat(x_sample), flat(att_s).astype(BF16), ys_s, us, flat(p_sample[0]),
                           post_w, tm=db * nq).reshape(db, nq, D_MODEL)
    sample_out = (ks.reshape(1, db, nq, N_HEADS, HEAD_DIM), vs.reshape(1, db, nq, N_HEADS, HEAD_DIM),
                  logfs.reshape(1, db, nq, N_HEADS), sr_s[None], si_s[None])
    return (y_prompt, y_sample) + prompt_out + sample_out
```

```python
import functools
import math

import jax
import jax.numpy as jnp
from jax import lax
from jax.experimental import pallas as pl
from jax.experimental.pallas import tpu as pltpu

F32 = jnp.float32
BF16 = jnp.bfloat16

D_MODEL = 1024
ATTN_WIDTH = 512
SSM_WIDTH = 512
HEAD_DIM = 64
N_HEADS = 8
SSM_GROUP = 16
N_GROUPS = 32
STATE_DIM = 64
D_FF = 4096
PLE_DIM = 256
PAGE_SIZE = 128
ALPHA = 2.0 ** 0.25
LN_EPS = 1e-5
NEG_INF = -1e30
QK_SCALE = HEAD_DIM ** -0.5
LOG2E = math.log2(math.e)

VMEM_LIMIT_BYTES = 56 * 1024 * 1024

IN_PROJ_ROWS = 512
ATTN_TILE = 256
SSM_CHUNK = 16
POST_ROWS = 512

_NT = (((1,), (1,)), ((), ()))


def _params(*sem):
    return pltpu.CompilerParams(dimension_semantics=sem, vmem_limit_bytes=VMEM_LIMIT_BYTES)


def _const_spec(shape):
    return pl.BlockSpec(shape, lambda *_: (0,) * len(shape))


def _layer_norm(x, g, b):
    mu = jnp.mean(x, axis=-1, keepdims=True)
    xc = x - mu
    var = jnp.mean(xc * xc, axis=-1, keepdims=True)
    return xc * lax.rsqrt(var + LN_EPS) * g + b


def _log_sigmoid(x):
    return jnp.minimum(x, 0.0) - jnp.log1p(jnp.exp(-jnp.abs(x)))


def _sigmoid(x):
    return 1.0 / (1.0 + jnp.exp(-x))


def _gelu_tanh(x):
    return 0.5 * x * (1.0 + jnp.tanh(math.sqrt(2.0 / math.pi) * (x + 0.044715 * (x * x * x))))


def _dot(a, b):
    return jnp.dot(a, b, preferred_element_type=F32)


def _dot_nt(a, b):
    return lax.dot_general(a, b, _NT, preferred_element_type=F32)


def _split3(x):
    hi = x.astype(BF16)
    r1 = x - hi.astype(F32)
    mid = r1.astype(BF16)
    lo = (r1 - mid.astype(F32)).astype(BF16)
    return hi, mid, lo


LANES = 128


def _lane_tiles_spec(tm, width):
    return pl.BlockSpec((1, width // LANES, tm, LANES), lambda b, i: (b, 0, i, 0))


def _store_lane_tiles(ref, x):
    for v in range(x.shape[-1] // LANES):
        ref[0, v] = x[:, v * LANES:(v + 1) * LANES]


def _load_lane_tiles(ref):
    return jnp.concatenate([ref[0, v] for v in range(ref.shape[1])], axis=-1)


def _dot_exact01(x, m01):
    hi, mid, lo = _split3(x)
    return _dot(hi, m01) + _dot(mid, m01) + _dot(lo, m01)


HEAD_SLOT = 2 * HEAD_DIM
N_PIECES = 3
PIECE_ROWS = 128
SUM_ROWS = 16


def _in_proj_kernel(*refs, tm, q_scale, attn_layouts):
    x_ref, g_ref, b_ref, wu_ref = refs[:4]
    if attn_layouts:
        (wqT_ref, wkT_ref, wvT_ref, wfT_ref, bfc_ref, tri_ref, place_ref,
         u_ref, kT_ref, vT_ref, lfT_ref, cT_ref, qTb_ref, k2_ref, vTb_ref, carry_ref) = refs[4:]
    else:
        wq_ref, wk_ref, wv_ref, wf_ref, bf_ref, u_ref, q_ref, k_ref, v_ref, lf_ref = refs[4:]
    hb = _layer_norm(x_ref[0], g_ref[...], b_ref[...]).astype(BF16)
    _store_lane_tiles(u_ref, _dot(hb, wu_ref[...]))
    if attn_layouts:
        qTb_ref[0] = (_dot_nt(wqT_ref[...], hb) * q_scale).astype(BF16)
        kT = _dot_nt(wkT_ref[...], hb)
        kT_ref[0] = kT
        vT = _dot_nt(wvT_ref[...], hb)
        vT_ref[0] = vT
        vTb_ref[0] = vT.astype(BF16)
        lfT = _log_sigmoid(_dot_nt(wfT_ref[...], hb) + bfc_ref[...])
        lfT_ref[0] = lfT

        @pl.when(pl.program_id(1) == 0)
        def _():
            carry_ref[...] = jnp.zeros_like(carry_ref)

        c = carry_ref[...] + _dot_exact01(lfT, tri_ref[...])
        cT_ref[0] = c
        carry_ref[...] = c[:, tm - 1:tm]
        hi, mid, lo = _split3(c * LOG2E)
        row = lax.broadcasted_iota(jnp.int32, (PIECE_ROWS - N_PIECES * N_HEADS, tm), 0)
        ones = jnp.where(row < N_PIECES, 1.0, 0.0)
        pieces = jnp.concatenate([hi.astype(F32), mid.astype(F32), lo.astype(F32), ones], axis=0)
        bias = _dot(pieces.T.astype(BF16), place_ref[...])
        k_tok = kT.T
        gap = jnp.zeros((tm, HEAD_SLOT - HEAD_DIM), F32)
        k_slots = jnp.concatenate([x for h in range(N_HEADS) for x in (k_tok[:, h * HEAD_DIM:(h + 1) * HEAD_DIM], gap)],
                                  axis=1)
        k2_ref[0] = (k_slots + bias).astype(BF16)
    else:
        q_ref[0] = _dot(hb, wq_ref[...]) * q_scale
        k_ref[0] = _dot(hb, wk_ref[...])
        v_ref[0] = _dot(hb, wv_ref[...])
        lf_ref[0] = _log_sigmoid(_dot(hb, wf_ref[...]) + bf_ref[...])


def _in_proj(x3, ln_g, ln_b, w_in, b_f, *, tm, q_scale, attn_layouts):
    nb, t, _ = x3.shape
    wq = w_in[:, :ATTN_WIDTH].astype(BF16)
    wk = w_in[:, ATTN_WIDTH:2 * ATTN_WIDTH].astype(BF16)
    wv = w_in[:, 2 * ATTN_WIDTH:3 * ATTN_WIDTH].astype(BF16)
    wf = w_in[:, 3 * ATTN_WIDTH:3 * ATTN_WIDTH + N_HEADS].astype(BF16)
    wu = w_in[:, 3 * ATTN_WIDTH + N_HEADS:].astype(BF16)
    tok = lambda w: pl.BlockSpec((1, tm, w), lambda b, i: (b, i, 0))
    feat = lambda w: pl.BlockSpec((1, w, tm), lambda b, i: (b, 0, i))
    sds = jax.ShapeDtypeStruct
    ins = [x3, ln_g.reshape(1, -1), ln_b.reshape(1, -1), wu]
    in_specs = [tok(D_MODEL), _const_spec((1, D_MODEL)), _const_spec((1, D_MODEL)), _const_spec((D_MODEL, SSM_WIDTH))]
    out_shape = [sds((nb, SSM_WIDTH // LANES, t, LANES), F32)]
    out_specs = [_lane_tiles_spec(tm, SSM_WIDTH)]
    scratch = []
    if attn_layouts:
        tri = (jnp.arange(tm)[:, None] <= jnp.arange(tm)[None, :]).astype(BF16)
        row = jnp.arange(PIECE_ROWS)[:, None]
        lane = jnp.arange(N_HEADS * HEAD_SLOT)[None, :]
        head, bias_lane = lane // HEAD_SLOT, lane % HEAD_SLOT - HEAD_DIM
        piece_row = (row < N_PIECES * N_HEADS) & (row % N_HEADS == head) & (row // N_HEADS == bias_lane)
        ones_row = (row >= N_PIECES * N_HEADS) & (row - N_PIECES * N_HEADS + N_PIECES == bias_lane) & (bias_lane < 2 * N_PIECES)
        place = (piece_row | ones_row).astype(BF16)
        ins += [wq.T, wk.T, wv.T, wf.T, b_f.reshape(N_HEADS, 1), tri, place]
        in_specs += [_const_spec((ATTN_WIDTH, D_MODEL))] * 3
        in_specs += [_const_spec((N_HEADS, D_MODEL)), _const_spec((N_HEADS, 1)), _const_spec((tm, tm)),
                     _const_spec((PIECE_ROWS, N_HEADS * HEAD_SLOT))]
        out_shape += [sds((nb, ATTN_WIDTH, t), F32), sds((nb, ATTN_WIDTH, t), F32),
                      sds((nb, N_HEADS, t), F32), sds((nb, N_HEADS, t), F32),
                      sds((nb, ATTN_WIDTH, t), BF16), sds((nb, t, N_HEADS * HEAD_SLOT), BF16), sds((nb, ATTN_WIDTH, t), BF16)]
        out_specs += [feat(ATTN_WIDTH), feat(ATTN_WIDTH), feat(N_HEADS), feat(N_HEADS),
                      feat(ATTN_WIDTH), tok(N_HEADS * HEAD_SLOT), feat(ATTN_WIDTH)]
        scratch = [pltpu.VMEM((N_HEADS, 1), F32)]
    else:
        ins += [wq, wk, wv, wf, b_f.reshape(1, N_HEADS)]
        in_specs += [_const_spec((D_MODEL, ATTN_WIDTH))] * 3 + [_const_spec((D_MODEL, N_HEADS)), _const_spec((1, N_HEADS))]
        out_shape += [sds((nb, t, ATTN_WIDTH), F32)] * 3 + [sds((nb, t, N_HEADS), F32)]
        out_specs += [tok(ATTN_WIDTH)] * 3 + [tok(N_HEADS)]
    return pl.pallas_call(
        functools.partial(_in_proj_kernel, tm=tm, q_scale=q_scale, attn_layouts=attn_layouts),
        grid=(nb, t // tm), in_specs=in_specs, out_specs=out_specs, out_shape=out_shape,
        scratch_shapes=scratch, compiler_params=_params("arbitrary", "arbitrary"),
        name="in_proj_prompt" if attn_layouts else "in_proj_sample")(*ins)


def _fox_prompt_kernel(qT_ref, k2_ref, vT_ref, cT_ref, o_ref, bq_ref, m_ref, acc_ref, s0_ref, s1_ref, *, tq):
    qi = pl.program_id(1)
    q0 = pl.multiple_of(qi * tq, tq)
    heads = [slice(h * HEAD_DIM, (h + 1) * HEAD_DIM) for h in range(N_HEADS)]
    slots = [slice(h * HEAD_SLOT, (h + 1) * HEAD_SLOT) for h in range(N_HEADS)]

    c0 = cT_ref[0, :, pl.ds(q0, tq)][:, 0:1] * LOG2E
    c0_pieces = [p.astype(F32) for p in _split3(c0)]
    r = lax.broadcasted_iota(jnp.int32, (HEAD_SLOT - HEAD_DIM, tq), 0)
    for h in range(N_HEADS):
        bq_ref[h, 0:HEAD_DIM, :] = qT_ref[0, heads[h], :]
        sel = jnp.where(r < N_PIECES, -1.0, 0.0)
        for j, piece in enumerate(c0_pieces):
            sel = jnp.where(r == N_PIECES + j, piece[h:h + 1, :], sel)
        bq_ref[h, HEAD_DIM:, :] = sel.astype(BF16)

    key_row = lax.broadcasted_iota(jnp.int32, (tq, tq), 0)
    qry_col = lax.broadcasted_iota(jnp.int32, (tq, tq), 1)

    m_ref[...] = jnp.full_like(m_ref, NEG_INF)
    acc_ref[...] = jnp.zeros_like(acc_ref)
    ones = jnp.ones((SUM_ROWS, tq), BF16)

    s_refs = (s0_ref, s1_ref)

    def scores(kt, h, buf):
        ks = pl.multiple_of(kt * tq, tq)
        s_refs[buf][h] = _dot(k2_ref[0, pl.ds(ks, tq), slots[h]], bq_ref[h])

    def accumulate(kt, h, buf, diagonal):
        ks = pl.multiple_of(kt * tq, tq)
        s = s_refs[buf][h]
        if diagonal:
            s = jnp.where(key_row <= qry_col, s, NEG_INF)
        m_new = jnp.maximum(m_ref[h], jnp.max(s, axis=0, keepdims=True))
        alpha = jnp.exp2(m_ref[h] - m_new)
        m_ref[h] = m_new
        p = jnp.exp2((s - m_new).astype(BF16))
        v1 = jnp.concatenate([vT_ref[0, heads[h], pl.ds(ks, tq)], ones], axis=0)
        acc_ref[h] = alpha * acc_ref[h] + _dot(v1, p)

    def stage(kt, cur, *, diagonal=False, issue_next=True):
        for h in range(N_HEADS + 1):
            if issue_next and h < N_HEADS:
                scores(kt + 1, h, 1 - cur)
            if h >= 1:
                accumulate(kt, h - 1, cur, diagonal)

    for h in range(N_HEADS):
        scores(0, h, 0)

    def body(j, carry):
        stage(2 * j, 0)
        stage(2 * j + 1, 1)
        return carry

    lax.fori_loop(0, qi // 2, body, 0)

    @pl.when(qi % 2 == 0)
    def _():
        stage(qi, 0, diagonal=True, issue_next=False)

    @pl.when(qi % 2 == 1)
    def _():
        stage(qi - 1, 0)
        stage(qi, 1, diagonal=True, issue_next=False)

    outs = [acc_ref[h, 0:HEAD_DIM, :] / acc_ref[h, HEAD_DIM:HEAD_DIM + 1, :] for h in range(N_HEADS)]
    o_ref[0] = jnp.concatenate(outs, axis=0).T.astype(o_ref.dtype)


def _fox_prompt(qT, k2, vT, cT, *, tq):
    nb, t, _ = k2.shape
    whole_f = lambda w: pl.BlockSpec((1, w, t), lambda b, i: (b, 0, 0))
    whole_t = lambda w: pl.BlockSpec((1, t, w), lambda b, i: (b, 0, 0))
    return pl.pallas_call(
        functools.partial(_fox_prompt_kernel, tq=tq),
        grid=(nb, t // tq),
        in_specs=[pl.BlockSpec((1, ATTN_WIDTH, tq), lambda b, i: (b, 0, i)),
                  whole_t(N_HEADS * HEAD_SLOT), whole_f(ATTN_WIDTH), whole_f(N_HEADS)],
        out_specs=pl.BlockSpec((1, tq, ATTN_WIDTH), lambda b, i: (b, i, 0)),
        out_shape=jax.ShapeDtypeStruct((nb, t, ATTN_WIDTH), BF16),
        scratch_shapes=[pltpu.VMEM((N_HEADS, HEAD_SLOT, tq), BF16),
                        pltpu.VMEM((N_HEADS, 1, tq), F32), pltpu.VMEM((N_HEADS, HEAD_DIM + SUM_ROWS, tq), F32),
                        pltpu.VMEM((N_HEADS, tq, tq), F32), pltpu.VMEM((N_HEADS, tq, tq), F32)],
        compiler_params=_params("arbitrary", "arbitrary"), name="fox_prompt")(qT, k2, vT, cT)


RING = 3
SAMPLE_SEQS_PER_STEP = 1

def _fox_sample_kernel(pt_ref, q_ref, kn_ref, vn_ref, lfn_ref, sl_ref, *refs, n_pages, nq, n_seq):
    kc_ref, vc_ref = refs[0], refs[1]
    pf_refs = refs[2:2 + n_pages]
    o_ref, kbuf, vbuf, sem = refs[2 + n_pages:]
    step, n_steps = pl.program_id(0), pl.num_programs(0)

    def copies(seq, slot):
        out = []
        for j in range(n_pages):
            page = pt_ref[seq, j]
            out.append(pltpu.make_async_copy(kc_ref.at[page], kbuf.at[slot, j], sem.at[slot, 0]))
            out.append(pltpu.make_async_copy(vc_ref.at[page], vbuf.at[slot, j], sem.at[slot, 1]))
        return out

    @pl.when(step == 0)
    def _():
        for first in range(RING - 1):
            for c in copies(first, first):
                c.start()

    @pl.when(step + RING - 1 < n_steps)
    def _():
        for c in copies(step + RING - 1, (step + RING - 1) % RING):
            c.start()

    slot = step % RING
    for c in copies(step, slot):
        c.wait()
    k_refs = [kbuf.at[slot, j] for j in range(n_pages)]
    v_refs = [vbuf.at[slot, j] for j in range(n_pages)]
    _fox_sample_one(0, q_ref, kn_ref, vn_ref, lfn_ref, sl_ref, k_refs, v_refs, pf_refs, o_ref, n_pages=n_pages, nq=nq)


def _fox_sample_one(e, q_ref, kn_ref, vn_ref, lfn_ref, sl_ref, k_refs, v_refs, pf_refs, o_ref, *, n_pages, nq):
    nr = nq * N_HEADS
    head_of_lane = lax.broadcasted_iota(jnp.int32, (N_HEADS, ATTN_WIDTH), 1) // HEAD_DIM
    own = head_of_lane == lax.broadcasted_iota(jnp.int32, (N_HEADS, ATTN_WIDTH), 0)
    q = q_ref[e]
    qbd32 = jnp.concatenate([jnp.where(own, q[i:i + 1, :], 0.0) for i in range(nq)], axis=0)
    qbd = qbd32.astype(BF16)
    rep = lambda e: jnp.concatenate([e] * nq, axis=0)
    rows = lambda j: slice(j * N_HEADS, (j + 1) * N_HEADS)

    lfn = lfn_ref[e]
    e_new = rep(_dot_exact01(lfn, sl_ref[...]))
    kn, vn = kn_ref[e], vn_ref[e]
    qidx = lax.broadcasted_iota(jnp.int32, (nr, 1), 0) // N_HEADS
    s_new = [jnp.where(qidx >= j, jnp.sum(qbd32 * kn[j:j + 1, :], axis=-1, keepdims=True) + e_new[:, j:j + 1], NEG_INF)
             for j in range(nq)]

    pf_all = jnp.concatenate([r[0, 0] for r in pf_refs], axis=0)
    sfx_all = _dot_exact01(pf_all, sl_ref[...])
    tot_all = sfx_all[:, 0:1] + pf_all[:, 0:1]
    tail = jnp.sum(lfn, axis=-1, keepdims=True)
    scores = [None] * n_pages
    for j in reversed(range(n_pages)):
        kt = k_refs[j][...].reshape(ATTN_WIDTH, PAGE_SIZE).astype(BF16)
        scores[j] = _dot(qbd, kt) + rep(sfx_all[rows(j)] + tail)
        tail = tail + tot_all[rows(j)]
    s_all = jnp.concatenate(scores, axis=1)
    m = jnp.max(s_all, axis=-1, keepdims=True)
    for s in s_new:
        m = jnp.maximum(m, s)
    p_all = jnp.exp(s_all - m)
    l = jnp.sum(p_all, axis=-1, keepdims=True)
    acc = jnp.zeros((nr, ATTN_WIDTH), F32)
    for j, s in enumerate(s_new):
        p = jnp.exp(s - m)
        l = l + p
        acc = acc + p * vn[j:j + 1, :]
    p_all = p_all.astype(BF16)
    for j in range(n_pages):
        vt = v_refs[j][...].reshape(ATTN_WIDTH, PAGE_SIZE).astype(BF16)
        acc = acc + _dot_nt(p_all[:, j * PAGE_SIZE:(j + 1) * PAGE_SIZE], vt)
    acc = acc / l
    for i in range(nq):
        o_ref[e, i:i + 1, :] = jnp.sum(jnp.where(own, acc[i * N_HEADS:(i + 1) * N_HEADS], 0.0), axis=0, keepdims=True)


def _fox_sample(q, k_new, v_new, logf_new, cache_k, cache_v, cache_logf, page_table):
    db, nq, _ = q.shape
    n_pages = page_table.shape[1]
    kc = jnp.transpose(cache_k, (0, 2, 3, 1))
    vc = jnp.transpose(cache_v, (0, 2, 3, 1))
    pf = jnp.transpose(cache_logf, (0, 2, 1))
    lfn_t = jnp.swapaxes(jnp.pad(logf_new, ((0, 0), (0, PAGE_SIZE - nq), (0, 0))), 1, 2)
    strict_lower = (jnp.arange(PAGE_SIZE)[:, None] > jnp.arange(PAGE_SIZE)[None, :]).astype(BF16)
    ns = SAMPLE_SEQS_PER_STEP
    seq = lambda r, w: pl.BlockSpec((ns, r, w), lambda b, pt: (b, 0, 0))
    page = lambda e, j: pl.BlockSpec((1, 1, N_HEADS, HEAD_DIM, PAGE_SIZE), lambda b, pt: (0, pt[ns * b + e, j], 0, 0, 0))
    page_f = lambda e, j: pl.BlockSpec((1, 1, N_HEADS, PAGE_SIZE), lambda b, pt: (0, pt[ns * b + e, j], 0, 0))
    every = [(e, j) for e in range(ns) for j in range(n_pages)]
    in_specs = [seq(nq, ATTN_WIDTH), seq(nq, ATTN_WIDTH), seq(nq, ATTN_WIDTH), seq(N_HEADS, PAGE_SIZE),
                pl.BlockSpec((PAGE_SIZE, PAGE_SIZE), lambda b, pt: (0, 0))]
    in_specs += [pl.BlockSpec(memory_space=pl.ANY)] * 2 + [page_f(e, j) for e, j in every]
    return pl.pallas_call(
        functools.partial(_fox_sample_kernel, n_pages=n_pages, nq=nq, n_seq=ns),
        grid_spec=pltpu.PrefetchScalarGridSpec(
            num_scalar_prefetch=1, grid=(db // ns,), in_specs=in_specs, out_specs=seq(nq, ATTN_WIDTH),
            scratch_shapes=[pltpu.VMEM((RING, n_pages, N_HEADS, HEAD_DIM, PAGE_SIZE), F32),
                            pltpu.VMEM((RING, n_pages, N_HEADS, HEAD_DIM, PAGE_SIZE), F32),
                            pltpu.SemaphoreType.DMA((RING, 2))]),
        out_shape=jax.ShapeDtypeStruct((db, nq, ATTN_WIDTH), F32),
        compiler_params=_params("arbitrary"), name="fox_sample")(
            page_table, q, k_new, v_new, lfn_t, strict_lower,
            kc, vc, *([pf[None]] * len(every)))


PREP_GROUPS = 2

def _cmul(ar, ai, br, bi):
    return ar * br - ai * bi, ar * bi + ai * br


def _cpow_by_bits(ar, ai, n, nbits):
    pr = jnp.ones(jnp.broadcast_shapes(ar.shape, n.shape), F32)
    pi = jnp.zeros_like(pr)
    for b in range(nbits):
        bit = ((n >> b) & 1) == 1
        fr = jnp.where(bit, ar, 1.0)
        fi = jnp.where(bit, ai, 0.0)
        pr, pi = _cmul(pr, pi, fr, fi)
        ar, ai = _cmul(ar, ai, ar, ai)
    return pr, pi


def _ssm_prep_kernel(*refs, chunk, small, n_groups):
    for e in range(n_groups):
        _ssm_prep_group(*[r.at[pl.ds(e, 1)] for r in refs], chunk=chunk, small=small)


def _ssm_prep_group(ldt_ref, lr_ref, li_ref, bre_ref, bim_ref, cre_ref, cim_ref,
                    toep_ref, ctl_ref, obs_ref, al_ref,
                    toeps_ref, ctlrs_ref, ctlis_ref, obsrs_ref, obsis_ref, asr_ref, asi_ref, *, chunk, small):
    lw, sw = chunk * SSM_GROUP, small * SSM_GROUP
    p = STATE_DIM
    nbits = chunk.bit_length()
    dt = jnp.exp(ldt_ref[0])
    lr_row, li_row = lr_ref[0], li_ref[0]

    def discretise(lr, li):
        mag = jnp.exp(lr * dt)
        return mag * jnp.cos(li * dt), mag * jnp.sin(li * dt)

    def column(row):
        eye = lax.broadcasted_iota(jnp.int32, (p, p), 0) == lax.broadcasted_iota(jnp.int32, (p, p), 1)
        return jnp.sum(jnp.where(eye, row, 0.0), axis=-1, keepdims=True)

    lr, li = column(lr_row), column(li_row)
    ar, ai = discretise(lr, li)
    den = lr * lr + li * li
    nr, ni = ar - 1.0, ai
    cr, ci = (nr * lr + ni * li) / den, (ni * lr - nr * li) / den
    btr = jnp.concatenate([bre_ref[0]] * (2 * lw // LANES), axis=1)
    bti = jnp.concatenate([bim_ref[0]] * (2 * lw // LANES), axis=1)
    bbr, bbi = _cmul(cr, ci, btr, bti)
    sig = lax.broadcasted_iota(jnp.int32, (1, 2 * lw), 1) // SSM_GROUP
    er, ei = _cpow_by_bits(ar, ai, jnp.maximum(chunk - 1 - sig, 0), nbits)
    hr, hi = _cmul(er, ei, bbr, bbi)
    hr = jnp.where(sig < chunk, hr, 0.0)
    hi = jnp.where(sig < chunk, hi, 0.0)
    ctl_ref[0] = jnp.concatenate([hr[:, :lw], hi[:, :lw]], axis=0).astype(BF16)
    ctlrs_ref[0] = hr[:, lw - sw:lw]
    ctlis_ref[0] = hi[:, lw - sw:lw]
    c_re, c_im = cre_ref[0], cim_ref[0]
    hp = lax.Precision.HIGHEST
    k_all = (jnp.dot(c_re, hr, precision=hp, preferred_element_type=F32)
             - jnp.dot(c_im, hi, precision=hp, preferred_element_type=F32))
    for t in range(chunk):
        off = (chunk - 1 - t) * SSM_GROUP
        toep_ref[0, t * SSM_GROUP:(t + 1) * SSM_GROUP, :] = k_all[:, off:off + lw].astype(BF16)
        if t < small:
            toeps_ref[0, t * SSM_GROUP:(t + 1) * SSM_GROUP, :] = k_all[:, off:off + sw]

    arr, air = discretise(lr_row, li_row)
    tp1 = lax.broadcasted_iota(jnp.int32, (lw, 1), 0) // SSM_GROUP + 1
    pr, pi = _cpow_by_bits(arr, air, tp1, nbits)
    c_r = jnp.concatenate([c_re] * chunk, axis=0)
    c_i = jnp.concatenate([c_im] * chunk, axis=0)
    obs_re = c_r * pr - c_i * pi
    obs_im = -(c_r * pi + c_i * pr)
    obs_ref[0] = jnp.concatenate([obs_re, obs_im], axis=1).astype(BF16)
    obsrs_ref[0] = obs_re[:sw]
    obsis_ref[0] = obs_im[:sw]
    n = 1
    while n < chunk:
        arr, air = _cmul(arr, air, arr, air)
        n *= 2
        if n == small:
            asr_ref[0] = arr
            asi_ref[0] = air
    al_ref[0] = jnp.concatenate([jnp.concatenate([arr, arr], axis=1), jnp.concatenate([-air, air], axis=1)], axis=0)


def _ssm_prep(lam_re, lam_im, log_dt, b_re, b_im, c_re, c_im, *, chunk, small):
    assert chunk & (chunk - 1) == 0 and small & (small - 1) == 0 and 1 < small < chunk
    g, p, j = N_GROUPS, STATE_DIM, SSM_GROUP
    lw, sw = chunk * j, small * j
    lane_tile = lambda b: jnp.tile(b, (1, 1, LANES // j))
    ins = [log_dt.reshape(g, 1, 1), lam_re.reshape(g, 1, p), lam_im.reshape(g, 1, p), lane_tile(b_re), lane_tile(b_im),
           c_re, c_im]
    grp = lambda a, b: pl.BlockSpec((PREP_GROUPS, a, b), lambda i: (i, 0, 0))
    in_specs = [grp(1, 1), grp(1, p), grp(1, p), grp(p, LANES), grp(p, LANES), grp(j, p), grp(j, p)]
    shapes = [((lw, lw), BF16), ((2 * p, lw), BF16), ((lw, 2 * p), BF16), ((2, 2 * p), F32),
              ((sw, sw), F32), ((p, sw), F32), ((p, sw), F32), ((sw, p), F32), ((sw, p), F32), ((1, p), F32), ((1, p), F32)]
    outs = pl.pallas_call(
        functools.partial(_ssm_prep_kernel, chunk=chunk, small=small, n_groups=PREP_GROUPS),
        grid=(g // PREP_GROUPS,), in_specs=in_specs,
        out_specs=[grp(*s) for s, _ in shapes],
        out_shape=[jax.ShapeDtypeStruct((g,) + s, d) for s, d in shapes],
        compiler_params=_params("arbitrary"), name="ssm_prep")(*ins)
    return outs[:4], outs[4:]


def _ssm_step_kernel(u_ref, toep_ref, ctlr_ref, ctli_ref, obsr_ref, obsi_ref, ar_ref, ai_ref, h0r_ref, h0i_ref,
                     y_ref, hr_ref, hi_ref, *, chunk):
    n = GROUPS_PER_TILE
    n_seq = h0r_ref.shape[1]
    lw = chunk * SSM_GROUP
    zero = jnp.zeros((n_seq, LANES), F32)
    xs = [u_ref[0, 0, pl.ds(t, n_seq, stride=chunk), :] if t < chunk else zero for t in range(n)]
    ys = []
    for e, x in enumerate(_block_transpose(xs)):
        u = x[:, :lw].astype(BF16)
        h0r, h0i = h0r_ref[e], h0i_ref[e]
        er, ei = _cmul(ar_ref[e], ai_ref[e], h0r, h0i)
        hr_ref[e] = _dot_nt(u, ctlr_ref[e].astype(BF16)) + er
        hi_ref[e] = _dot_nt(u, ctli_ref[e].astype(BF16)) + ei
        y = (_dot_nt(u, toep_ref[e].astype(BF16)) + _dot_nt(h0r.astype(BF16), obsr_ref[e].astype(BF16))
             + _dot_nt(h0i.astype(BF16), obsi_ref[e].astype(BF16)))
        ys.append(jnp.concatenate([y, jnp.zeros((n_seq, LANES - lw), F32)], axis=1))
    for t, y in enumerate(_block_transpose(ys)[:chunk]):
        y_ref[0, 0, pl.ds(t, n_seq, stride=chunk), :] = y


def _ssm_step(u, ops, h0, *, chunk):
    n_seq = h0[0].shape[0]
    g, p, n = N_GROUPS, STATE_DIM, GROUPS_PER_TILE
    lw = chunk * SSM_GROUP
    assert lw <= LANES and u.shape == (1, g // n, n_seq * chunk, LANES)
    grp = lambda a, b: pl.BlockSpec((n, a, b), lambda v: (v, 0, 0))
    tile = pl.BlockSpec((1, 1, n_seq * chunk, LANES), lambda v: (0, v, 0, 0))
    y, hr, hi = pl.pallas_call(
        functools.partial(_ssm_step_kernel, chunk=chunk), grid=(g // n,),
        in_specs=[tile, grp(lw, lw), grp(p, lw), grp(p, lw), grp(lw, p), grp(lw, p), grp(1, p), grp(1, p),
                  grp(n_seq, p), grp(n_seq, p)],
        out_specs=[tile, grp(n_seq, p), grp(n_seq, p)],
        out_shape=[jax.ShapeDtypeStruct(u.shape, F32), jax.ShapeDtypeStruct((g, n_seq, p), F32),
                   jax.ShapeDtypeStruct((g, n_seq, p), F32)],
        compiler_params=_params("arbitrary"), name="ssm_step")(
            u, *ops, jnp.swapaxes(h0[0], 0, 1), jnp.swapaxes(h0[1], 0, 1))
    return y, jnp.swapaxes(hr, 0, 1), jnp.swapaxes(hi, 0, 1)


GROUPS_PER_TILE = LANES // SSM_GROUP
GROUPS_PER_TRIP = 8


def _block_transpose(xs):
    n = len(xs)
    blk = lax.broadcasted_iota(jnp.int32, xs[0].shape, 1) // SSM_GROUP
    xs = list(xs)
    d = n // 2
    while d:
        upper = (blk & d) != 0
        for i in range(n):
            if not i & d:
                lo, hi = xs[i], xs[i + d]
                xs[i] = jnp.where(upper, pltpu.roll(hi, SSM_GROUP * d, axis=1), lo)
                xs[i + d] = jnp.where(upper, hi, pltpu.roll(lo, LANES - SSM_GROUP * d, axis=1))
        d //= 2
    return xs


def _ssm_seq_kernel(u_ref, toep_ref, ctl_ref, obs_ref, al_ref, y_ref, h_ref, ug_ref, yg_ref, *, chunk):
    t = u_ref.shape[2]
    rows = t // chunk
    lw = chunk * SSM_GROUP
    n = GROUPS_PER_TILE
    p = STATE_DIM
    ridx = lax.broadcasted_iota(jnp.int32, (rows, 1), 0)

    def shifted(x, d):
        return jnp.where(ridx >= d, pltpu.roll(x, d, axis=0), 0.0)

    def group_tile(v, carry):
        for w in range(lw // LANES):
            xs = [u_ref[0, v, pl.ds(n * w + k, rows, stride=chunk), :] for k in range(n)]
            for k, x in enumerate(_block_transpose(xs)):
                ug_ref[k, :, w * LANES:(w + 1) * LANES] = x.astype(BF16)

        def groups(i, c):
            ks = [i * GROUPS_PER_TRIP + e for e in range(GROUPS_PER_TRIP)]
            gs = [v * n + k for k in ks]
            ugs = [ug_ref[k] for k in ks]
            ys = [_dot_nt(ug, toep_ref[g]) for ug, g in zip(ugs, gs)]
            hs = [_dot_nt(ug, ctl_ref[g]) for ug, g in zip(ugs, gs)]
            ars = [al_ref[g][0:1, :] for g in gs]
            ais = [al_ref[g][1:2, :] for g in gs]
            d = 1
            while d < rows:
                for e in range(GROUPS_PER_TRIP):
                    sh = shifted(hs[e], d)
                    hs[e] = hs[e] + ars[e] * sh + ais[e] * pltpu.roll(sh, p, axis=1)
                    ars[e], ais[e] = ars[e] * ars[e] - ais[e] * ais[e], 2.0 * ars[e] * ais[e]
                d *= 2
            for e, (k, g) in enumerate(zip(ks, gs)):
                h_ref[0, pl.ds(g, 1), :] = hs[e][rows - 1:rows, :]
                yg_ref[k] = ys[e] + _dot_nt(shifted(hs[e], 1).astype(BF16), obs_ref[g])
            return c

        lax.fori_loop(0, n // GROUPS_PER_TRIP, groups, 0)
        for w in range(lw // LANES):
            ys = [yg_ref[k, :, w * LANES:(w + 1) * LANES] for k in range(n)]
            for k, y in enumerate(_block_transpose(ys)):
                y_ref[0, v, pl.ds(n * w + k, rows, stride=chunk), :] = y
        return carry

    lax.fori_loop(0, N_GROUPS // n, group_tile, 0)


def _ssm_seq(u, ops, *, chunk):
    n_seq, _, t, _ = u.shape
    g, p = N_GROUPS, STATE_DIM
    lw = chunk * SSM_GROUP
    assert lw % LANES == 0 and chunk == GROUPS_PER_TILE * (lw // LANES) and SSM_WIDTH == g * SSM_GROUP
    whole = lambda a: pl.BlockSpec(a.shape, lambda b: (0, 0, 0), pipeline_mode=pl.Buffered(1))
    seq = pl.BlockSpec((1, SSM_WIDTH // LANES, t, LANES), lambda b: (b, 0, 0, 0))
    y, h = pl.pallas_call(
        functools.partial(_ssm_seq_kernel, chunk=chunk), grid=(n_seq,),
        in_specs=[seq] + [whole(o) for o in ops],
        out_specs=[seq, pl.BlockSpec((1, g, 2 * p), lambda b: (b, 0, 0))],
        out_shape=[jax.ShapeDtypeStruct(u.shape, F32), jax.ShapeDtypeStruct((n_seq, g, 2 * p), F32)],
        scratch_shapes=[pltpu.VMEM((GROUPS_PER_TILE, t // chunk, lw), BF16),
                        pltpu.VMEM((GROUPS_PER_TILE, t // chunk, lw), F32)],
        compiler_params=_params("arbitrary"), name="ssm_seq")(u, *ops)
    return y, h[:, :, :p], h[:, :, p:]


FF_CHUNK = 1024


def _post_kernel(x_ref, att_ref, ys_ref, u_ref, p_ref, lng_ref, lnb_ref, dsk_ref, wglu_ref, bglu_ref,
                 woa_ref, wos_ref, l1g_ref, l1b_ref, wup_ref, wdn_ref, wpe_ref, wpg_ref, bpg_ref,
                 l2g_ref, l2b_ref, o_ref):
    h = _layer_norm(x_ref[0], lng_ref[...], lnb_ref[...])
    y = _gelu_tanh(_load_lane_tiles(ys_ref) + dsk_ref[...] * _load_lane_tiles(u_ref))
    y = y * _sigmoid(_dot(y.astype(BF16), wglu_ref[...]) + bglu_ref[...])
    mix = _dot(att_ref[0], woa_ref[...]) + _dot(y.astype(BF16), wos_ref[...])
    h1 = _layer_norm(ALPHA * h + mix, l1g_ref[...], l1b_ref[...])
    h1b = h1.astype(BF16)
    e = _sigmoid(_dot(h1b, wpg_ref[...]) + bpg_ref[...]) * _dot(p_ref[0].astype(BF16), wpe_ref[...])
    acc = ALPHA * h1 + e
    for c in range(D_FF // FF_CHUNK):
        cs = slice(c * FF_CHUNK, (c + 1) * FF_CHUNK)
        a = jnp.maximum(_dot(h1b, wup_ref[:, cs]), 0.0)
        acc = acc + _dot((a * a).astype(BF16), wdn_ref[cs, :])
    o_ref[0] = _layer_norm(acc, l2g_ref[...], l2b_ref[...])


def _post_mixer(x, att, ys, u, p, weights, *, tm):
    nb, t, _ = x.shape
    tok = lambda w: pl.BlockSpec((1, tm, w), lambda b, i: (b, i, 0))
    const = lambda a: pl.BlockSpec(a.shape, lambda b, i: (0, 0), pipeline_mode=pl.Buffered(1))
    return pl.pallas_call(
        _post_kernel, grid=(nb, t // tm),
        in_specs=[tok(D_MODEL), tok(ATTN_WIDTH), _lane_tiles_spec(tm, SSM_WIDTH), _lane_tiles_spec(tm, SSM_WIDTH),
                  tok(PLE_DIM)] + [const(w) for w in weights],
        out_specs=tok(D_MODEL), out_shape=jax.ShapeDtypeStruct((nb, t, D_MODEL), F32),
        compiler_params=_params("arbitrary", "arbitrary"), name="post_mixer")(x, att, ys, u, p, *weights)


def kernel(x_prompt, x_sample, cache_k, cache_v, cache_logf, state_re, state_im, page_table, p_prompt, p_sample,
           ln_in_g, ln_in_b, w_in, b_f, lam_re, lam_im, log_dt, b_re, b_im, c_re, c_im, d_skip, w_glu, b_glu,
           w_out, ln1_g, ln1_b, w_up, w_down, w_pe, w_pg, b_pg, ln2_g, ln2_b):
    assert w_in.shape[0] == 1, "one trunk layer"
    nb, t, _ = x_prompt.shape
    db, nq, _ = x_sample.shape
    row = lambda a: a.reshape(1, -1)
    post_w = [row(ln_in_g), row(ln_in_b), row(d_skip[0]), w_glu[0].astype(BF16), row(b_glu[0]),
              w_out[0, :ATTN_WIDTH].astype(BF16), w_out[0, ATTN_WIDTH:].astype(BF16), row(ln1_g[0]), row(ln1_b[0]),
              w_up[0].astype(BF16), w_down[0].astype(BF16), w_pe[0].astype(BF16), w_pg[0].astype(BF16),
              row(b_pg[0]), row(ln2_g[0]), row(ln2_b[0])]
    ssm_par = (lam_re[0], lam_im[0], log_dt[0], b_re[0], b_im[0], c_re[0], c_im[0])

    u, kT, vT, lfT, cT, qTb, k2, vTb = _in_proj(x_prompt, ln_in_g, ln_in_b, w_in[0], b_f[0],
                                                     tm=IN_PROJ_ROWS, q_scale=QK_SCALE * LOG2E, attn_layouts=True)
    att = _fox_prompt(qTb, k2, vTb, cT, tq=ATTN_TILE)
    ssm_ops, ssm_ops_sample = _ssm_prep(*ssm_par, chunk=SSM_CHUNK, small=nq)
    ys, sr, si = _ssm_seq(u, ssm_ops, chunk=SSM_CHUNK)
    y_prompt = _post_mixer(x_prompt, att, ys, u, p_prompt[0], post_w, tm=POST_ROWS)
    heads_last = lambda a: jnp.transpose(a.reshape(nb, N_HEADS, HEAD_DIM, t), (0, 3, 1, 2))[None]
    prompt_out = (heads_last(kT), heads_last(vT), jnp.swapaxes(lfT, 1, 2)[None], sr[None], si[None])

    us, qs, ks, vs, logfs = _in_proj(x_sample.reshape(1, db * nq, D_MODEL), ln_in_g, ln_in_b, w_in[0], b_f[0],
                                     tm=db * nq, q_scale=QK_SCALE, attn_layouts=False)
    seq = lambda a: a.reshape(db, nq, a.shape[-1])
    att_s = _fox_sample(seq(qs), seq(ks), seq(vs), seq(logfs), cache_k[0], cache_v[0], cache_logf[0], page_table)
    ys_s, sr_s, si_s = _ssm_step(us, ssm_ops_sample, (state_re[0], state_im[0]), chunk=nq)
    flat = lambda a: a.reshape(1, db * nq, a.shape[-1])
    y_sample = _post_mixer(flat(x_sample), flat(att_s).astype(BF16), ys_s, us, flat(p_sample[0]),
                           post_w, tm=db * nq).reshape(db, nq, D_MODEL)
    sample_out = (ks.reshape(1, db, nq, N_HEADS, HEAD_DIM), vs.reshape(1, db, nq, N_HEADS, HEAD_DIM),
                  logfs.reshape(1, db, nq, N_HEADS), sr_s[None], si_s[None])
    return (y_prompt, y_sample) + prompt_out + sample_out
```
